```python
import math
import jax, jax.numpy as jnp
from jax import lax
import numpy as np

D_MODEL = 2048
BATCH = 4
SEQ = 4096
DEPTH = 1

GRID_W = 64
CTX_LEN = 256
Q_BLOCK = 128
ROPE_THETA = 10000.0
EPS = 1e-6
A_HEADS = 8
A_KV_HEADS = 2
A_HEAD_DIM = 128
B_HEADS = 8
B_NOPE = 128
B_ROPE = 64
B_V = 128
B_KV_RANK = 512
N_GROUPS = 4
EXPERTS_PER_GROUP = 4
N_EXPERTS = N_GROUPS * EXPERTS_PER_GROUP
TOP_K_IN_GROUP = 2
D_EXPERT = 1024
IN_SIZES = (A_HEADS * A_HEAD_DIM, A_KV_HEADS * A_HEAD_DIM, A_KV_HEADS * A_HEAD_DIM,
            B_HEADS * (B_NOPE + B_ROPE), B_KV_RANK, B_ROPE, 2 * D_MODEL)
D_IN = sum(IN_SIZES)
N_MOD = 6

kernel_name = 'hybrid_gqa_mla_hmoe_diffusion_block'


def _split_points():
    return [int(v) for v in np.cumsum(IN_SIZES)[:-1]]


def rmsnorm(x, g):
    xf = x.astype(jnp.float32)
    y = xf * lax.rsqrt(jnp.mean(xf * xf, axis=-1, keepdims=True) + EPS)
    return (y * g.astype(jnp.float32)).astype(x.dtype)


def modulate(h, shift, scale):
    return h * (1 + scale) + shift


def axial_rope_tables(n_tokens, rot_dim):
    rows = n_tokens // GRID_W
    row = jnp.repeat(jnp.arange(rows), GRID_W).astype(jnp.float32)
    col = jnp.tile(jnp.arange(GRID_W), rows).astype(jnp.float32)
    n_freq = rot_dim // 4
    freqs = ROPE_THETA ** (-jnp.arange(n_freq, dtype=jnp.float32) / n_freq)
    ang = jnp.concatenate([row[:, None] * freqs, col[:, None] * freqs], axis=-1)
    return jnp.cos(ang), jnp.sin(ang)


def apply_rope(x, cos, sin):
    shape = (1, cos.shape[0]) + (1,) * (x.ndim - 3) + (cos.shape[-1],)
    cos = cos.reshape(shape).astype(x.dtype)
    sin = sin.reshape(shape).astype(x.dtype)
    x1, x2 = jnp.split(x, 2, axis=-1)
    return jnp.concatenate([x1 * cos - x2 * sin, x1 * sin + x2 * cos], axis=-1)


def mixer_inputs(h, p, rope):
    B, S, _ = h.shape
    qa, ka, va, qb, ckv, kr, gl = jnp.split(h @ p['w_in'], _split_points(), axis=-1)
    qa = rmsnorm(qa.reshape(B, S, A_HEADS, A_HEAD_DIM), p['a_q_norm'])
    ka = rmsnorm(ka.reshape(B, S, A_KV_HEADS, A_HEAD_DIM), p['a_k_norm'])
    va = va.reshape(B, S, A_KV_HEADS, A_HEAD_DIM)
    qb = qb.reshape(B, S, B_HEADS, B_NOPE + B_ROPE)
    qbn, qbr = qb[..., :B_NOPE], qb[..., B_NOPE:]
    kv = (rmsnorm(ckv, p['b_kv_norm']) @ p['w_ukv']).reshape(B, S, B_HEADS, B_NOPE + B_V)
    kbn, vb = kv[..., :B_NOPE], kv[..., B_NOPE:]
    if rope is not None:
        cos_a, sin_a, cos_b, sin_b = rope
        qa = apply_rope(qa, cos_a, sin_a)
        ka = apply_rope(ka, cos_a, sin_a)
        qbr = apply_rope(qbr, cos_b, sin_b)
        kr = apply_rope(kr, cos_b, sin_b)
    qa = qa.reshape(B, S, A_KV_HEADS, A_HEADS // A_KV_HEADS, A_HEAD_DIM)
    return {'qa': qa, 'ka': ka, 'va': va, 'qbn': qbn, 'qbr': qbr,
            'kbn': kbn, 'kr': kr, 'vb': vb, 'gl': gl}


def gqa_attend(q, k, v):
    s = jnp.einsum('bqkgd,bskd->bkgqs', q.astype(jnp.float32), k.astype(jnp.float32))
    pr = jax.nn.softmax(s / math.sqrt(A_HEAD_DIM), axis=-1)
    return jnp.einsum('bkgqs,bskd->bqkgd', pr, v.astype(jnp.float32)).astype(v.dtype)


def mla_attend(qn, qr, kn, kr, v):
    f32 = jnp.float32
    s = (jnp.einsum('bqhd,bshd->bhqs', qn.astype(f32), kn.astype(f32))
         + jnp.einsum('bqhd,bsd->bhqs', qr.astype(f32), kr.astype(f32)))
    pr = jax.nn.softmax(s / math.sqrt(B_NOPE + B_ROPE), axis=-1)
    return jnp.einsum('bhqs,bshd->bqhd', pr, v.astype(f32)).astype(v.dtype)


def blocked(fn, qs, *kv):
    def to_blocks(t):
        B, S = t.shape[:2]
        return jnp.moveaxis(t.reshape((B, S // Q_BLOCK, Q_BLOCK) + t.shape[2:]), 1, 0)
    out = lax.map(lambda qb: fn(*qb, *kv), tuple(to_blocks(q) for q in qs))
    nb, B = out.shape[:2]
    return jnp.moveaxis(out, 0, 1).reshape((B, nb * Q_BLOCK) + out.shape[3:])


def merge_branches(oa, ob, gl, p):
    B, S = oa.shape[:2]
    ya = oa.reshape(B, S, -1) @ p['w_br_a']
    yb = ob.reshape(B, S, -1) @ p['w_br_b']
    ga, gb = jnp.split(jax.nn.sigmoid(gl), 2, axis=-1)
    return (ga * ya + gb * yb) @ p['w_out']


def hier_moe(h, p):
    B, S, D = h.shape
    t = h.reshape(-1, D)
    g_prob = jax.nn.softmax((t @ p['w_group'] + p['b_group']).astype(jnp.float32), axis=-1)
    g_val, g_idx = lax.top_k(g_prob, 1)
    e_all = jnp.einsum('nd,gde->nge', t, p['w_router']) + p['b_router']
    e_logits = jnp.take_along_axis(e_all, g_idx[:, :, None], axis=1)[:, 0].astype(jnp.float32)
    e_val, e_idx = lax.top_k(jax.nn.softmax(e_logits, axis=-1), TOP_K_IN_GROUP)
    w = g_val * e_val / jnp.sum(e_val, axis=-1, keepdims=True)
    eid = g_idx * EXPERTS_PER_GROUP + e_idx
    gate = jnp.sum(jax.nn.one_hot(eid, N_EXPERTS, dtype=jnp.float32) * w[..., None], axis=1)
    gate = gate.astype(t.dtype)
    y = jnp.zeros_like(t)
    for e in range(N_EXPERTS):
        hid = jax.nn.silu(t @ p['w_e_gate'][e]) * (t @ p['w_e_up'][e])
        y = y + gate[:, e:e + 1] * (hid @ p['w_e_down'][e])
    return y.reshape(B, S, D)


def setup_inputs(seed: int = 0) -> dict:
    key = jax.random.key(seed)
    ks = jax.random.split(key, 24)
    nrm = jax.random.normal
    f32 = jnp.float32
    D = D_MODEL
    return {
        'x': nrm(ks[0], (BATCH, SEQ, D), f32),
        'c': nrm(ks[1], (BATCH, D), f32),
        'ctx': nrm(ks[2], (BATCH, CTX_LEN, D), f32),
        'c_ctx': nrm(ks[3], (D,), f32),
        'w_mod': nrm(ks[4], (DEPTH, D, N_MOD * D), f32) * (0.5 * D ** -0.5),
        'b_mod': nrm(ks[5], (DEPTH, N_MOD * D), f32) * 0.02,
        'norm_mix': 1.0 + 0.05 * nrm(ks[6], (DEPTH, D), f32),
        'norm_ffn': 1.0 + 0.05 * nrm(ks[7], (DEPTH, D), f32),
        'w_in': nrm(ks[8], (DEPTH, D, D_IN), f32) * D ** -0.5,
        'a_q_norm': 1.0 + 0.05 * nrm(ks[9], (DEPTH, A_HEAD_DIM), f32),
        'a_k_norm': 1.0 + 0.05 * nrm(ks[10], (DEPTH, A_HEAD_DIM), f32),
        'b_kv_norm': 1.0 + 0.05 * nrm(ks[11], (DEPTH, B_KV_RANK), f32),
        'w_ukv': nrm(ks[12], (DEPTH, B_KV_RANK, B_HEADS * (B_NOPE + B_V)), f32) * B_KV_RANK ** -0.5,
        'w_br_a': nrm(ks[13], (DEPTH, A_HEADS * A_HEAD_DIM, D), f32) * (A_HEADS * A_HEAD_DIM) ** -0.5,
        'w_br_b': nrm(ks[14], (DEPTH, B_HEADS * B_V, D), f32) * (B_HEADS * B_V) ** -0.5,
        'w_out': nrm(ks[15], (DEPTH, D, D), f32) * D ** -0.5,
        'w_group': nrm(ks[16], (DEPTH, D, N_GROUPS), f32) * D ** -0.5,
        'b_group': nrm(ks[17], (DEPTH, N_GROUPS), f32) * 0.01,
        'w_router': nrm(ks[18], (DEPTH, N_GROUPS, D, EXPERTS_PER_GROUP), f32) * D ** -0.5,
        'b_router': nrm(ks[19], (DEPTH, N_GROUPS, EXPERTS_PER_GROUP), f32) * 0.01,
        'w_e_gate': nrm(ks[20], (DEPTH, N_EXPERTS, D, D_EXPERT), f32) * D ** -0.5,
        'w_e_up': nrm(ks[21], (DEPTH, N_EXPERTS, D, D_EXPERT), f32) * D ** -0.5,
        'w_e_down': nrm(ks[22], (DEPTH, N_EXPERTS, D_EXPERT, D), f32) * D_EXPERT ** -0.5,
        'norm_final': 1.0 + 0.05 * nrm(ks[23], (D,), f32),
    }


def reference(x, c, ctx, c_ctx, w_mod, b_mod, norm_mix, norm_ffn, w_in, a_q_norm, a_k_norm,
              b_kv_norm, w_ukv, w_br_a, w_br_b, w_out, w_group, b_group, w_router, b_router,
              w_e_gate, w_e_up, w_e_down, norm_final):
    B, S, D = x.shape
    rope = (*axial_rope_tables(S, A_HEAD_DIM), *axial_rope_tables(S, B_ROPE))
    silu_c = jax.nn.silu(c)
    silu_cc = jax.nn.silu(c_ctx)
    for l in range(DEPTH):
        p = {'w_in': w_in[l], 'a_q_norm': a_q_norm[l], 'a_k_norm': a_k_norm[l],
             'b_kv_norm': b_kv_norm[l], 'w_ukv': w_ukv[l], 'w_br_a': w_br_a[l],
             'w_br_b': w_br_b[l], 'w_out': w_out[l], 'w_group': w_group[l],
             'b_group': b_group[l], 'w_router': w_router[l], 'b_router': b_router[l],
             'w_e_gate': w_e_gate[l], 'w_e_up': w_e_up[l], 'w_e_down': w_e_down[l]}
        mod_x = (silu_c @ w_mod[l] + b_mod[l])[:, None, :]
        mod_c = (silu_cc @ w_mod[l] + b_mod[l])[None, None, :]
        sh1, sc1, g1, sh2, sc2, g2 = jnp.split(mod_x, N_MOD, axis=-1)
        csh1, csc1, cg1, csh2, csc2, cg2 = jnp.split(mod_c, N_MOD, axis=-1)

        hx = modulate(rmsnorm(x, norm_mix[l]), sh1, sc1)
        hc = modulate(rmsnorm(ctx, norm_mix[l]), csh1, csc1)
        mx = mixer_inputs(hx, p, rope)
        mc = mixer_inputs(hc, p, None)

        ka = jnp.concatenate([mc['ka'], mx['ka']], axis=1)
        va = jnp.concatenate([mc['va'], mx['va']], axis=1)
        kbn = jnp.concatenate([mc['kbn'], mx['kbn']], axis=1)
        kr = jnp.concatenate([mc['kr'], mx['kr']], axis=1)
        vb = jnp.concatenate([mc['vb'], mx['vb']], axis=1)
        oa = blocked(gqa_attend, (mx['qa'],), ka, va)
        ob = blocked(mla_attend, (mx['qbn'], mx['qbr']), kbn, kr, vb)
        x_new = x + g1 * merge_branches(oa, ob, mx['gl'], p)
        x_new = x_new + g2 * hier_moe(modulate(rmsnorm(x_new, norm_ffn[l]), sh2, sc2), p)

        if l + 1 < DEPTH:
            oac = gqa_attend(mc['qa'], mc['ka'], mc['va'])
            obc = mla_attend(mc['qbn'], mc['qbr'], mc['kbn'], mc['kr'], mc['vb'])
            ctx = ctx + cg1 * merge_branches(oac, obc, mc['gl'], p)
            ctx = ctx + cg2 * hier_moe(modulate(rmsnorm(ctx, norm_ffn[l]), csh2, csc2), p)
        x = x_new
    return rmsnorm(x, norm_final)
```

```python
import functools
import math

import jax
import jax.numpy as jnp
import numpy as np
from jax import lax
from jax.experimental import pallas as pl
from jax.experimental.pallas import tpu as pltpu

GRID_W = 64
ROPE_THETA = 10000.0
EPS = 1e-6
A_HEADS = 8
A_KV_HEADS = 2
A_HEAD_DIM = 128
B_HEADS = 8
B_NOPE = 128
B_ROPE = 64
B_V = 128
B_KV_RANK = 512
N_GROUPS = 4
EXPERTS_PER_GROUP = 4
N_EXPERTS = N_GROUPS * EXPERTS_PER_GROUP
D_EXPERT = 1024
N_MOD = 6

LANES = 128
V7X_VMEM_LIMIT_BYTES = 56 * 1024 * 1024

BF16 = jnp.bfloat16
F32 = jnp.float32

P_GA = 0
P_GB = 2048
P_QA = 4096
P_KA = 5120
P_VA = 5376
P_QBN = 5632
P_QBR = 6656
P_COLS = 7168
PROJ_TN = 512


def _cparams(*sem):
    return pltpu.CompilerParams(dimension_semantics=sem, vmem_limit_bytes=V7X_VMEM_LIMIT_BYTES)


def _dot(a, b):
    return jnp.dot(a, b, preferred_element_type=F32)


def _dot_nt(a, b):
    return lax.dot_general(a, b, (((1,), (1,)), ((), ())), preferred_element_type=F32)


def _resident(shape):
    return pl.BlockSpec(shape, lambda *_: (0,) * len(shape), pipeline_mode=pl.Buffered(1))


def _rms(v, gain):
    return v * lax.rsqrt(jnp.mean(v * v, axis=-1, keepdims=True) + EPS) * gain


def _mod_kernel(c_ref, w_ref, b_ref, o_ref):
    c = c_ref[...]
    s = (c * jax.nn.sigmoid(c)).astype(BF16)
    o_ref[...] = _dot(s, w_ref[...].astype(BF16)) + b_ref[...]


def _modulation(cond, w_mod, b_mod):
    rows, d = cond.shape
    n = w_mod.shape[1]
    tn = 1024
    return pl.pallas_call(
        _mod_kernel,
        grid=(n // tn,),
        in_specs=[pl.BlockSpec((rows, d), lambda j: (0, 0)),
                  pl.BlockSpec((d, tn), lambda j: (0, j)),
                  pl.BlockSpec((1, tn), lambda j: (0, j))],
        out_specs=pl.BlockSpec((rows, tn), lambda j: (0, j)),
        out_shape=jax.ShapeDtypeStruct((rows, n), F32),
        compiler_params=_cparams("parallel"),
        name="modulation",
    )(cond, w_mod, b_mod.reshape(1, n))


def _swap_halves_64(v):
    lane = lax.broadcasted_iota(jnp.int32, v.shape, 1)
    return jnp.where((lane & 63) < 32, pltpu.roll(v, LANES - 32, 1), pltpu.roll(v, 32, 1))


def _pre_kernel(use_rope, x_ref, sh_ref, sc_ref, g_ref, wck_ref, gkv_ref, wk_ref, wv_ref,
                cb_ref, sb_ref, h_ref, kb_ref, vb_ref):
    xf = x_ref[0]
    h = _rms(xf, g_ref[...]) * (1.0 + sc_ref[0]) + sh_ref[0]
    hb = h.astype(BF16)
    h_ref[0] = hb
    p = _dot(hb, wck_ref[...])
    cn = _rms(p[:, :B_KV_RANK], gkv_ref[...]).astype(BF16)
    kr2 = p[:, B_KV_RANK:]
    if use_rope:
        kr2 = kr2 * cb_ref[...] + _swap_halves_64(kr2) * sb_ref[...]
    kr2 = kr2.astype(BF16)
    kbn = _dot(cn, wk_ref[...]).astype(BF16)
    vb_ref[0] = _dot(cn, wv_ref[...]).astype(BF16)
    for hd in range(B_HEADS):
        kb_ref[0, :, hd * 256:hd * 256 + B_NOPE] = kbn[:, hd * B_NOPE:(hd + 1) * B_NOPE]
        kb_ref[0, :, hd * 256 + B_NOPE:(hd + 1) * 256] = kr2


def _prologue(x, shift, scale, gain, w_ck, g_kv, w_k, w_v, cos_b, sin_b, use_rope, tm):
    bt, st, d = x.shape
    grid = (bt, st // tm)
    row = lambda b, i: (b, i, 0)
    per_b = lambda b, i: (b, 0, 0)
    fixed = lambda b, i: (0, 0)
    return pl.pallas_call(
        functools.partial(_pre_kernel, use_rope),
        grid=grid,
        in_specs=[pl.BlockSpec((1, tm, d), row),
                  pl.BlockSpec((1, 1, d), per_b),
                  pl.BlockSpec((1, 1, d), per_b),
                  pl.BlockSpec((1, d), fixed),
                  _resident(w_ck.shape),
                  pl.BlockSpec((1, B_KV_RANK), fixed),
                  _resident(w_k.shape),
                  _resident(w_v.shape),
                  pl.BlockSpec((tm, LANES), lambda b, i: (i, 0)),
                  pl.BlockSpec((tm, LANES), lambda b, i: (i, 0))],
        out_specs=[pl.BlockSpec((1, tm, d), row),
                   pl.BlockSpec((1, tm, B_HEADS * 256), row),
                   pl.BlockSpec((1, tm, B_HEADS * B_V), row)],
        out_shape=[jax.ShapeDtypeStruct((bt, st, d), BF16),
                   jax.ShapeDtypeStruct((bt, st, B_HEADS * 256), BF16),
                   jax.ShapeDtypeStruct((bt, st, B_HEADS * B_V), BF16)],
        compiler_params=_cparams("parallel", "parallel"),
        name="prologue_rope" if use_rope else "prologue_ctx",
    )(x, shift, scale, gain, w_ck, g_kv, w_k, w_v, cos_b, sin_b)


def _proj_kernel(use_rope, h_ref, w_ref, gq_ref, gk_ref, ca_ref, sa_ref, cb_ref, sb_ref,
                 o_ref, acc_ref):
    j = pl.program_id(2)
    acc_ref[...] = _dot(h_ref[0], w_ref[...])
    n_blk = PROJ_TN // LANES

    def rope_a(v):
        if not use_rope:
            return v
        return v * ca_ref[...] + pltpu.roll(v, A_HEAD_DIM // 2, 1) * sa_ref[...]

    def rope_b(v):
        if not use_rope:
            return v
        return v * cb_ref[...] + _swap_halves_64(v) * sb_ref[...]

    def blk(k):
        return acc_ref[:, k * LANES:(k + 1) * LANES]

    def put(k, v):
        o_ref[0, :, k * LANES:(k + 1) * LANES] = v.astype(BF16)

    @pl.when(j < P_QA // PROJ_TN)
    def _():
        o_ref[0] = jax.nn.sigmoid(acc_ref[...]).astype(BF16)

    @pl.when((j >= P_QA // PROJ_TN) & (j < P_KA // PROJ_TN))
    def _():
        for k in range(n_blk):
            put(k, rope_a(_rms(blk(k), gq_ref[...])) * (1.0 / math.sqrt(A_HEAD_DIM)))

    @pl.when(j == P_KA // PROJ_TN)
    def _():
        for k in range(A_KV_HEADS):
            put(k, rope_a(_rms(blk(k), gk_ref[...])))
        for k in range(A_KV_HEADS, n_blk):
            put(k, blk(k))

    @pl.when((j >= P_QBN // PROJ_TN) & (j < P_QBR // PROJ_TN))
    def _():
        o_ref[0] = (acc_ref[...] * (1.0 / math.sqrt(B_NOPE + B_ROPE))).astype(BF16)

    @pl.when(j == P_QBR // PROJ_TN)
    def _():
        for k in range(n_blk):
            put(k, rope_b(blk(k)) * (1.0 / math.sqrt(B_NOPE + B_ROPE)))


def _projection(h, w_p, g_q, g_k, cos_a, sin_a, cos_b, sin_b, use_rope, tm):
    bt, st, d = h.shape
    grid = (bt, st // tm, P_COLS // PROJ_TN)
    fixed = lambda b, i, j: (0, 0)
    tab = lambda b, i, j: (i, 0)
    return pl.pallas_call(
        functools.partial(_proj_kernel, use_rope),
        grid=grid,
        in_specs=[pl.BlockSpec((1, tm, d), lambda b, i, j: (b, i, 0)),
                  pl.BlockSpec((d, PROJ_TN), lambda b, i, j: (0, j)),
                  pl.BlockSpec((1, LANES), fixed),
                  pl.BlockSpec((1, LANES), fixed),
                  pl.BlockSpec((tm, LANES), tab),
                  pl.BlockSpec((tm, LANES), tab),
                  pl.BlockSpec((tm, LANES), tab),
                  pl.BlockSpec((tm, LANES), tab)],
        out_specs=pl.BlockSpec((1, tm, PROJ_TN), lambda b, i, j: (b, i, j)),
        out_shape=jax.ShapeDtypeStruct((bt, st, P_COLS), BF16),
        scratch_shapes=[pltpu.VMEM((tm, PROJ_TN), F32)],
        compiler_params=_cparams("parallel", "parallel", "arbitrary"),
        name="projection_rope" if use_rope else "projection_ctx",
    )(h, w_p, g_q, g_k, cos_a, sin_a, cos_b, sin_b)


def _flash(q, kc_ref, vc_ref, kl_ref, vl_ref, tk):
    s = _dot_nt(q, kc_ref[0])
    m0 = jnp.max(s, axis=-1, keepdims=True)
    p = jnp.exp(s - m0)
    l0 = jnp.sum(p, axis=-1, keepdims=True)
    a0 = _dot(p.astype(BF16), vc_ref[0])
    n_chunks = kl_ref.shape[1] // tk

    def body(c, carry):
        m, l, acc = carry
        off = pl.multiple_of(c * tk, tk)
        s = _dot_nt(q, kl_ref[0, pl.ds(off, tk), :])
        m_new = jnp.maximum(m, jnp.max(s, axis=-1, keepdims=True))
        alpha = jnp.exp(m - m_new)
        p = jnp.exp(s - m_new)
        l = alpha * l + jnp.sum(p, axis=-1, keepdims=True)
        acc = alpha * acc + _dot(p.astype(BF16), vl_ref[0, pl.ds(off, tk), :])
        return m_new, l, acc

    _, l, acc = lax.fori_loop(0, n_chunks, body, (m0, l0, a0))
    return acc / l


def _gqa_kernel(tk, q_ref, kl_ref, vl_ref, kc_ref, vc_ref, o_ref):
    group = A_HEADS // A_KV_HEADS
    tq = q_ref.shape[1]
    q = jnp.concatenate([q_ref[0, :, g * A_HEAD_DIM:(g + 1) * A_HEAD_DIM] for g in range(group)],
                        axis=0)
    out = _flash(q, kc_ref, vc_ref, kl_ref, vl_ref, tk)
    for g in range(group):
        o_ref[0, :, g * A_HEAD_DIM:(g + 1) * A_HEAD_DIM] = out[g * tq:(g + 1) * tq].astype(BF16)


def _gqa_attention(p_lat, p_ctx, tq, tk):
    b, s, _ = p_lat.shape
    n_ctx = p_ctx.shape[1]
    group_w = (A_HEADS // A_KV_HEADS) * A_HEAD_DIM
    return pl.pallas_call(
        functools.partial(_gqa_kernel, tk),
        grid=(b, A_KV_HEADS, s // tq),
        in_specs=[pl.BlockSpec((1, tq, group_w), lambda bb, k, i: (bb, i, P_QA // group_w + k)),
                  pl.BlockSpec((1, s, LANES), lambda bb, k, i: (bb, 0, P_KA // LANES + k)),
                  pl.BlockSpec((1, s, LANES), lambda bb, k, i: (bb, 0, P_VA // LANES + k)),
                  pl.BlockSpec((1, n_ctx, LANES), lambda bb, k, i: (bb, 0, P_KA // LANES + k)),
                  pl.BlockSpec((1, n_ctx, LANES), lambda bb, k, i: (bb, 0, P_VA // LANES + k))],
        out_specs=pl.BlockSpec((1, tq, group_w), lambda bb, k, i: (bb, i, k)),
        out_shape=jax.ShapeDtypeStruct((b, s, A_HEADS * A_HEAD_DIM), BF16),
        compiler_params=_cparams("parallel", "parallel", "parallel"),
        name="gqa_attention",
    )(p_lat, p_lat, p_lat, p_ctx, p_ctx)


def _mla_kernel(tk, qn_ref, qr_ref, kl_ref, vl_ref, kc_ref, vc_ref, o_ref):
    hd = pl.program_id(1)
    qr = qr_ref[0]
    lane = lax.broadcasted_iota(jnp.int32, qr.shape, 1)
    qr = jnp.where((lane >> 6) == (hd & 1), qr, jnp.zeros_like(qr))
    q = jnp.concatenate([qn_ref[0], qr], axis=-1)
    o_ref[0] = _flash(q, kc_ref, vc_ref, kl_ref, vl_ref, tk).astype(BF16)


def _mla_attention(p_lat, kb_lat, vb_lat, kb_ctx, vb_ctx, tq, tk):
    b, s, _ = p_lat.shape
    n_ctx = kb_ctx.shape[1]
    return pl.pallas_call(
        functools.partial(_mla_kernel, tk),
        grid=(b, B_HEADS, s // tq),
        in_specs=[pl.BlockSpec((1, tq, LANES), lambda bb, h, i: (bb, i, P_QBN // LANES + h)),
                  pl.BlockSpec((1, tq, LANES), lambda bb, h, i: (bb, i, P_QBR // LANES + h // 2)),
                  pl.BlockSpec((1, s, 256), lambda bb, h, i: (bb, 0, h)),
                  pl.BlockSpec((1, s, B_V), lambda bb, h, i: (bb, 0, h)),
                  pl.BlockSpec((1, n_ctx, 256), lambda bb, h, i: (bb, 0, h)),
                  pl.BlockSpec((1, n_ctx, B_V), lambda bb, h, i: (bb, 0, h))],
        out_specs=pl.BlockSpec((1, tq, B_V), lambda bb, h, i: (bb, i, h)),
        out_shape=jax.ShapeDtypeStruct((b, s, B_HEADS * B_V), BF16),
        compiler_params=_cparams("parallel", "parallel", "parallel"),
        name="mla_attention",
    )(p_lat, p_lat, kb_lat, vb_lat, kb_ctx, vb_ctx)


def _merge_kernel(oa_ref, ob_ref, ga_ref, gb_ref, x_ref, g1_ref, sh_ref, sc_ref, gn_ref,
                  wa_ref, wb_ref, wo_ref, wr_ref, br_ref, xn_ref, h2_ref, rt_ref):
    ya = _dot(oa_ref[0], wa_ref[...])
    yb = _dot(ob_ref[0], wb_ref[...])
    mix = (ga_ref[0].astype(F32) * ya + gb_ref[0].astype(F32) * yb).astype(BF16)
    xn = x_ref[0] + g1_ref[0] * _dot(mix, wo_ref[...])
    xn_ref[0] = xn
    h2 = _rms(xn, gn_ref[...]) * (1.0 + sc_ref[0]) + sh_ref[0]
    h2_ref[0] = h2
    logits = _dot(h2.astype(BF16), wr_ref[...]) + br_ref[...]
    lt = logits.T
    gl = [lt[g:g + 1, :] for g in range(N_GROUPS)]
    gmax = functools.reduce(jnp.maximum, gl)
    gsum = functools.reduce(lambda a, b_: a + b_, [jnp.exp(v - gmax) for v in gl])
    g_val = 1.0 / gsum
    g_idx = jnp.full(gmax.shape, N_GROUPS - 1, jnp.int32)
    for g in range(N_GROUPS - 2, -1, -1):
        g_idx = jnp.where(gl[g] == gmax, g, g_idx)
    el = []
    for e in range(EXPERTS_PER_GROUP):
        v = lt[N_GROUPS + e:N_GROUPS + e + 1, :]
        for g in range(1, N_GROUPS):
            row = N_GROUPS + g * EXPERTS_PER_GROUP + e
            v = jnp.where(g_idx == g, lt[row:row + 1, :], v)
        el.append(v)
    emax = functools.reduce(jnp.maximum, el)
    i1 = jnp.full(emax.shape, EXPERTS_PER_GROUP - 1, jnp.int32)
    for e in range(EXPERTS_PER_GROUP - 2, -1, -1):
        i1 = jnp.where(el[e] == emax, e, i1)
    neg = jnp.full(emax.shape, -jnp.inf, F32)
    el2 = [jnp.where(i1 == e, neg, el[e]) for e in range(EXPERTS_PER_GROUP)]
    emax2 = functools.reduce(jnp.maximum, el2)
    i2 = jnp.full(emax.shape, EXPERTS_PER_GROUP - 1, jnp.int32)
    for e in range(EXPERTS_PER_GROUP - 2, -1, -1):
        i2 = jnp.where(el2[e] == emax2, e, i2)
    p2 = jnp.exp(emax2 - emax)
    w1 = g_val / (1.0 + p2)
    w2 = g_val * p2 / (1.0 + p2)
    e1 = (g_idx * EXPERTS_PER_GROUP + i1).astype(F32)
    e2 = (g_idx * EXPERTS_PER_GROUP + i2).astype(F32)
    zero = jnp.zeros_like(w1)
    rt_ref[...] = jnp.concatenate([e1, e2, w1, w2, zero, zero, zero, zero], axis=0)


def _merge_route(oa, ob, p_lat, x, g1, sh2, sc2, g_ffn, w_a, w_b, w_o, w_r, b_r, tm):
    b, s, d = x.shape
    n_i = s // tm
    row = lambda bb, i: (bb, i, 0)
    per_b = lambda bb, i: (bb, 0, 0)
    fixed = lambda bb, i: (0, 0)
    return pl.pallas_call(
        _merge_kernel,
        grid=(b, n_i),
        in_specs=[pl.BlockSpec((1, tm, oa.shape[2]), row),
                  pl.BlockSpec((1, tm, ob.shape[2]), row),
                  pl.BlockSpec((1, tm, d), lambda bb, i: (bb, i, P_GA // d)),
                  pl.BlockSpec((1, tm, d), lambda bb, i: (bb, i, P_GB // d)),
                  pl.BlockSpec((1, tm, d), row),
                  pl.BlockSpec((1, 1, d), per_b),
                  pl.BlockSpec((1, 1, d), per_b),
                  pl.BlockSpec((1, 1, d), per_b),
                  pl.BlockSpec((1, d), fixed),
                  _resident(w_a.shape),
                  _resident(w_b.shape),
                  _resident(w_o.shape),
                  _resident(w_r.shape),
                  pl.BlockSpec((1, LANES), fixed)],
        out_specs=[pl.BlockSpec((1, tm, d), row),
                   pl.BlockSpec((1, tm, d), row),
                   pl.BlockSpec((8, tm), lambda bb, i: (0, bb * n_i + i))],
        out_shape=[jax.ShapeDtypeStruct((b, s, d), F32),
                   jax.ShapeDtypeStruct((b, s, d), F32),
                   jax.ShapeDtypeStruct((8, b * s), F32)],
        compiler_params=_cparams("parallel", "parallel"),
        name="merge_route",
    )(oa, ob, p_lat, p_lat, x, g1, sh2, sc2, g_ffn, w_a, w_b, w_o, w_r, b_r)


def _moe_kernel(tm, te_ref, nv_ref, src_ref, dst_ref, h_hbm, ws_ref, wg_ref, wu_ref, wd_ref,
                y_hbm, xbuf, ybuf, gsem, ssem):
    t = pl.program_id(0)
    base = t * tm
    nv = nv_ref[t]

    @pl.when(t == 0)
    def _():
        xbuf[...] = jnp.zeros_like(xbuf)

    def gather(r):
        return pltpu.make_async_copy(h_hbm.at[pl.ds(src_ref[base + r], 1)],
                                     xbuf.at[pl.ds(r, 1)], gsem)

    def scatter(r):
        return pltpu.make_async_copy(ybuf.at[pl.ds(r, 1)],
                                     y_hbm.at[pl.ds(dst_ref[base + r], 1)], ssem)

    def start_all(make):
        def body(r, c):
            make(r).start()
            return c
        lax.fori_loop(0, nv, body, 0)

    def wait_all(make):
        def body(r, c):
            make(r).wait()
            return c
        lax.fori_loop(0, nv, body, 0)

    @pl.when(nv > 0)
    def _():
        start_all(gather)
        wait_all(gather)
        xb = xbuf[...].astype(BF16)
        gate = _dot(xb, wg_ref[0])
        up = _dot(xb, wu_ref[0])
        hid = (gate * jax.nn.sigmoid(gate) * up).astype(BF16)
        ybuf[...] = _dot(hid, wd_ref[0]) * ws_ref[...]
        start_all(scatter)
        wait_all(scatter)


def _moe(h2, tile_expert, tile_valid, src_tok, dst_row, w_sorted, w_gate, w_up, w_down, tm):
    n, d = h2.shape
    n_tiles = tile_expert.shape[0]
    grid_spec = pltpu.PrefetchScalarGridSpec(
        num_scalar_prefetch=4,
        grid=(n_tiles,),
        in_specs=[pl.BlockSpec(memory_space=pl.ANY),
                  pl.BlockSpec((tm, 1), lambda t, te, nv, sr, ds: (t, 0)),
                  pl.BlockSpec((1, d, D_EXPERT), lambda t, te, nv, sr, ds: (te[t], 0, 0)),
                  pl.BlockSpec((1, d, D_EXPERT), lambda t, te, nv, sr, ds: (te[t], 0, 0)),
                  pl.BlockSpec((1, D_EXPERT, d), lambda t, te, nv, sr, ds: (te[t], 0, 0))],
        out_specs=pl.BlockSpec(memory_space=pl.ANY),
        scratch_shapes=[pltpu.VMEM((tm, d), F32),
                        pltpu.VMEM((tm, d), F32),
                        pltpu.SemaphoreType.DMA(()),
                        pltpu.SemaphoreType.DMA(())],
    )
    return pl.pallas_call(
        functools.partial(_moe_kernel, tm),
        grid_spec=grid_spec,
        out_shape=jax.ShapeDtypeStruct((2 * n, d), F32),
        compiler_params=_cparams("arbitrary"),
        name="moe_experts",
    )(tile_expert, tile_valid, src_tok, dst_row, h2, w_sorted, w_gate, w_up, w_down)


def _moe_plan(eid, wts, tm):
    n = eid.shape[1]
    pairs = 2 * n
    n_tiles = pairs // tm + N_EXPERTS
    e_flat = eid.reshape(pairs)
    order = jnp.argsort(e_flat, stable=True).astype(jnp.int32)
    counts = jnp.sum(e_flat[None, :] == jnp.arange(N_EXPERTS, dtype=jnp.int32)[:, None],
                     axis=1).astype(jnp.int32)
    padded = ((counts + tm - 1) // tm) * tm
    pad_end = jnp.cumsum(padded)
    pad_start = pad_end - padded
    raw_start = jnp.cumsum(counts) - counts
    tile_row0 = jnp.arange(n_tiles, dtype=jnp.int32) * tm
    tile_expert = jnp.minimum(
        jnp.sum(tile_row0[:, None] >= pad_end[None, :], axis=1), N_EXPERTS - 1).astype(jnp.int32)
    in_range = tile_row0 < pad_end[-1]
    tile_valid = jnp.where(
        in_range, jnp.clip(counts[tile_expert] - (tile_row0 - pad_start[tile_expert]), 0, tm), 0
    ).astype(jnp.int32)
    pos = jnp.arange(n_tiles * tm, dtype=jnp.int32)
    pe = jnp.repeat(tile_expert, tm)
    off = pos - pad_start[pe]
    valid = (pos % tm) < jnp.repeat(tile_valid, tm)
    pair = jnp.where(valid, order[jnp.clip(raw_start[pe] + off, 0, pairs - 1)], 0)
    src_tok = (pair % n).astype(jnp.int32)
    dst_row = pair.astype(jnp.int32)
    w_sorted = jnp.where(valid, wts.reshape(pairs)[pair], 0.0).astype(F32)[:, None]
    return tile_expert, tile_valid, src_tok, dst_row, w_sorted


def _final_kernel(x_ref, y0_ref, y1_ref, g2_ref, gn_ref, o_ref):
    v = x_ref[0] + g2_ref[0] * (y0_ref[0, 0] + y1_ref[0, 0])
    o_ref[0] = _rms(v, gn_ref[...])


def _final(x_new, y2, g2, g_final, tm):
    b, s, d = x_new.shape
    row = lambda bb, i: (bb, i, 0)
    return pl.pallas_call(
        _final_kernel,
        grid=(b, s // tm),
        in_specs=[pl.BlockSpec((1, tm, d), row),
                  pl.BlockSpec((1, 1, tm, d), lambda bb, i: (0, bb, i, 0)),
                  pl.BlockSpec((1, 1, tm, d), lambda bb, i: (1, bb, i, 0)),
                  pl.BlockSpec((1, 1, d), lambda bb, i: (bb, 0, 0)),
                  pl.BlockSpec((1, d), lambda bb, i: (0, 0))],
        out_specs=pl.BlockSpec((1, tm, d), row),
        out_shape=jax.ShapeDtypeStruct((b, s, d), F32),
        compiler_params=_cparams("parallel", "parallel"),
        name="final_norm",
    )(x_new, y2, y2, g2, g_final)


def _rope_tables(n_tokens, rot_dim):
    rows = n_tokens // GRID_W
    row = jnp.repeat(jnp.arange(rows), GRID_W).astype(F32)
    col = jnp.tile(jnp.arange(GRID_W), rows).astype(F32)
    n_freq = rot_dim // 4
    freqs = ROPE_THETA ** (-jnp.arange(n_freq, dtype=F32) / n_freq)
    ang = jnp.concatenate([row[:, None] * freqs, col[:, None] * freqs], axis=-1)
    cos, sin = jnp.cos(ang), jnp.sin(ang)
    reps = LANES // rot_dim
    cos_t = jnp.tile(jnp.concatenate([cos, cos], axis=-1), (1, reps))
    sin_t = jnp.tile(jnp.concatenate([-sin, sin], axis=-1), (1, reps))
    return cos_t, sin_t


def kernel(x, c, ctx, c_ctx, w_mod, b_mod, norm_mix, norm_ffn, w_in, a_q_norm, a_k_norm, b_kv_norm, w_ukv, w_br_a, w_br_b, w_out, w_group, b_group, w_router, b_router, w_e_gate, w_e_up, w_e_down, norm_final):
    b, s, d = x.shape
    n_ctx = ctx.shape[1]
    assert w_mod.shape[0] == 1, "single-layer block"
    assert s % GRID_W == 0

    wi = w_in[0]
    qa_w = A_HEADS * A_HEAD_DIM
    kv_w = A_KV_HEADS * A_HEAD_DIM
    o_qb = qa_w + 2 * kv_w
    qb_w = B_HEADS * (B_NOPE + B_ROPE)
    o_ckv = o_qb + qb_w
    o_kr = o_ckv + B_KV_RANK
    o_gl = o_kr + B_ROPE
    w_qb = wi[:, o_qb:o_ckv].reshape(d, B_HEADS, B_NOPE + B_ROPE)
    w_p = jnp.concatenate(
        [wi[:, o_gl:], wi[:, :o_qb],
         w_qb[:, :, :B_NOPE].reshape(d, B_HEADS * B_NOPE),
         w_qb[:, :, B_NOPE:].reshape(d, B_HEADS * B_ROPE)], axis=1).astype(BF16)
    w_ck = jnp.concatenate([wi[:, o_ckv:o_kr], wi[:, o_kr:o_gl], wi[:, o_kr:o_gl]], axis=1).astype(BF16)
    w_kv = w_ukv[0].reshape(B_KV_RANK, B_HEADS, B_NOPE + B_V)
    w_k = w_kv[:, :, :B_NOPE].reshape(B_KV_RANK, B_HEADS * B_NOPE).astype(BF16)
    w_v = w_kv[:, :, B_NOPE:].reshape(B_KV_RANK, B_HEADS * B_V).astype(BF16)
    w_r = jnp.concatenate(
        [w_group[0], jnp.transpose(w_router[0], (1, 0, 2)).reshape(d, N_EXPERTS),
         jnp.zeros((d, LANES - N_GROUPS - N_EXPERTS), F32)], axis=1).astype(BF16)
    b_r = jnp.concatenate([b_group[0], b_router[0].reshape(N_EXPERTS),
                           jnp.zeros((LANES - N_GROUPS - N_EXPERTS,), F32)])[None, :]

    cond = jnp.concatenate([c, c_ctx[None, :], jnp.zeros((8 - b - 1, d), F32)], axis=0)
    mod = _modulation(cond, w_mod[0], b_mod[0])
    mx = mod[:b].reshape(b, N_MOD, 1, d)
    sh1, sc1, g1, sh2, sc2, g2 = [mx[:, k] for k in range(N_MOD)]
    mc = mod[b].reshape(N_MOD, 1, 1, d)
    csh1, csc1 = mc[0], mc[1]

    cos_a, sin_a = _rope_tables(s, A_HEAD_DIM)
    cos_b, sin_b = _rope_tables(s, B_ROPE)
    g_mix = norm_mix[0][None, :]
    g_kv = b_kv_norm[0][None, :]
    g_q = a_q_norm[0][None, :]
    g_k = a_k_norm[0][None, :]

    h, kb_lat, vb_lat = _prologue(x, sh1, sc1, g_mix, w_ck, g_kv, w_k, w_v, cos_b, sin_b, True,
                                  min(512, s))
    p_lat = _projection(h, w_p, g_q, g_k, cos_a, sin_a, cos_b, sin_b, True, min(1024, s))

    rows_c = b * n_ctx
    ctx_flat = ctx.reshape(1, rows_c, d)
    tab_c = jnp.zeros((rows_c, LANES), F32)
    hc, kb_ctx, vb_ctx = _prologue(ctx_flat, csh1, csc1, g_mix, w_ck, g_kv, w_k, w_v, tab_c, tab_c,
                                   False, min(512, rows_c))
    p_ctx = _projection(hc, w_p, g_q, g_k, tab_c, tab_c, tab_c, tab_c, False, min(1024, rows_c))
    p_ctx = p_ctx.reshape(b, n_ctx, P_COLS)
    kb_ctx = kb_ctx.reshape(b, n_ctx, -1)
    vb_ctx = vb_ctx.reshape(b, n_ctx, -1)

    tk = min(512, s)
    oa = _gqa_attention(p_lat, p_ctx, min(256, s), tk)
    ob = _mla_attention(p_lat, kb_lat, vb_lat, kb_ctx, vb_ctx, min(512, s), tk)

    x_new, h2, route = _merge_route(
        oa, ob, p_lat, x, g1, sh2, sc2, norm_ffn[0][None, :],
        w_br_a[0].astype(BF16), w_br_b[0].astype(BF16), w_out[0].astype(BF16), w_r, b_r,
        min(256, s))

    n = b * s
    moe_tm = 256
    eid = route[0:2].astype(jnp.int32)
    plan = _moe_plan(eid, route[2:4], moe_tm)
    y2 = _moe(h2.reshape(n, d), *plan, w_e_gate[0].astype(BF16), w_e_up[0].astype(BF16),
              w_e_down[0].astype(BF16), moe_tm)

    return _final(x_new, y2.reshape(2, b, s, d), g2, norm_final[None, :], min(512, s))
```

```python
import functools
import math

import jax
import jax.numpy as jnp
import numpy as np
from jax import lax
from jax.experimental import pallas as pl
from jax.experimental.pallas import tpu as pltpu

GRID_W = 64
ROPE_THETA = 10000.0
EPS = 1e-6
A_HEADS = 8
A_KV_HEADS = 2
A_HEAD_DIM = 128
B_HEADS = 8
B_NOPE = 128
B_ROPE = 64
B_V = 128
B_KV_RANK = 512
N_GROUPS = 4
EXPERTS_PER_GROUP = 4
N_EXPERTS = N_GROUPS * EXPERTS_PER_GROUP
D_EXPERT = 1024
N_MOD = 6

LANES = 128
V7X_VMEM_LIMIT_BYTES = 56 * 1024 * 1024

BF16 = jnp.bfloat16
F32 = jnp.float32

P_GA = 0
P_GB = 2048
P_QA = 4096
P_KA = 5120
P_VA = 5376
P_QBN = 5632
P_QBR = 6656
P_COLS = 7168
PROJ_TN = 512

LOG2_E = math.log2(math.e)
A_SCORE_SCALE = LOG2_E / math.sqrt(A_HEAD_DIM)
B_SCORE_SCALE = LOG2_E / math.sqrt(B_NOPE + B_ROPE)
FLASH_GROUP = 512


def _cparams(*sem):
    return pltpu.CompilerParams(dimension_semantics=sem, vmem_limit_bytes=V7X_VMEM_LIMIT_BYTES)


def _dot(a, b):
    return jnp.dot(a, b, preferred_element_type=F32)


def _dot_nt(a, b):
    return lax.dot_general(a, b, (((1,), (1,)), ((), ())), preferred_element_type=F32)


def _resident(shape):
    return pl.BlockSpec(shape, lambda *_: (0,) * len(shape), pipeline_mode=pl.Buffered(1))


def _rms(v, gain):
    return v * lax.rsqrt(jnp.mean(v * v, axis=-1, keepdims=True) + EPS) * gain


def _mod_kernel(c_ref, w_ref, b_ref, o_ref):
    c = c_ref[...]
    s = (c * jax.nn.sigmoid(c)).astype(BF16)
    o_ref[...] = _dot(s, w_ref[...].astype(BF16)) + b_ref[...]


def _modulation(cond, w_mod, b_mod):
    rows, d = cond.shape
    n = w_mod.shape[1]
    tn = 1024
    return pl.pallas_call(
        _mod_kernel,
        grid=(n // tn,),
        in_specs=[pl.BlockSpec((rows, d), lambda j: (0, 0)),
                  pl.BlockSpec((d, tn), lambda j: (0, j)),
                  pl.BlockSpec((1, tn), lambda j: (0, j))],
        out_specs=pl.BlockSpec((rows, tn), lambda j: (0, j)),
        out_shape=jax.ShapeDtypeStruct((rows, n), F32),
        compiler_params=_cparams("parallel"),
        name="modulation",
    )(cond, w_mod, b_mod.reshape(1, n))


def _swap_halves_64(v):
    lane = lax.broadcasted_iota(jnp.int32, v.shape, 1)
    return jnp.where((lane & 63) < 32, pltpu.roll(v, LANES - 32, 1), pltpu.roll(v, 32, 1))


def _pre_kernel(use_rope, x_ref, sh_ref, sc_ref, g_ref, wck_ref, gkv_ref, wk_ref, wv_ref,
                cb_ref, sb_ref, h_ref, kb_ref, vb_ref):
    xf = x_ref[0]
    h = _rms(xf, g_ref[...]) * (1.0 + sc_ref[0]) + sh_ref[0]
    hb = h.astype(BF16)
    h_ref[0] = hb
    p = _dot(hb, wck_ref[...])
    cn = _rms(p[:, :B_KV_RANK], gkv_ref[...]).astype(BF16)
    kr2 = p[:, B_KV_RANK:]
    if use_rope:
        kr2 = kr2 * cb_ref[...] + _swap_halves_64(kr2) * sb_ref[...]
    kr2 = kr2.astype(BF16)
    kbn = _dot(cn, wk_ref[...]).astype(BF16)
    vb_ref[0] = _dot(cn, wv_ref[...]).astype(BF16)
    for hd in range(B_HEADS):
        kb_ref[0, :, hd * 256:hd * 256 + B_NOPE] = kbn[:, hd * B_NOPE:(hd + 1) * B_NOPE]
        kb_ref[0, :, hd * 256 + B_NOPE:(hd + 1) * 256] = kr2


def _prologue(x, shift, scale, gain, w_ck, g_kv, w_k, w_v, cos_b, sin_b, use_rope, tm):
    bt, st, d = x.shape
    grid = (bt, st // tm)
    row = lambda b, i: (b, i, 0)
    per_b = lambda b, i: (b, 0, 0)
    fixed = lambda b, i: (0, 0)
    return pl.pallas_call(
        functools.partial(_pre_kernel, use_rope),
        grid=grid,
        in_specs=[pl.BlockSpec((1, tm, d), row),
                  pl.BlockSpec((1, 1, d), per_b),
                  pl.BlockSpec((1, 1, d), per_b),
                  pl.BlockSpec((1, d), fixed),
                  _resident(w_ck.shape),
                  pl.BlockSpec((1, B_KV_RANK), fixed),
                  _resident(w_k.shape),
                  _resident(w_v.shape),
                  pl.BlockSpec((tm, LANES), lambda b, i: (i, 0)),
                  pl.BlockSpec((tm, LANES), lambda b, i: (i, 0))],
        out_specs=[pl.BlockSpec((1, tm, d), row),
                   pl.BlockSpec((1, tm, B_HEADS * 256), row),
                   pl.BlockSpec((1, tm, B_HEADS * B_V), row)],
        out_shape=[jax.ShapeDtypeStruct((bt, st, d), BF16),
                   jax.ShapeDtypeStruct((bt, st, B_HEADS * 256), BF16),
                   jax.ShapeDtypeStruct((bt, st, B_HEADS * B_V), BF16)],
        compiler_params=_cparams("parallel", "parallel"),
        name="prologue_rope" if use_rope else "prologue_ctx",
    )(x, shift, scale, gain, w_ck, g_kv, w_k, w_v, cos_b, sin_b)


def _proj_kernel(use_rope, h_ref, w_ref, gq_ref, gk_ref, ca_ref, sa_ref, cb_ref, sb_ref,
                 o_ref, acc_ref):
    j = pl.program_id(2)
    acc_ref[...] = _dot(h_ref[0], w_ref[...])
    n_blk = PROJ_TN // LANES

    def rope_a(v):
        if not use_rope:
            return v
        return v * ca_ref[...] + pltpu.roll(v, A_HEAD_DIM // 2, 1) * sa_ref[...]

    def rope_b(v):
        if not use_rope:
            return v
        return v * cb_ref[...] + _swap_halves_64(v) * sb_ref[...]

    def blk(k):
        return acc_ref[:, k * LANES:(k + 1) * LANES]

    def put(k, v):
        o_ref[0, :, k * LANES:(k + 1) * LANES] = v.astype(BF16)

    @pl.when(j < P_QA // PROJ_TN)
    def _():
        o_ref[0] = jax.nn.sigmoid(acc_ref[...]).astype(BF16)

    @pl.when((j >= P_QA // PROJ_TN) & (j < P_KA // PROJ_TN))
    def _():
        for k in range(n_blk):
            put(k, rope_a(_rms(blk(k), gq_ref[...])) * A_SCORE_SCALE)

    @pl.when(j == P_KA // PROJ_TN)
    def _():
        for k in range(A_KV_HEADS):
            put(k, rope_a(_rms(blk(k), gk_ref[...])))
        for k in range(A_KV_HEADS, n_blk):
            put(k, blk(k))

    @pl.when((j >= P_QBN // PROJ_TN) & (j < P_QBR // PROJ_TN))
    def _():
        o_ref[0] = (acc_ref[...] * B_SCORE_SCALE).astype(BF16)

    @pl.when(j == P_QBR // PROJ_TN)
    def _():
        for k in range(n_blk):
            put(k, rope_b(blk(k)) * B_SCORE_SCALE)


def _projection(h, w_p, g_q, g_k, cos_a, sin_a, cos_b, sin_b, use_rope, tm):
    bt, st, d = h.shape
    grid = (bt, st // tm, P_COLS // PROJ_TN)
    fixed = lambda b, i, j: (0, 0)
    tab = lambda b, i, j: (i, 0)
    return pl.pallas_call(
        functools.partial(_proj_kernel, use_rope),
        grid=grid,
        in_specs=[pl.BlockSpec((1, tm, d), lambda b, i, j: (b, i, 0)),
                  pl.BlockSpec((d, PROJ_TN), lambda b, i, j: (0, j)),
                  pl.BlockSpec((1, LANES), fixed),
                  pl.BlockSpec((1, LANES), fixed),
                  pl.BlockSpec((tm, LANES), tab),
                  pl.BlockSpec((tm, LANES), tab),
                  pl.BlockSpec((tm, LANES), tab),
                  pl.BlockSpec((tm, LANES), tab)],
        out_specs=pl.BlockSpec((1, tm, PROJ_TN), lambda b, i, j: (b, i, j)),
        out_shape=jax.ShapeDtypeStruct((bt, st, P_COLS), BF16),
        scratch_shapes=[pltpu.VMEM((tm, PROJ_TN), F32)],
        compiler_params=_cparams("parallel", "parallel", "arbitrary"),
        name="projection_rope" if use_rope else "projection_ctx",
    )(h, w_p, g_q, g_k, cos_a, sin_a, cos_b, sin_b)


def _flash(q_ref, kc_ref, vct_ref, kl_ref, vlt_ref, st_ref, acc_ref, tk):
    n_chunks = kl_ref.shape[1] // tk
    m_rows = q_ref.shape[0]
    spans = [(i, i + FLASH_GROUP) for i in range(0, m_rows, FLASH_GROUP)]

    def latent_scores(c, slot):
        k = kl_ref[0, pl.ds(pl.multiple_of(c * tk, tk), tk), :]
        for lo, hi in spans:
            st_ref[slot, :, lo:hi] = _dot_nt(k, q_ref[lo:hi, :])

    def update(c, slot, stats):
        vt = vlt_ref[0, 0, c]
        out = []
        for (lo, hi), (m, l) in zip(spans, stats):
            st = st_ref[slot, :, lo:hi]
            m_new = jnp.maximum(m, jnp.max(st, axis=0, keepdims=True))
            alpha = jnp.exp2(m - m_new)
            pt = jnp.exp2(st - m_new)
            out.append((m_new, alpha * l + jnp.sum(pt, axis=0, keepdims=True)))
            acc_ref[:, lo:hi] = alpha * acc_ref[:, lo:hi] + _dot(vt, pt.astype(BF16))
        return tuple(out)

    latent_scores(0, 0)
    stats = []
    for lo, hi in spans:
        st = _dot_nt(kc_ref[0], q_ref[lo:hi, :])
        m0 = jnp.max(st, axis=0, keepdims=True)
        pt = jnp.exp2(st - m0)
        stats.append((m0, jnp.sum(pt, axis=0, keepdims=True)))
        acc_ref[:, lo:hi] = _dot(vct_ref[0, 0], pt.astype(BF16))

    def body(i, stats):
        c = 2 * i
        latent_scores(c + 1, 1)
        stats = update(c, 0, stats)
        latent_scores(c + 2, 0)
        return update(c + 1, 1, stats)

    n_pairs = (n_chunks - 1) // 2
    stats = lax.fori_loop(0, n_pairs, body, tuple(stats))
    if n_chunks - 2 * n_pairs == 2:
        latent_scores(n_chunks - 1, 1)
        stats = update(n_chunks - 2, 0, stats)
        stats = update(n_chunks - 1, 1, stats)
    else:
        stats = update(n_chunks - 1, 0, stats)
    return jnp.concatenate(
        [(acc_ref[:, lo:hi] / l).T for (lo, hi), (_, l) in zip(spans, stats)], axis=0)


def _flash_scratch(m_rows, dk, dv, tk):
    return [pltpu.VMEM((m_rows, dk), BF16),
            pltpu.VMEM((2, tk, m_rows), F32),
            pltpu.VMEM((dv, m_rows), F32)]


def _gqa_kernel(tk, q_ref, kl_ref, vlt_ref, kc_ref, vct_ref, o_ref, qs_ref, st_ref, acc_ref):
    group = A_HEADS // A_KV_HEADS
    tq = q_ref.shape[1]
    for g in range(group):
        qs_ref[g * tq:(g + 1) * tq, :] = q_ref[0, :, g * A_HEAD_DIM:(g + 1) * A_HEAD_DIM]
    out = _flash(qs_ref, kc_ref, vct_ref, kl_ref, vlt_ref, st_ref, acc_ref, tk)
    for g in range(group):
        o_ref[0, :, g * A_HEAD_DIM:(g + 1) * A_HEAD_DIM] = out[g * tq:(g + 1) * tq].astype(BF16)


def _value_major(v, heads, tk):
    b, n, w = v.shape
    dv = w // heads
    return jnp.transpose(v.reshape(b, n // tk, tk, heads, dv), (0, 3, 1, 4, 2))


def _gqa_attention(p_lat, p_ctx, tq, tk):
    b, s, _ = p_lat.shape
    n_ctx = p_ctx.shape[1]
    group_w = (A_HEADS // A_KV_HEADS) * A_HEAD_DIM
    kv_w = A_KV_HEADS * A_HEAD_DIM
    vlt = _value_major(p_lat[:, :, P_VA:P_VA + kv_w], A_KV_HEADS, tk)
    vct = _value_major(p_ctx[:, :, P_VA:P_VA + kv_w], A_KV_HEADS, n_ctx)[:, :, 0]
    return pl.pallas_call(
        functools.partial(_gqa_kernel, tk),
        grid=(b, A_KV_HEADS, s // tq),
        in_specs=[pl.BlockSpec((1, tq, group_w), lambda bb, k, i: (bb, i, P_QA // group_w + k)),
                  pl.BlockSpec((1, s, LANES), lambda bb, k, i: (bb, 0, P_KA // LANES + k)),
                  pl.BlockSpec((1, 1, s // tk, A_HEAD_DIM, tk), lambda bb, k, i: (bb, k, 0, 0, 0)),
                  pl.BlockSpec((1, n_ctx, LANES), lambda bb, k, i: (bb, 0, P_KA // LANES + k)),
                  pl.BlockSpec((1, 1, A_HEAD_DIM, n_ctx), lambda bb, k, i: (bb, k, 0, 0))],
        out_specs=pl.BlockSpec((1, tq, group_w), lambda bb, k, i: (bb, i, k)),
        out_shape=jax.ShapeDtypeStruct((b, s, A_HEADS * A_HEAD_DIM), BF16),
        scratch_shapes=_flash_scratch((A_HEADS // A_KV_HEADS) * tq, A_HEAD_DIM, A_HEAD_DIM, tk),
        compiler_params=_cparams("parallel", "parallel", "parallel"),
        name="gqa_attention",
    )(p_lat, p_lat, vlt, p_ctx, vct)


def _mla_kernel(tk, qn_ref, qr_ref, kl_ref, vlt_ref, kc_ref, vct_ref, o_ref, qs_ref, st_ref, acc_ref):
    hd = pl.program_id(1)
    qr = qr_ref[0]
    lane = lax.broadcasted_iota(jnp.int32, qr.shape, 1)
    qs_ref[:, :B_NOPE] = qn_ref[0]
    qs_ref[:, B_NOPE:] = jnp.where((lane >> 6) == (hd & 1), qr, jnp.zeros_like(qr))
    o_ref[0] = _flash(qs_ref, kc_ref, vct_ref, kl_ref, vlt_ref, st_ref, acc_ref, tk).astype(BF16)


def _mla_attention(p_lat, kb_lat, vb_lat, kb_ctx, vb_ctx, tq, tk):
    b, s, _ = p_lat.shape
    n_ctx = kb_ctx.shape[1]
    vlt = _value_major(vb_lat, B_HEADS, tk)
    vct = _value_major(vb_ctx, B_HEADS, n_ctx)[:, :, 0]
    return pl.pallas_call(
        functools.partial(_mla_kernel, tk),
        grid=(b, B_HEADS, s // tq),
        in_specs=[pl.BlockSpec((1, tq, LANES), lambda bb, h, i: (bb, i, P_QBN // LANES + h)),
                  pl.BlockSpec((1, tq, LANES), lambda bb, h, i: (bb, i, P_QBR // LANES + h // 2)),
                  pl.BlockSpec((1, s, 256), lambda bb, h, i: (bb, 0, h)),
                  pl.BlockSpec((1, 1, s // tk, B_V, tk), lambda bb, h, i: (bb, h, 0, 0, 0)),
                  pl.BlockSpec((1, n_ctx, 256), lambda bb, h, i: (bb, 0, h)),
                  pl.BlockSpec((1, 1, B_V, n_ctx), lambda bb, h, i: (bb, h, 0, 0))],
        out_specs=pl.BlockSpec((1, tq, B_V), lambda bb, h, i: (bb, i, h)),
        out_shape=jax.ShapeDtypeStruct((b, s, B_HEADS * B_V), BF16),
        scratch_shapes=_flash_scratch(tq, 2 * LANES, B_V, tk),
        compiler_params=_cparams("parallel", "parallel", "parallel"),
        name="mla_attention",
    )(p_lat, p_lat, kb_lat, vlt, kb_ctx, vct)


def _merge_kernel(oa_ref, ob_ref, ga_ref, gb_ref, x_ref, g1_ref, sh_ref, sc_ref, gn_ref,
                  wa_ref, wb_ref, wo_ref, wr_ref, br_ref, xn_ref, h2_ref, rt_ref):
    ya = _dot(oa_ref[0], wa_ref[...])
    yb = _dot(ob_ref[0], wb_ref[...])
    mix = (ga_ref[0].astype(F32) * ya + gb_ref[0].astype(F32) * yb).astype(BF16)
    xn = x_ref[0] + g1_ref[0] * _dot(mix, wo_ref[...])
    xn_ref[0] = xn
    h2 = _rms(xn, gn_ref[...]) * (1.0 + sc_ref[0]) + sh_ref[0]
    h2_ref[0] = h2
    logits = _dot(h2.astype(BF16), wr_ref[...]) + br_ref[...]
    lt = logits.T
    gl = [lt[g:g + 1, :] for g in range(N_GROUPS)]
    gmax = functools.reduce(jnp.maximum, gl)
    gsum = functools.reduce(lambda a, b_: a + b_, [jnp.exp(v - gmax) for v in gl])
    g_val = 1.0 / gsum
    g_idx = jnp.full(gmax.shape, N_GROUPS - 1, jnp.int32)
    for g in range(N_GROUPS - 2, -1, -1):
        g_idx = jnp.where(gl[g] == gmax, g, g_idx)
    el = []
    for e in range(EXPERTS_PER_GROUP):
        v = lt[N_GROUPS + e:N_GROUPS + e + 1, :]
        for g in range(1, N_GROUPS):
            row = N_GROUPS + g * EXPERTS_PER_GROUP + e
            v = jnp.where(g_idx == g, lt[row:row + 1, :], v)
        el.append(v)
    emax = functools.reduce(jnp.maximum, el)
    i1 = jnp.full(emax.shape, EXPERTS_PER_GROUP - 1, jnp.int32)
    for e in range(EXPERTS_PER_GROUP - 2, -1, -1):
        i1 = jnp.where(el[e] == emax, e, i1)
    neg = jnp.full(emax.shape, -jnp.inf, F32)
    el2 = [jnp.where(i1 == e, neg, el[e]) for e in range(EXPERTS_PER_GROUP)]
    emax2 = functools.reduce(jnp.maximum, el2)
    i2 = jnp.full(emax.shape, EXPERTS_PER_GROUP - 1, jnp.int32)
    for e in range(EXPERTS_PER_GROUP - 2, -1, -1):
        i2 = jnp.where(el2[e] == emax2, e, i2)
    p2 = jnp.exp(emax2 - emax)
    w1 = g_val / (1.0 + p2)
    w2 = g_val * p2 / (1.0 + p2)
    e1 = (g_idx * EXPERTS_PER_GROUP + i1).astype(F32)
    e2 = (g_idx * EXPERTS_PER_GROUP + i2).astype(F32)
    zero = jnp.zeros_like(w1)
    rt_ref[...] = jnp.concatenate([e1, e2, w1, w2, zero, zero, zero, zero], axis=0)


def _merge_route(oa, ob, p_lat, x, g1, sh2, sc2, g_ffn, w_a, w_b, w_o, w_r, b_r, tm):
    b, s, d = x.shape
    n_i = s // tm
    row = lambda bb, i: (bb, i, 0)
    per_b = lambda bb, i: (bb, 0, 0)
    fixed = lambda bb, i: (0, 0)
    return pl.pallas_call(
        _merge_kernel,
        grid=(b, n_i),
        in_specs=[pl.BlockSpec((1, tm, oa.shape[2]), row),
                  pl.BlockSpec((1, tm, ob.shape[2]), row),
                  pl.BlockSpec((1, tm, d), lambda bb, i: (bb, i, P_GA // d)),
                  pl.BlockSpec((1, tm, d), lambda bb, i: (bb, i, P_GB // d)),
                  pl.BlockSpec((1, tm, d), row),
                  pl.BlockSpec((1, 1, d), per_b),
                  pl.BlockSpec((1, 1, d), per_b),
                  pl.BlockSpec((1, 1, d), per_b),
                  pl.BlockSpec((1, d), fixed),
                  _resident(w_a.shape),
                  _resident(w_b.shape),
                  _resident(w_o.shape),
                  _resident(w_r.shape),
                  pl.BlockSpec((1, LANES), fixed)],
        out_specs=[pl.BlockSpec((1, tm, d), row),
                   pl.BlockSpec((1, tm, d), row),
                   pl.BlockSpec((8, tm), lambda bb, i: (0, bb * n_i + i))],
        out_shape=[jax.ShapeDtypeStruct((b, s, d), F32),
                   jax.ShapeDtypeStruct((b, s, d), F32),
                   jax.ShapeDtypeStruct((8, b * s), F32)],
        compiler_params=_cparams("parallel", "parallel"),
        name="merge_route",
    )(oa, ob, p_lat, p_lat, x, g1, sh2, sc2, g_ffn, w_a, w_b, w_o, w_r, b_r)


def _moe_kernel(tm, n_tiles, te_ref, src_ref, dst_ref, h_hbm, ws_ref, wg_ref, wu_ref, wd_ref,
                y_hbm, xbuf, ybuf, gsem, ssem):
    t = pl.program_id(0)

    def gather_start(tile, slot):
        for r in range(tm):
            pltpu.make_async_copy(h_hbm.at[pl.ds(src_ref[tile * tm + r], 1)],
                                  xbuf.at[slot, pl.ds(r, 1)], gsem.at[slot]).start()

    def gather_wait(slot):
        pltpu.make_async_copy(h_hbm.at[pl.ds(0, tm)], xbuf.at[slot], gsem.at[slot]).wait()

    def scatter_start(tile, slot):
        for r in range(tm):
            pltpu.make_async_copy(ybuf.at[slot, pl.ds(r, 1)],
                                  y_hbm.at[pl.ds(dst_ref[(tile + 1) * tm + r], 1)],
                                  ssem.at[slot]).start()

    def scatter_wait(slot):
        pltpu.make_async_copy(ybuf.at[slot], y_hbm.at[pl.ds(0, tm)], ssem.at[slot]).wait()

    @pl.when(t == 0)
    def _():
        ybuf[1] = jnp.zeros(ybuf.shape[1:], F32)
        gather_start(0, 0)

    def step(slot):
        other = 1 - slot
        gather_wait(slot)
        gather_start(t + 1, other)
        scatter_start(t - 1, other)
        xb = xbuf[slot].astype(BF16)
        gate = _dot(xb, wg_ref[0])
        up = _dot(xb, wu_ref[0])
        hid = (gate * jax.nn.sigmoid(gate) * up).astype(BF16)
        ybuf[slot] = _dot(hid, wd_ref[0]) * ws_ref[...]
        scatter_wait(other)

    for slot in range(2):
        pl.when((t & 1) == slot)(functools.partial(step, slot))

    @pl.when(t == n_tiles - 1)
    def _():
        slot = (n_tiles - 1) % 2
        scatter_start(t, slot)
        scatter_wait(slot)
        gather_wait(1 - slot)


def _moe(h2, tile_expert, src_tok, dst_row, w_sorted, w_gate, w_up, w_down, tm):
    n, d = h2.shape
    n_tiles = tile_expert.shape[0]
    grid_spec = pltpu.PrefetchScalarGridSpec(
        num_scalar_prefetch=3,
        grid=(n_tiles,),
        in_specs=[pl.BlockSpec(memory_space=pl.ANY),
                  pl.BlockSpec((tm, 1), lambda t, te, sr, ds: (t, 0)),
                  pl.BlockSpec((1, d, D_EXPERT), lambda t, te, sr, ds: (te[t], 0, 0)),
                  pl.BlockSpec((1, d, D_EXPERT), lambda t, te, sr, ds: (te[t], 0, 0)),
                  pl.BlockSpec((1, D_EXPERT, d), lambda t, te, sr, ds: (te[t], 0, 0))],
        out_specs=pl.BlockSpec(memory_space=pl.ANY),
        scratch_shapes=[pltpu.VMEM((2, tm, d), F32),
                        pltpu.VMEM((2, tm, d), F32),
                        pltpu.SemaphoreType.DMA((2,)),
                        pltpu.SemaphoreType.DMA((2,))],
    )
    return pl.pallas_call(
        functools.partial(_moe_kernel, tm, n_tiles),
        grid_spec=grid_spec,
        out_shape=jax.ShapeDtypeStruct((2 * n + tm, d), F32),
        compiler_params=_cparams("arbitrary"),
        name="moe_experts",
    )(tile_expert, src_tok, dst_row, h2, w_sorted, w_gate, w_up, w_down)


def _moe_plan(eid, wts, tm):
    n = eid.shape[1]
    pairs = 2 * n
    n_tiles = pairs // tm + N_EXPERTS
    e_flat = eid.reshape(pairs)
    order = jnp.argsort(e_flat, stable=True).astype(jnp.int32)
    counts = jnp.sum(e_flat[None, :] == jnp.arange(N_EXPERTS, dtype=jnp.int32)[:, None],
                     axis=1).astype(jnp.int32)
    padded = ((counts + tm - 1) // tm) * tm
    pad_end = jnp.cumsum(padded)
    pad_start = pad_end - padded
    raw_start = jnp.cumsum(counts) - counts
    tile_row0 = jnp.arange(n_tiles, dtype=jnp.int32) * tm
    tile_expert = jnp.minimum(
        jnp.sum(tile_row0[:, None] >= pad_end[None, :], axis=1), N_EXPERTS - 1).astype(jnp.int32)
    in_range = tile_row0 < pad_end[-1]
    tile_valid = jnp.where(
        in_range, jnp.clip(counts[tile_expert] - (tile_row0 - pad_start[tile_expert]), 0, tm), 0
    ).astype(jnp.int32)
    pos = jnp.arange(n_tiles * tm, dtype=jnp.int32)
    pe = jnp.repeat(tile_expert, tm)
    off = pos - pad_start[pe]
    valid = (pos % tm) < jnp.repeat(tile_valid, tm)
    pair = jnp.where(valid, order[jnp.clip(raw_start[pe] + off, 0, pairs - 1)], 0)
    dummy = pairs + jnp.arange(tm, dtype=jnp.int32)
    src_tok = jnp.concatenate([pair % n, jnp.zeros((tm,), jnp.int32)]).astype(jnp.int32)
    dst_row = jnp.concatenate([dummy, jnp.where(valid, pair, pairs + pos % tm)]).astype(jnp.int32)
    w_sorted = jnp.where(valid, wts.reshape(pairs)[pair], 0.0).astype(F32)[:, None]
    return tile_expert, src_tok, dst_row, w_sorted


def _final_kernel(x_ref, y0_ref, y1_ref, g2_ref, gn_ref, o_ref):
    v = x_ref[0] + g2_ref[0] * (y0_ref[...] + y1_ref[...])
    o_ref[0] = _rms(v, gn_ref[...])


def _final(x_new, y2, g2, g_final, tm):
    b, s, d = x_new.shape
    row = lambda bb, i: (bb, i, 0)
    n_i = s // tm
    return pl.pallas_call(
        _final_kernel,
        grid=(b, n_i),
        in_specs=[pl.BlockSpec((1, tm, d), row),
                  pl.BlockSpec((tm, d), lambda bb, i: (bb * n_i + i, 0)),
                  pl.BlockSpec((tm, d), lambda bb, i: (b * n_i + bb * n_i + i, 0)),
                  pl.BlockSpec((1, 1, d), lambda bb, i: (bb, 0, 0)),
                  pl.BlockSpec((1, d), lambda bb, i: (0, 0))],
        out_specs=pl.BlockSpec((1, tm, d), row),
        out_shape=jax.ShapeDtypeStruct((b, s, d), F32),
        compiler_params=_cparams("parallel", "parallel"),
        name="final_norm",
    )(x_new, y2, y2, g2, g_final)


def _rope_tables(n_tokens, rot_dim):
    rows = n_tokens // GRID_W
    row = jnp.repeat(jnp.arange(rows), GRID_W).astype(F32)
    col = jnp.tile(jnp.arange(GRID_W), rows).astype(F32)
    n_freq = rot_dim // 4
    freqs = ROPE_THETA ** (-jnp.arange(n_freq, dtype=F32) / n_freq)
    ang = jnp.concatenate([row[:, None] * freqs, col[:, None] * freqs], axis=-1)
    cos, sin = jnp.cos(ang), jnp.sin(ang)
    reps = LANES // rot_dim
    cos_t = jnp.tile(jnp.concatenate([cos, cos], axis=-1), (1, reps))
    sin_t = jnp.tile(jnp.concatenate([-sin, sin], axis=-1), (1, reps))
    return cos_t, sin_t


def kernel(x, c, ctx, c_ctx, w_mod, b_mod, norm_mix, norm_ffn, w_in, a_q_norm, a_k_norm, b_kv_norm, w_ukv, w_br_a, w_br_b, w_out, w_group, b_group, w_router, b_router, w_e_gate, w_e_up, w_e_down, norm_final):
    b, s, d = x.shape
    n_ctx = ctx.shape[1]
    assert w_mod.shape[0] == 1, "single-layer block"
    assert s % GRID_W == 0

    wi = w_in[0]
    qa_w = A_HEADS * A_HEAD_DIM
    kv_w = A_KV_HEADS * A_HEAD_DIM
    o_qb = qa_w + 2 * kv_w
    qb_w = B_HEADS * (B_NOPE + B_ROPE)
    o_ckv = o_qb + qb_w
    o_kr = o_ckv + B_KV_RANK
    o_gl = o_kr + B_ROPE
    w_qb = wi[:, o_qb:o_ckv].reshape(d, B_HEADS, B_NOPE + B_ROPE)
    w_p = jnp.concatenate(
        [wi[:, o_gl:], wi[:, :o_qb],
         w_qb[:, :, :B_NOPE].reshape(d, B_HEADS * B_NOPE),
         w_qb[:, :, B_NOPE:].reshape(d, B_HEADS * B_ROPE)], axis=1).astype(BF16)
    w_ck = jnp.concatenate([wi[:, o_ckv:o_kr], wi[:, o_kr:o_gl], wi[:, o_kr:o_gl]], axis=1).astype(BF16)
    w_kv = w_ukv[0].reshape(B_KV_RANK, B_HEADS, B_NOPE + B_V)
    w_k = w_kv[:, :, :B_NOPE].reshape(B_KV_RANK, B_HEADS * B_NOPE).astype(BF16)
    w_v = w_kv[:, :, B_NOPE:].reshape(B_KV_RANK, B_HEADS * B_V).astype(BF16)
    w_r = jnp.concatenate(
        [w_group[0], jnp.transpose(w_router[0], (1, 0, 2)).reshape(d, N_EXPERTS),
         jnp.zeros((d, LANES - N_GROUPS - N_EXPERTS), F32)], axis=1).astype(BF16)
    b_r = jnp.concatenate([b_group[0], b_router[0].reshape(N_EXPERTS),
                           jnp.zeros((LANES - N_GROUPS - N_EXPERTS,), F32)])[None, :]

    cond = jnp.concatenate([c, c_ctx[None, :], jnp.zeros((8 - b - 1, d), F32)], axis=0)
    mod = _modulation(cond, w_mod[0], b_mod[0])
    mx = mod[:b].reshape(b, N_MOD, 1, d)
    sh1, sc1, g1, sh2, sc2, g2 = [mx[:, k] for k in range(N_MOD)]
    mc = mod[b].reshape(N_MOD, 1, 1, d)
    csh1, csc1 = mc[0], mc[1]

    cos_a, sin_a = _rope_tables(s, A_HEAD_DIM)
    cos_b, sin_b = _rope_tables(s, B_ROPE)
    g_mix = norm_mix[0][None, :]
    g_kv = b_kv_norm[0][None, :]
    g_q = a_q_norm[0][None, :]
    g_k = a_k_norm[0][None, :]

    h, kb_lat, vb_lat = _prologue(x, sh1, sc1, g_mix, w_ck, g_kv, w_k, w_v, cos_b, sin_b, True,
                                  min(512, s))
    p_lat = _projection(h, w_p, g_q, g_k, cos_a, sin_a, cos_b, sin_b, True, min(1024, s))

    rows_c = b * n_ctx
    ctx_flat = ctx.reshape(1, rows_c, d)
    tab_c = jnp.zeros((rows_c, LANES), F32)
    hc, kb_ctx, vb_ctx = _prologue(ctx_flat, csh1, csc1, g_mix, w_ck, g_kv, w_k, w_v, tab_c, tab_c,
                                   False, min(512, rows_c))
    p_ctx = _projection(hc, w_p, g_q, g_k, tab_c, tab_c, tab_c, tab_c, False, min(1024, rows_c))
    p_ctx = p_ctx.reshape(b, n_ctx, P_COLS)
    kb_ctx = kb_ctx.reshape(b, n_ctx, -1)
    vb_ctx = vb_ctx.reshape(b, n_ctx, -1)

    tk = min(512, s)
    oa = _gqa_attention(p_lat, p_ctx, min(256, s), tk)
    ob = _mla_attention(p_lat, kb_lat, vb_lat, kb_ctx, vb_ctx, min(1024, s), tk)

    x_new, h2, route = _merge_route(
        oa, ob, p_lat, x, g1, sh2, sc2, norm_ffn[0][None, :],
        w_br_a[0].astype(BF16), w_br_b[0].astype(BF16), w_out[0].astype(BF16), w_r, b_r,
        min(256, s))

    n = b * s
    moe_tm = 256
    eid = route[0:2].astype(jnp.int32)
    plan = _moe_plan(eid, route[2:4], moe_tm)
    y2 = _moe(h2.reshape(n, d), *plan, w_e_gate[0].astype(BF16), w_e_up[0].astype(BF16),
              w_e_down[0].astype(BF16), moe_tm)

    return _final(x_new, y2, g2, norm_final[None, :], min(512, s))
```

```python
import functools
import math

import jax
import jax.numpy as jnp
import numpy as np
from jax import lax
from jax.experimental import pallas as pl
from jax.experimental.pallas import tpu as pltpu

GRID_W = 64
ROPE_THETA = 10000.0
EPS = 1e-6
A_HEADS = 8
A_KV_HEADS = 2
A_HEAD_DIM = 128
B_HEADS = 8
B_NOPE = 128
B_ROPE = 64
B_V = 128
B_KV_RANK = 512
N_GROUPS = 4
EXPERTS_PER_GROUP = 4
N_EXPERTS = N_GROUPS * EXPERTS_PER_GROUP
D_EXPERT = 1024
N_MOD = 6

LANES = 128
V7X_VMEM_LIMIT_BYTES = 56 * 1024 * 1024

BF16 = jnp.bfloat16
F32 = jnp.float32

P_GA = 0
P_GB = 2048
P_QA = 4096
P_KA = 5120
P_VA = 5376
P_QBN = 5632
P_QBR = 6656
P_COLS = 7168
PROJ_TN = 512

LOG2_E = math.log2(math.e)
A_SCORE_SCALE = LOG2_E / math.sqrt(A_HEAD_DIM)
B_SCORE_SCALE = LOG2_E / math.sqrt(B_NOPE + B_ROPE)
FLASH_GROUP = 512


def _cparams(*sem):
    return pltpu.CompilerParams(dimension_semantics=sem, vmem_limit_bytes=V7X_VMEM_LIMIT_BYTES)


def _dot(a, b):
    return jnp.dot(a, b, preferred_element_type=F32)


def _dot_nt(a, b):
    return lax.dot_general(a, b, (((1,), (1,)), ((), ())), preferred_element_type=F32)


def _resident(shape):
    return pl.BlockSpec(shape, lambda *_: (0,) * len(shape), pipeline_mode=pl.Buffered(1))


def _rms(v, gain):
    return v * lax.rsqrt(jnp.mean(v * v, axis=-1, keepdims=True) + EPS) * gain


def _mod_kernel(c_ref, w_ref, b_ref, o_ref):
    c = c_ref[...]
    s = (c * jax.nn.sigmoid(c)).astype(BF16)
    o_ref[...] = _dot(s, w_ref[...].astype(BF16)) + b_ref[...]


def _modulation(cond, w_mod, b_mod):
    rows, d = cond.shape
    n = w_mod.shape[1]
    tn = 1024
    return pl.pallas_call(
        _mod_kernel,
        grid=(n // tn,),
        in_specs=[pl.BlockSpec((rows, d), lambda j: (0, 0)),
                  pl.BlockSpec((d, tn), lambda j: (0, j)),
                  pl.BlockSpec((1, tn), lambda j: (0, j))],
        out_specs=pl.BlockSpec((rows, tn), lambda j: (0, j)),
        out_shape=jax.ShapeDtypeStruct((rows, n), F32),
        compiler_params=_cparams("parallel"),
        name="modulation",
    )(cond, w_mod, b_mod.reshape(1, n))


def _swap_halves_64(v):
    lane = lax.broadcasted_iota(jnp.int32, v.shape, 1)
    return jnp.where((lane & 63) < 32, pltpu.roll(v, LANES - 32, 1), pltpu.roll(v, 32, 1))


def _pre_kernel(use_rope, x_ref, sh_ref, sc_ref, g_ref, wck_ref, gkv_ref, wk_ref, wv_ref,
                cb_ref, sb_ref, h_ref, kb_ref, vb_ref):
    xf = x_ref[0]
    h = _rms(xf, g_ref[...]) * (1.0 + sc_ref[0]) + sh_ref[0]
    hb = h.astype(BF16)
    h_ref[0] = hb
    p = _dot(hb, wck_ref[...])
    cn = _rms(p[:, :B_KV_RANK], gkv_ref[...]).astype(BF16)
    kr2 = p[:, B_KV_RANK:]
    if use_rope:
        kr2 = kr2 * cb_ref[...] + _swap_halves_64(kr2) * sb_ref[...]
    kr2 = kr2.astype(BF16)
    kbn = _dot(cn, wk_ref[...]).astype(BF16)
    vb_ref[0] = _dot(cn, wv_ref[...]).astype(BF16)
    for hd in range(B_HEADS):
        kb_ref[0, :, hd * 256:hd * 256 + B_NOPE] = kbn[:, hd * B_NOPE:(hd + 1) * B_NOPE]
        kb_ref[0, :, hd * 256 + B_NOPE:(hd + 1) * 256] = kr2


def _prologue(x, shift, scale, gain, w_ck, g_kv, w_k, w_v, cos_b, sin_b, use_rope, tm):
    bt, st, d = x.shape
    grid = (bt, st // tm)
    row = lambda b, i: (b, i, 0)
    per_b = lambda b, i: (b, 0, 0)
    fixed = lambda b, i: (0, 0)
    return pl.pallas_call(
        functools.partial(_pre_kernel, use_rope),
        grid=grid,
        in_specs=[pl.BlockSpec((1, tm, d), row),
                  pl.BlockSpec((1, 1, d), per_b),
                  pl.BlockSpec((1, 1, d), per_b),
                  pl.BlockSpec((1, d), fixed),
                  _resident(w_ck.shape),
                  pl.BlockSpec((1, B_KV_RANK), fixed),
                  _resident(w_k.shape),
                  _resident(w_v.shape),
                  pl.BlockSpec((tm, LANES), lambda b, i: (i, 0)),
                  pl.BlockSpec((tm, LANES), lambda b, i: (i, 0))],
        out_specs=[pl.BlockSpec((1, tm, d), row),
                   pl.BlockSpec((1, tm, B_HEADS * 256), row),
                   pl.BlockSpec((1, tm, B_HEADS * B_V), row)],
        out_shape=[jax.ShapeDtypeStruct((bt, st, d), BF16),
                   jax.ShapeDtypeStruct((bt, st, B_HEADS * 256), BF16),
                   jax.ShapeDtypeStruct((bt, st, B_HEADS * B_V), BF16)],
        compiler_params=_cparams("parallel", "parallel"),
        name="prologue_rope" if use_rope else "prologue_ctx",
    )(x, shift, scale, gain, w_ck, g_kv, w_k, w_v, cos_b, sin_b)


def _proj_kernel(use_rope, h_ref, w_ref, gq_ref, gk_ref, ca_ref, sa_ref, cb_ref, sb_ref, o_ref):
    h = h_ref[0]
    n_blk = PROJ_TN // LANES

    def rope_a(v):
        if not use_rope:
            return v
        return v * ca_ref[...] + pltpu.roll(v, A_HEAD_DIM // 2, 1) * sa_ref[...]

    def rope_b(v):
        if not use_rope:
            return v
        return v * cb_ref[...] + _swap_halves_64(v) * sb_ref[...]

    for j in range(P_COLS // PROJ_TN):
        c0 = j * PROJ_TN
        acc = _dot(h, w_ref[:, c0:c0 + PROJ_TN])

        def blk(k):
            return acc[:, k * LANES:(k + 1) * LANES]

        def put(k, v):
            o_ref[0, :, c0 + k * LANES:c0 + (k + 1) * LANES] = v.astype(BF16)

        if c0 < P_QA:
            o_ref[0, :, c0:c0 + PROJ_TN] = jax.nn.sigmoid(acc).astype(BF16)
        elif c0 < P_KA:
            for k in range(n_blk):
                put(k, rope_a(_rms(blk(k), gq_ref[...])) * A_SCORE_SCALE)
        elif c0 < P_QBN:
            for k in range(A_KV_HEADS):
                put(k, rope_a(_rms(blk(k), gk_ref[...])))
            for k in range(A_KV_HEADS, n_blk):
                put(k, blk(k))
        elif c0 < P_QBR:
            o_ref[0, :, c0:c0 + PROJ_TN] = (acc * B_SCORE_SCALE).astype(BF16)
        else:
            for k in range(n_blk):
                put(k, rope_b(blk(k)) * B_SCORE_SCALE)


def _projection(h, w_p, g_q, g_k, cos_a, sin_a, cos_b, sin_b, use_rope, tm):
    bt, st, d = h.shape
    fixed = lambda b, i: (0, 0)
    tab = lambda b, i: (i, 0)
    return pl.pallas_call(
        functools.partial(_proj_kernel, use_rope),
        grid=(bt, st // tm),
        in_specs=[pl.BlockSpec((1, tm, d), lambda b, i: (b, i, 0)),
                  _resident(w_p.shape),
                  pl.BlockSpec((1, LANES), fixed),
                  pl.BlockSpec((1, LANES), fixed),
                  pl.BlockSpec((tm, LANES), tab),
                  pl.BlockSpec((tm, LANES), tab),
                  pl.BlockSpec((tm, LANES), tab),
                  pl.BlockSpec((tm, LANES), tab)],
        out_specs=pl.BlockSpec((1, tm, P_COLS), lambda b, i: (b, i, 0)),
        out_shape=jax.ShapeDtypeStruct((bt, st, P_COLS), BF16),
        compiler_params=_cparams("parallel", "parallel"),
        name="projection_rope" if use_rope else "projection_ctx",
    )(h, w_p, g_q, g_k, cos_a, sin_a, cos_b, sin_b)


def _flash(q_ref, kc_ref, vct_ref, kl_ref, vlt_ref, st_ref, acc_ref, tk):
    n_chunks = kl_ref.shape[1] // tk
    m_rows = q_ref.shape[0]
    spans = [(i, i + FLASH_GROUP) for i in range(0, m_rows, FLASH_GROUP)]

    def latent_scores(c, slot):
        k = kl_ref[0, pl.ds(pl.multiple_of(c * tk, tk), tk), :]
        for lo, hi in spans:
            st_ref[slot, :, lo:hi] = _dot_nt(k, q_ref[lo:hi, :])

    def update(c, slot, stats):
        vt = vlt_ref[0, 0, c]
        out = []
        for (lo, hi), (m, l) in zip(spans, stats):
            st = st_ref[slot, :, lo:hi]
            m_new = jnp.maximum(m, jnp.max(st, axis=0, keepdims=True))
            alpha = jnp.exp2(m - m_new)
            pt = jnp.exp2(st - m_new)
            out.append((m_new, alpha * l + jnp.sum(pt, axis=0, keepdims=True)))
            acc_ref[:, lo:hi] = alpha * acc_ref[:, lo:hi] + _dot(vt, pt.astype(BF16))
        return tuple(out)

    latent_scores(0, 0)
    stats = []
    for lo, hi in spans:
        st = _dot_nt(kc_ref[0], q_ref[lo:hi, :])
        m0 = jnp.max(st, axis=0, keepdims=True)
        pt = jnp.exp2(st - m0)
        stats.append((m0, jnp.sum(pt, axis=0, keepdims=True)))
        acc_ref[:, lo:hi] = _dot(vct_ref[0, 0], pt.astype(BF16))

    stats = tuple(stats)
    for c in range(n_chunks):
        if c + 1 < n_chunks:
            latent_scores(c + 1, (c + 1) & 1)
        stats = update(c, c & 1, stats)
    return jnp.concatenate(
        [(acc_ref[:, lo:hi] / l).T for (lo, hi), (_, l) in zip(spans, stats)], axis=0)


def _flash_scratch(m_rows, dk, dv, tk):
    return [pltpu.VMEM((m_rows, dk), BF16),
            pltpu.VMEM((2, tk, m_rows), F32),
            pltpu.VMEM((dv, m_rows), F32)]


def _gqa_kernel(tk, q_ref, kl_ref, vlt_ref, kc_ref, vct_ref, o_ref, qs_ref, st_ref, acc_ref):
    group = A_HEADS // A_KV_HEADS
    tq = q_ref.shape[1]
    for g in range(group):
        qs_ref[g * tq:(g + 1) * tq, :] = q_ref[0, :, g * A_HEAD_DIM:(g + 1) * A_HEAD_DIM]
    out = _flash(qs_ref, kc_ref, vct_ref, kl_ref, vlt_ref, st_ref, acc_ref, tk)
    for g in range(group):
        o_ref[0, :, g * A_HEAD_DIM:(g + 1) * A_HEAD_DIM] = out[g * tq:(g + 1) * tq].astype(BF16)


def _value_major(v, heads, tk):
    b, n, w = v.shape
    dv = w // heads
    return jnp.transpose(v.reshape(b, n // tk, tk, heads, dv), (0, 3, 1, 4, 2))


def _gqa_attention(p_lat, p_ctx, tq, tk):
    b, s, _ = p_lat.shape
    n_ctx = p_ctx.shape[1]
    group_w = (A_HEADS // A_KV_HEADS) * A_HEAD_DIM
    kv_w = A_KV_HEADS * A_HEAD_DIM
    vlt = _value_major(p_lat[:, :, P_VA:P_VA + kv_w], A_KV_HEADS, tk)
    vct = _value_major(p_ctx[:, :, P_VA:P_VA + kv_w], A_KV_HEADS, n_ctx)[:, :, 0]
    return pl.pallas_call(
        functools.partial(_gqa_kernel, tk),
        grid=(b, A_KV_HEADS, s // tq),
        in_specs=[pl.BlockSpec((1, tq, group_w), lambda bb, k, i: (bb, i, P_QA // group_w + k)),
                  pl.BlockSpec((1, s, LANES), lambda bb, k, i: (bb, 0, P_KA // LANES + k)),
                  pl.BlockSpec((1, 1, s // tk, A_HEAD_DIM, tk), lambda bb, k, i: (bb, k, 0, 0, 0)),
                  pl.BlockSpec((1, n_ctx, LANES), lambda bb, k, i: (bb, 0, P_KA // LANES + k)),
                  pl.BlockSpec((1, 1, A_HEAD_DIM, n_ctx), lambda bb, k, i: (bb, k, 0, 0))],
        out_specs=pl.BlockSpec((1, tq, group_w), lambda bb, k, i: (bb, i, k)),
        out_shape=jax.ShapeDtypeStruct((b, s, A_HEADS * A_HEAD_DIM), BF16),
        scratch_shapes=_flash_scratch((A_HEADS // A_KV_HEADS) * tq, A_HEAD_DIM, A_HEAD_DIM, tk),
        compiler_params=_cparams("parallel", "parallel", "parallel"),
        name="gqa_attention",
    )(p_lat, p_lat, vlt, p_ctx, vct)


def _mla_kernel(tk, qn_ref, qr_ref, kl_ref, vlt_ref, kc_ref, vct_ref, o_ref, qs_ref, st_ref, acc_ref):
    hd = pl.program_id(1)
    qr = qr_ref[0]
    lane = lax.broadcasted_iota(jnp.int32, qr.shape, 1)
    qs_ref[:, :B_NOPE] = qn_ref[0]
    qs_ref[:, B_NOPE:] = jnp.where((lane >> 6) == (hd & 1), qr, jnp.zeros_like(qr))
    o_ref[0] = _flash(qs_ref, kc_ref, vct_ref, kl_ref, vlt_ref, st_ref, acc_ref, tk).astype(BF16)


def _mla_attention(p_lat, kb_lat, vb_lat, kb_ctx, vb_ctx, tq, tk):
    b, s, _ = p_lat.shape
    n_ctx = kb_ctx.shape[1]
    vlt = _value_major(vb_lat, B_HEADS, tk)
    vct = _value_major(vb_ctx, B_HEADS, n_ctx)[:, :, 0]
    return pl.pallas_call(
        functools.partial(_mla_kernel, tk),
        grid=(b, B_HEADS, s // tq),
        in_specs=[pl.BlockSpec((1, tq, LANES), lambda bb, h, i: (bb, i, P_QBN // LANES + h)),
                  pl.BlockSpec((1, tq, LANES), lambda bb, h, i: (bb, i, P_QBR // LANES + h // 2)),
                  pl.BlockSpec((1, s, 256), lambda bb, h, i: (bb, 0, h)),
                  pl.BlockSpec((1, 1, s // tk, B_V, tk), lambda bb, h, i: (bb, h, 0, 0, 0)),
                  pl.BlockSpec((1, n_ctx, 256), lambda bb, h, i: (bb, 0, h)),
                  pl.BlockSpec((1, 1, B_V, n_ctx), lambda bb, h, i: (bb, h, 0, 0))],
        out_specs=pl.BlockSpec((1, tq, B_V), lambda bb, h, i: (bb, i, h)),
        out_shape=jax.ShapeDtypeStruct((b, s, B_HEADS * B_V), BF16),
        scratch_shapes=_flash_scratch(tq, 2 * LANES, B_V, tk),
        compiler_params=_cparams("parallel", "parallel", "parallel"),
        name="mla_attention",
    )(p_lat, p_lat, kb_lat, vlt, kb_ctx, vct)


def _merge_kernel(oa_ref, ob_ref, ga_ref, gb_ref, x_ref, g1_ref, sh_ref, sc_ref, gn_ref,
                  wa_ref, wb_ref, wo_ref, wr_ref, br_ref, xn_ref, h2_ref, rt_ref):
    ya = _dot(oa_ref[0], wa_ref[...])
    yb = _dot(ob_ref[0], wb_ref[...])
    mix = (ga_ref[0].astype(F32) * ya + gb_ref[0].astype(F32) * yb).astype(BF16)
    xn = x_ref[0] + g1_ref[0] * _dot(mix, wo_ref[...])
    xn_ref[0] = xn
    h2 = _rms(xn, gn_ref[...]) * (1.0 + sc_ref[0]) + sh_ref[0]
    h2_ref[0] = h2
    logits = _dot(h2.astype(BF16), wr_ref[...]) + br_ref[...]
    lt = logits.T
    gl = [lt[g:g + 1, :] for g in range(N_GROUPS)]
    gmax = functools.reduce(jnp.maximum, gl)
    gsum = functools.reduce(lambda a, b_: a + b_, [jnp.exp(v - gmax) for v in gl])
    g_val = 1.0 / gsum
    g_idx = jnp.full(gmax.shape, N_GROUPS - 1, jnp.int32)
    for g in range(N_GROUPS - 2, -1, -1):
        g_idx = jnp.where(gl[g] == gmax, g, g_idx)
    el = []
    for e in range(EXPERTS_PER_GROUP):
        v = lt[N_GROUPS + e:N_GROUPS + e + 1, :]
        for g in range(1, N_GROUPS):
            row = N_GROUPS + g * EXPERTS_PER_GROUP + e
            v = jnp.where(g_idx == g, lt[row:row + 1, :], v)
        el.append(v)
    emax = functools.reduce(jnp.maximum, el)
    i1 = jnp.full(emax.shape, EXPERTS_PER_GROUP - 1, jnp.int32)
    for e in range(EXPERTS_PER_GROUP - 2, -1, -1):
        i1 = jnp.where(el[e] == emax, e, i1)
    neg = jnp.full(emax.shape, -jnp.inf, F32)
    el2 = [jnp.where(i1 == e, neg, el[e]) for e in range(EXPERTS_PER_GROUP)]
    emax2 = functools.reduce(jnp.maximum, el2)
    i2 = jnp.full(emax.shape, EXPERTS_PER_GROUP - 1, jnp.int32)
    for e in range(EXPERTS_PER_GROUP - 2, -1, -1):
        i2 = jnp.where(el2[e] == emax2, e, i2)
    p2 = jnp.exp(emax2 - emax)
    w1 = g_val / (1.0 + p2)
    w2 = g_val * p2 / (1.0 + p2)
    e1 = (g_idx * EXPERTS_PER_GROUP + i1).astype(F32)
    e2 = (g_idx * EXPERTS_PER_GROUP + i2).astype(F32)
    zero = jnp.zeros_like(w1)
    rt_ref[...] = jnp.concatenate([e1, e2, w1, w2, zero, zero, zero, zero], axis=0)


def _merge_route(oa, ob, p_lat, x, g1, sh2, sc2, g_ffn, w_a, w_b, w_o, w_r, b_r, tm):
    b, s, d = x.shape
    n_i = s // tm
    row = lambda bb, i: (bb, i, 0)
    per_b = lambda bb, i: (bb, 0, 0)
    fixed = lambda bb, i: (0, 0)
    return pl.pallas_call(
        _merge_kernel,
        grid=(b, n_i),
        in_specs=[pl.BlockSpec((1, tm, oa.shape[2]), row),
                  pl.BlockSpec((1, tm, ob.shape[2]), row),
                  pl.BlockSpec((1, tm, d), lambda bb, i: (bb, i, P_GA // d)),
                  pl.BlockSpec((1, tm, d), lambda bb, i: (bb, i, P_GB // d)),
                  pl.BlockSpec((1, tm, d), row),
                  pl.BlockSpec((1, 1, d), per_b),
                  pl.BlockSpec((1, 1, d), per_b),
                  pl.BlockSpec((1, 1, d), per_b),
                  pl.BlockSpec((1, d), fixed),
                  _resident(w_a.shape),
                  _resident(w_b.shape),
                  _resident(w_o.shape),
                  _resident(w_r.shape),
                  pl.BlockSpec((1, LANES), fixed)],
        out_specs=[pl.BlockSpec((1, tm, d), row),
                   pl.BlockSpec((1, tm, d), row),
                   pl.BlockSpec((8, tm), lambda bb, i: (0, bb * n_i + i))],
        out_shape=[jax.ShapeDtypeStruct((b, s, d), F32),
                   jax.ShapeDtypeStruct((b, s, d), F32),
                   jax.ShapeDtypeStruct((8, b * s), F32)],
        compiler_params=_cparams("parallel", "parallel"),
        name="merge_route",
    )(oa, ob, p_lat, p_lat, x, g1, sh2, sc2, g_ffn, w_a, w_b, w_o, w_r, b_r)


def _moe_kernel(tm, na_ref, te_ref, src_ref, dst_ref, h_hbm, ws_ref, wg_ref, wu_ref, wd_ref,
                y_hbm, xbuf, ybuf, gsem, ssem):
    t = pl.program_id(0)
    n_active = na_ref[0]

    def gather_start(tile, slot):
        for r in range(tm):
            pltpu.make_async_copy(h_hbm.at[pl.ds(src_ref[tile * tm + r], 1)],
                                  xbuf.at[slot, pl.ds(r, 1)], gsem.at[slot]).start()

    def gather_wait(slot):
        pltpu.make_async_copy(h_hbm.at[pl.ds(0, tm)], xbuf.at[slot], gsem.at[slot]).wait()

    def scatter_start(tile, slot):
        for r in range(tm):
            pltpu.make_async_copy(ybuf.at[slot, pl.ds(r, 1)],
                                  y_hbm.at[pl.ds(dst_ref[(tile + 1) * tm + r], 1)],
                                  ssem.at[slot]).start()

    def scatter_wait(slot):
        pltpu.make_async_copy(ybuf.at[slot], y_hbm.at[pl.ds(0, tm)], ssem.at[slot]).wait()

    @pl.when(t == 0)
    def _():
        ybuf[1] = jnp.zeros(ybuf.shape[1:], F32)
        gather_start(0, 0)

    def step(slot):
        other = 1 - slot
        gather_wait(slot)
        gate = _dot(xbuf[slot].astype(BF16), wg_ref[0])
        gather_start(t + 1, other)
        up = _dot(xbuf[slot].astype(BF16), wu_ref[0])
        scatter_start(t - 1, other)
        hid = (gate * jax.nn.sigmoid(gate) * up).astype(BF16)
        ybuf[slot] = _dot(hid, wd_ref[0]) * ws_ref[...]
        scatter_wait(other)

    def drain(slot):
        scatter_start(t, slot)
        scatter_wait(slot)
        gather_wait(1 - slot)

    for slot in range(2):
        mine = (t & 1) == slot
        pl.when(mine & (t < n_active))(functools.partial(step, slot))
        pl.when(mine & (t == n_active - 1))(functools.partial(drain, slot))


def _moe(h2, n_active, tile_expert, src_tok, dst_row, w_sorted, w_gate, w_up, w_down, tm):
    n, d = h2.shape
    n_tiles = tile_expert.shape[0]
    w_in_map = lambda t, na, te, sr, ds: (te[t], 0, 0)
    grid_spec = pltpu.PrefetchScalarGridSpec(
        num_scalar_prefetch=4,
        grid=(n_tiles,),
        in_specs=[pl.BlockSpec(memory_space=pl.ANY),
                  pl.BlockSpec((tm, 1), lambda t, na, te, sr, ds: (t, 0)),
                  pl.BlockSpec((1, d, D_EXPERT), w_in_map),
                  pl.BlockSpec((1, d, D_EXPERT), w_in_map),
                  pl.BlockSpec((1, D_EXPERT, d), w_in_map)],
        out_specs=pl.BlockSpec(memory_space=pl.ANY),
        scratch_shapes=[pltpu.VMEM((2, tm, d), F32),
                        pltpu.VMEM((2, tm, d), F32),
                        pltpu.SemaphoreType.DMA((2,)),
                        pltpu.SemaphoreType.DMA((2,))],
    )
    return pl.pallas_call(
        functools.partial(_moe_kernel, tm),
        grid_spec=grid_spec,
        out_shape=jax.ShapeDtypeStruct((2 * n + tm, d), F32),
        compiler_params=_cparams("arbitrary"),
        name="moe_experts",
    )(n_active, tile_expert, src_tok, dst_row, h2, w_sorted, w_gate, w_up, w_down)


def _moe_plan(eid, wts, tm):
    n = eid.shape[1]
    pairs = 2 * n
    n_tiles = pairs // tm + N_EXPERTS
    e_flat = eid.reshape(pairs)
    order = jnp.argsort(e_flat, stable=True).astype(jnp.int32)
    counts = jnp.sum(e_flat[None, :] == jnp.arange(N_EXPERTS, dtype=jnp.int32)[:, None],
                     axis=1).astype(jnp.int32)
    padded = ((counts + tm - 1) // tm) * tm
    pad_end = jnp.cumsum(padded)
    pad_start = pad_end - padded
    raw_start = jnp.cumsum(counts) - counts
    tile_row0 = jnp.arange(n_tiles, dtype=jnp.int32) * tm
    tile_expert = jnp.minimum(
        jnp.sum(tile_row0[:, None] >= pad_end[None, :], axis=1), N_EXPERTS - 1).astype(jnp.int32)
    in_range = tile_row0 < pad_end[-1]
    tile_valid = jnp.where(
        in_range, jnp.clip(counts[tile_expert] - (tile_row0 - pad_start[tile_expert]), 0, tm), 0
    ).astype(jnp.int32)
    pos = jnp.arange(n_tiles * tm, dtype=jnp.int32)
    pe = jnp.repeat(tile_expert, tm)
    off = pos - pad_start[pe]
    valid = (pos % tm) < jnp.repeat(tile_valid, tm)
    pair = jnp.where(valid, order[jnp.clip(raw_start[pe] + off, 0, pairs - 1)], 0)
    dummy = pairs + jnp.arange(tm, dtype=jnp.int32)
    src_tok = jnp.concatenate([pair % n, jnp.zeros((tm,), jnp.int32)]).astype(jnp.int32)
    dst_row = jnp.concatenate([dummy, jnp.where(valid, pair, pairs + pos % tm)]).astype(jnp.int32)
    w_sorted = jnp.where(valid, wts.reshape(pairs)[pair], 0.0).astype(F32)[:, None]
    n_active = (pad_end[-1:] // tm).astype(jnp.int32)
    return n_active, tile_expert, src_tok, dst_row, w_sorted


def _final_kernel(x_ref, y0_ref, y1_ref, g2_ref, gn_ref, o_ref):
    v = x_ref[0] + g2_ref[0] * (y0_ref[...] + y1_ref[...])
    o_ref[0] = _rms(v, gn_ref[...])


def _final(x_new, y2, g2, g_final, tm):
    b, s, d = x_new.shape
    row = lambda bb, i: (bb, i, 0)
    n_i = s // tm
    return pl.pallas_call(
        _final_kernel,
        grid=(b, n_i),
        in_specs=[pl.BlockSpec((1, tm, d), row),
                  pl.BlockSpec((tm, d), lambda bb, i: (bb * n_i + i, 0)),
                  pl.BlockSpec((tm, d), lambda bb, i: (b * n_i + bb * n_i + i, 0)),
                  pl.BlockSpec((1, 1, d), lambda bb, i: (bb, 0, 0)),
                  pl.BlockSpec((1, d), lambda bb, i: (0, 0))],
        out_specs=pl.BlockSpec((1, tm, d), row),
        out_shape=jax.ShapeDtypeStruct((b, s, d), F32),
        compiler_params=_cparams("parallel", "parallel"),
        name="final_norm",
    )(x_new, y2, y2, g2, g_final)


def _rope_tables(n_tokens, rot_dim):
    rows = n_tokens // GRID_W
    row = jnp.repeat(jnp.arange(rows), GRID_W).astype(F32)
    col = jnp.tile(jnp.arange(GRID_W), rows).astype(F32)
    n_freq = rot_dim // 4
    freqs = ROPE_THETA ** (-jnp.arange(n_freq, dtype=F32) / n_freq)
    ang = jnp.concatenate([row[:, None] * freqs, col[:, None] * freqs], axis=-1)
    cos, sin = jnp.cos(ang), jnp.sin(ang)
    reps = LANES // rot_dim
    cos_t = jnp.tile(jnp.concatenate([cos, cos], axis=-1), (1, reps))
    sin_t = jnp.tile(jnp.concatenate([-sin, sin], axis=-1), (1, reps))
    return cos_t, sin_t


def kernel(x, c, ctx, c_ctx, w_mod, b_mod, norm_mix, norm_ffn, w_in, a_q_norm, a_k_norm, b_kv_norm, w_ukv, w_br_a, w_br_b, w_out, w_group, b_group, w_router, b_router, w_e_gate, w_e_up, w_e_down, norm_final):
    b, s, d = x.shape
    n_ctx = ctx.shape[1]
    assert w_mod.shape[0] == 1, "single-layer block"
    assert s % GRID_W == 0

    wi = w_in[0]
    qa_w = A_HEADS * A_HEAD_DIM
    kv_w = A_KV_HEADS * A_HEAD_DIM
    o_qb = qa_w + 2 * kv_w
    qb_w = B_HEADS * (B_NOPE + B_ROPE)
    o_ckv = o_qb + qb_w
    o_kr = o_ckv + B_KV_RANK
    o_gl = o_kr + B_ROPE
    w_qb = wi[:, o_qb:o_ckv].reshape(d, B_HEADS, B_NOPE + B_ROPE)
    w_p = jnp.concatenate(
        [wi[:, o_gl:], wi[:, :o_qb],
         w_qb[:, :, :B_NOPE].reshape(d, B_HEADS * B_NOPE),
         w_qb[:, :, B_NOPE:].reshape(d, B_HEADS * B_ROPE)], axis=1).astype(BF16)
    w_ck = jnp.concatenate([wi[:, o_ckv:o_kr], wi[:, o_kr:o_gl], wi[:, o_kr:o_gl]], axis=1).astype(BF16)
    w_kv = w_ukv[0].reshape(B_KV_RANK, B_HEADS, B_NOPE + B_V)
    w_k = w_kv[:, :, :B_NOPE].reshape(B_KV_RANK, B_HEADS * B_NOPE).astype(BF16)
    w_v = w_kv[:, :, B_NOPE:].reshape(B_KV_RANK, B_HEADS * B_V).astype(BF16)
    w_r = jnp.concatenate(
        [w_group[0], jnp.transpose(w_router[0], (1, 0, 2)).reshape(d, N_EXPERTS),
         jnp.zeros((d, LANES - N_GROUPS - N_EXPERTS), F32)], axis=1).astype(BF16)
    b_r = jnp.concatenate([b_group[0], b_router[0].reshape(N_EXPERTS),
                           jnp.zeros((LANES - N_GROUPS - N_EXPERTS,), F32)])[None, :]

    cond = jnp.concatenate([c, c_ctx[None, :], jnp.zeros((8 - b - 1, d), F32)], axis=0)
    mod = _modulation(cond, w_mod[0], b_mod[0])
    mx = mod[:b].reshape(b, N_MOD, 1, d)
    sh1, sc1, g1, sh2, sc2, g2 = [mx[:, k] for k in range(N_MOD)]
    mc = mod[b].reshape(N_MOD, 1, 1, d)
    csh1, csc1 = mc[0], mc[1]

    cos_a, sin_a = _rope_tables(s, A_HEAD_DIM)
    cos_b, sin_b = _rope_tables(s, B_ROPE)
    g_mix = norm_mix[0][None, :]
    g_kv = b_kv_norm[0][None, :]
    g_q = a_q_norm[0][None, :]
    g_k = a_k_norm[0][None, :]

    h, kb_lat, vb_lat = _prologue(x, sh1, sc1, g_mix, w_ck, g_kv, w_k, w_v, cos_b, sin_b, True,
                                  min(512, s))
    p_lat = _projection(h, w_p, g_q, g_k, cos_a, sin_a, cos_b, sin_b, True, min(512, s))

    rows_c = b * n_ctx
    ctx_flat = ctx.reshape(1, rows_c, d)
    tab_c = jnp.zeros((rows_c, LANES), F32)
    hc, kb_ctx, vb_ctx = _prologue(ctx_flat, csh1, csc1, g_mix, w_ck, g_kv, w_k, w_v, tab_c, tab_c,
                                   False, min(512, rows_c))
    p_ctx = _projection(hc, w_p, g_q, g_k, tab_c, tab_c, tab_c, tab_c, False, min(512, rows_c))
    p_ctx = p_ctx.reshape(b, n_ctx, P_COLS)
    kb_ctx = kb_ctx.reshape(b, n_ctx, -1)
    vb_ctx = vb_ctx.reshape(b, n_ctx, -1)

    tk = min(512, s)
    oa = _gqa_attention(p_lat, p_ctx, min(256, s), tk)
    ob = _mla_attention(p_lat, kb_lat, vb_lat, kb_ctx, vb_ctx, min(1024, s), tk)

    x_new, h2, route = _merge_route(
        oa, ob, p_lat, x, g1, sh2, sc2, norm_ffn[0][None, :],
        w_br_a[0].astype(BF16), w_br_b[0].astype(BF16), w_out[0].astype(BF16), w_r, b_r,
        min(256, s))

    n = b * s
    moe_tm = 256
    eid = route[0:2].astype(jnp.int32)
    plan = _moe_plan(eid, route[2:4], moe_tm)
    y2 = _moe(h2.reshape(n, d), *plan, w_e_gate[0].astype(BF16), w_e_up[0].astype(BF16),
              w_e_down[0].astype(BF16), moe_tm)

    return _final(x_new, y2, g2, norm_final[None, :], min(512, s))
```

```python
import functools
import math

import jax
import jax.numpy as jnp
from jax import lax
from jax.experimental import pallas as pl
from jax.experimental.pallas import tpu as pltpu

GRID_W = 64
ROPE_THETA = 10000.0
EPS = 1e-6
A_HEADS = 8
A_KV_HEADS = 2
A_HEAD_DIM = 128
B_HEADS = 8
B_NOPE = 128
B_ROPE = 64
B_V = 128
B_KV_RANK = 512
N_GROUPS = 4
EXPERTS_PER_GROUP = 4
N_EXPERTS = N_GROUPS * EXPERTS_PER_GROUP
D_EXPERT = 1024
N_MOD = 6

LANES = 128
SUBLANES = 8
V7X_VMEM_LIMIT_BYTES = 56 * 1024 * 1024

BF16 = jnp.bfloat16
F32 = jnp.float32

P_GA = 0
P_GB = 2048
P_QA = 4096
P_KA = 5120
P_VA = 5376
P_QBN = 5632
P_QBR = 6656
P_COLS = 7168
PROJ_TN = 512

LOG2_E = math.log2(math.e)
A_SCORE_SCALE = LOG2_E / math.sqrt(A_HEAD_DIM)
B_SCORE_SCALE = LOG2_E / math.sqrt(B_NOPE + B_ROPE)
FLASH_GROUP = 512

TOKEN_TM = 512
MERGE_TM = 256
GQA_TQ = 256
MLA_TQ = 1024
FLASH_TK = 512
MOE_TM = 256


def _cparams(*sem):
    return pltpu.CompilerParams(dimension_semantics=sem, vmem_limit_bytes=V7X_VMEM_LIMIT_BYTES)


def _dot(a, b):
    return jnp.dot(a, b, preferred_element_type=F32)


def _dot_nt(a, b):
    return lax.dot_general(a, b, (((1,), (1,)), ((), ())), preferred_element_type=F32)


def _resident(shape):
    return pl.BlockSpec(shape, lambda *_: (0,) * len(shape), pipeline_mode=pl.Buffered(1))


def _rms(v, gain):
    return v * lax.rsqrt(jnp.mean(v * v, axis=-1, keepdims=True) + EPS) * gain


def _mod_kernel(c_ref, w_ref, b_ref, o_ref):
    c = c_ref[...]
    s = (c * jax.nn.sigmoid(c)).astype(BF16)
    o_ref[...] = _dot(s, w_ref[...].astype(BF16)) + b_ref[...]


def _modulation(cond, w_mod, b_mod):
    rows, d = cond.shape
    n = w_mod.shape[1]
    tn = 1024
    return pl.pallas_call(
        _mod_kernel,
        grid=(n // tn,),
        in_specs=[pl.BlockSpec((rows, d), lambda j: (0, 0)),
                  pl.BlockSpec((d, tn), lambda j: (0, j)),
                  pl.BlockSpec((1, tn), lambda j: (0, j))],
        out_specs=pl.BlockSpec((rows, tn), lambda j: (0, j)),
        out_shape=jax.ShapeDtypeStruct((rows, n), F32),
        compiler_params=_cparams("parallel"),
        name="modulation",
    )(cond, w_mod, b_mod.reshape(1, n))


def _swap_halves_64(v):
    lane = lax.broadcasted_iota(jnp.int32, v.shape, 1)
    return jnp.where((lane & 63) < 32, pltpu.roll(v, LANES - 32, 1), pltpu.roll(v, 32, 1))


def _pre_kernel(use_rope, x_ref, sh_ref, sc_ref, g_ref, wck_ref, gkv_ref, wk_ref, wvt_ref,
                cb_ref, sb_ref, h_ref, kb_ref, vbt_ref):
    xf = x_ref[0]
    h = _rms(xf, g_ref[...]) * (1.0 + sc_ref[0]) + sh_ref[0]
    hb = h.astype(BF16)
    h_ref[0] = hb
    p = _dot(hb, wck_ref[...])
    cn = _rms(p[:, :B_KV_RANK], gkv_ref[...]).astype(BF16)
    kr2 = p[:, B_KV_RANK:]
    if use_rope:
        kr2 = kr2 * cb_ref[...] + _swap_halves_64(kr2) * sb_ref[...]
    kr2 = kr2.astype(BF16)
    kbn = _dot(cn, wk_ref[...]).astype(BF16)
    vbt_ref[0] = _dot_nt(wvt_ref[...], cn).astype(BF16).reshape(vbt_ref.shape[1:])
    for hd in range(B_HEADS):
        kb_ref[0, :, hd * 256:hd * 256 + B_NOPE] = kbn[:, hd * B_NOPE:(hd + 1) * B_NOPE]
        kb_ref[0, :, hd * 256 + B_NOPE:(hd + 1) * 256] = kr2


def _prologue(x, shift, scale, gain, w_ck, g_kv, w_k, w_vt, cos_b, sin_b, use_rope, tm):
    bt, st, d = x.shape
    grid = (bt, st // tm)
    row = lambda b, i: (b, i, 0)
    per_b = lambda b, i: (b, 0, 0)
    fixed = lambda b, i: (0, 0)
    return pl.pallas_call(
        functools.partial(_pre_kernel, use_rope),
        grid=grid,
        in_specs=[pl.BlockSpec((1, tm, d), row),
                  pl.BlockSpec((1, 1, d), per_b),
                  pl.BlockSpec((1, 1, d), per_b),
                  pl.BlockSpec((1, d), fixed),
                  _resident(w_ck.shape),
                  pl.BlockSpec((1, B_KV_RANK), fixed),
                  _resident(w_k.shape),
                  _resident(w_vt.shape),
                  pl.BlockSpec((tm, LANES), lambda b, i: (i, 0)),
                  pl.BlockSpec((tm, LANES), lambda b, i: (i, 0))],
        out_specs=[pl.BlockSpec((1, tm, d), row),
                   pl.BlockSpec((1, tm, B_HEADS * 256), row),
                   pl.BlockSpec((1, B_HEADS, B_V, tm), lambda b, i: (b, 0, 0, i))],
        out_shape=[jax.ShapeDtypeStruct((bt, st, d), BF16),
                   jax.ShapeDtypeStruct((bt, st, B_HEADS * 256), BF16),
                   jax.ShapeDtypeStruct((bt, B_HEADS, B_V, st), BF16)],
        compiler_params=_cparams("parallel", "parallel"),
        name="prologue_rope" if use_rope else "prologue_ctx",
    )(x, shift, scale, gain, w_ck, g_kv, w_k, w_vt, cos_b, sin_b)


def _proj_kernel(use_rope, h_ref, w_ref, gq_ref, gk_ref, ca_ref, sa_ref, cb_ref, sb_ref,
                 o_ref, vat_ref):
    h = h_ref[0]
    n_blk = PROJ_TN // LANES

    def rope_a(v):
        if not use_rope:
            return v
        return v * ca_ref[...] + pltpu.roll(v, A_HEAD_DIM // 2, 1) * sa_ref[...]

    def rope_b(v):
        if not use_rope:
            return v
        return v * cb_ref[...] + _swap_halves_64(v) * sb_ref[...]

    for j in range(P_COLS // PROJ_TN):
        c0 = j * PROJ_TN
        acc = _dot(h, w_ref[:, c0:c0 + PROJ_TN])

        def blk(k):
            return acc[:, k * LANES:(k + 1) * LANES]

        def put(k, v):
            o_ref[0, :, c0 + k * LANES:c0 + (k + 1) * LANES] = v.astype(BF16)

        if c0 < P_QA:
            o_ref[0, :, c0:c0 + PROJ_TN] = jax.nn.sigmoid(acc).astype(BF16)
        elif c0 < P_KA:
            for k in range(n_blk):
                put(k, rope_a(_rms(blk(k), gq_ref[...])) * A_SCORE_SCALE)
        elif c0 < P_QBN:
            for k in range(A_KV_HEADS):
                put(k, rope_a(_rms(blk(k), gk_ref[...])))
            for k in range(A_KV_HEADS, n_blk):
                put(k, blk(k))
                vat_ref[0, k - A_KV_HEADS] = blk(k).T.astype(BF16)
        elif c0 < P_QBR:
            o_ref[0, :, c0:c0 + PROJ_TN] = (acc * B_SCORE_SCALE).astype(BF16)
        else:
            for k in range(n_blk):
                put(k, rope_b(blk(k)) * B_SCORE_SCALE)


def _projection(h, w_p, g_q, g_k, cos_a, sin_a, cos_b, sin_b, use_rope, tm):
    bt, st, d = h.shape
    fixed = lambda b, i: (0, 0)
    tab = lambda b, i: (i, 0)
    return pl.pallas_call(
        functools.partial(_proj_kernel, use_rope),
        grid=(bt, st // tm),
        in_specs=[pl.BlockSpec((1, tm, d), lambda b, i: (b, i, 0)),
                  _resident(w_p.shape),
                  pl.BlockSpec((1, LANES), fixed),
                  pl.BlockSpec((1, LANES), fixed),
                  pl.BlockSpec((tm, LANES), tab),
                  pl.BlockSpec((tm, LANES), tab),
                  pl.BlockSpec((tm, LANES), tab),
                  pl.BlockSpec((tm, LANES), tab)],
        out_specs=[pl.BlockSpec((1, tm, P_COLS), lambda b, i: (b, i, 0)),
                   pl.BlockSpec((1, A_KV_HEADS, A_HEAD_DIM, tm), lambda b, i: (b, 0, 0, i))],
        out_shape=[jax.ShapeDtypeStruct((bt, st, P_COLS), BF16),
                   jax.ShapeDtypeStruct((bt, A_KV_HEADS, A_HEAD_DIM, st), BF16)],
        compiler_params=_cparams("parallel", "parallel"),
        name="projection_rope" if use_rope else "projection_ctx",
    )(h, w_p, g_q, g_k, cos_a, sin_a, cos_b, sin_b)


def _flash(q_ref, kc_ref, vct_ref, kl_ref, vlt_ref, st_ref, acc_ref, tk):
    n_chunks = kl_ref.shape[1] // tk
    m_rows = q_ref.shape[0]
    spans = [(i, i + FLASH_GROUP) for i in range(0, m_rows, FLASH_GROUP)]

    def latent_scores(c, slot):
        k = kl_ref[0, c * tk:(c + 1) * tk, :]
        for lo, hi in spans:
            st_ref[slot, :, lo:hi] = _dot_nt(k, q_ref[lo:hi, :])

    def update(c, slot, stats):
        vt = vlt_ref[0, 0, :, c * tk:(c + 1) * tk]
        out = []
        for (lo, hi), (m, l) in zip(spans, stats):
            st = st_ref[slot, :, lo:hi]
            m_new = jnp.maximum(m, jnp.max(st, axis=0, keepdims=True))
            alpha = jnp.exp2(m - m_new)
            pt = jnp.exp2(st - m_new)
            out.append((m_new, alpha * l + jnp.sum(pt, axis=0, keepdims=True)))
            acc_ref[:, lo:hi] = alpha * acc_ref[:, lo:hi] + _dot(vt, pt.astype(BF16))
        return tuple(out)

    latent_scores(0, 0)
    stats = []
    for lo, hi in spans:
        st = _dot_nt(kc_ref[0], q_ref[lo:hi, :])
        m0 = jnp.max(st, axis=0, keepdims=True)
        pt = jnp.exp2(st - m0)
        stats.append((m0, jnp.sum(pt, axis=0, keepdims=True)))
        acc_ref[:, lo:hi] = _dot(vct_ref[0, 0], pt.astype(BF16))

    stats = tuple(stats)
    for c in range(n_chunks):
        if c + 1 < n_chunks:
            latent_scores(c + 1, (c + 1) & 1)
        stats = update(c, c & 1, stats)
    return jnp.concatenate(
        [(acc_ref[:, lo:hi] / l).T for (lo, hi), (_, l) in zip(spans, stats)], axis=0)


def _flash_scratch(m_rows, dk, dv, tk):
    return [pltpu.VMEM((m_rows, dk), BF16),
            pltpu.VMEM((2, tk, m_rows), F32),
            pltpu.VMEM((dv, m_rows), F32)]


def _gqa_kernel(tk, q_ref, kl_ref, vlt_ref, kc_ref, vct_ref, o_ref, qs_ref, st_ref, acc_ref):
    group = A_HEADS // A_KV_HEADS
    tq = q_ref.shape[1]
    for g in range(group):
        qs_ref[g * tq:(g + 1) * tq, :] = q_ref[0, :, g * A_HEAD_DIM:(g + 1) * A_HEAD_DIM]
    out = _flash(qs_ref, kc_ref, vct_ref, kl_ref, vlt_ref, st_ref, acc_ref, tk)
    for g in range(group):
        o_ref[0, :, g * A_HEAD_DIM:(g + 1) * A_HEAD_DIM] = out[g * tq:(g + 1) * tq].astype(BF16)


def _gqa_attention(p_lat, vat_lat, p_ctx, vat_ctx, tq, tk):
    b, s, _ = p_lat.shape
    n_ctx = p_ctx.shape[1]
    group_w = (A_HEADS // A_KV_HEADS) * A_HEAD_DIM
    return pl.pallas_call(
        functools.partial(_gqa_kernel, tk),
        grid=(b, A_KV_HEADS, s // tq),
        in_specs=[pl.BlockSpec((1, tq, group_w), lambda bb, k, i: (bb, i, P_QA // group_w + k)),
                  pl.BlockSpec((1, s, LANES), lambda bb, k, i: (bb, 0, P_KA // LANES + k)),
                  pl.BlockSpec((1, 1, A_HEAD_DIM, s), lambda bb, k, i: (bb, k, 0, 0)),
                  pl.BlockSpec((1, n_ctx, LANES), lambda bb, k, i: (bb, 0, P_KA // LANES + k)),
                  pl.BlockSpec((1, 1, A_HEAD_DIM, n_ctx), lambda bb, k, i: (bb, k, 0, 0))],
        out_specs=pl.BlockSpec((1, tq, group_w), lambda bb, k, i: (bb, i, k)),
        out_shape=jax.ShapeDtypeStruct((b, s, A_HEADS * A_HEAD_DIM), BF16),
        scratch_shapes=_flash_scratch((A_HEADS // A_KV_HEADS) * tq, A_HEAD_DIM, A_HEAD_DIM, tk),
        compiler_params=_cparams("parallel", "parallel", "parallel"),
        name="gqa_attention",
    )(p_lat, p_lat, vat_lat, p_ctx, vat_ctx)


def _mla_kernel(tk, qn_ref, qr_ref, kl_ref, vlt_ref, kc_ref, vct_ref, o_ref, qs_ref, st_ref, acc_ref):
    hd = pl.program_id(1)
    qr = qr_ref[0]
    lane = lax.broadcasted_iota(jnp.int32, qr.shape, 1)
    qs_ref[:, :B_NOPE] = qn_ref[0]
    qs_ref[:, B_NOPE:] = jnp.where((lane >> 6) == (hd & 1), qr, jnp.zeros_like(qr))
    o_ref[0] = _flash(qs_ref, kc_ref, vct_ref, kl_ref, vlt_ref, st_ref, acc_ref, tk).astype(BF16)


def _mla_attention(p_lat, kb_lat, vbt_lat, kb_ctx, vbt_ctx, tq, tk):
    b, s, _ = p_lat.shape
    n_ctx = kb_ctx.shape[1]
    return pl.pallas_call(
        functools.partial(_mla_kernel, tk),
        grid=(b, B_HEADS, s // tq),
        in_specs=[pl.BlockSpec((1, tq, LANES), lambda bb, h, i: (bb, i, P_QBN // LANES + h)),
                  pl.BlockSpec((1, tq, LANES), lambda bb, h, i: (bb, i, P_QBR // LANES + h // 2)),
                  pl.BlockSpec((1, s, 256), lambda bb, h, i: (bb, 0, h)),
                  pl.BlockSpec((1, 1, B_V, s), lambda bb, h, i: (bb, h, 0, 0)),
                  pl.BlockSpec((1, n_ctx, 256), lambda bb, h, i: (bb, 0, h)),
                  pl.BlockSpec((1, 1, B_V, n_ctx), lambda bb, h, i: (bb, h, 0, 0))],
        out_specs=pl.BlockSpec((1, tq, B_V), lambda bb, h, i: (bb, i, h)),
        out_shape=jax.ShapeDtypeStruct((b, s, B_HEADS * B_V), BF16),
        scratch_shapes=_flash_scratch(tq, 2 * LANES, B_V, tk),
        compiler_params=_cparams("parallel", "parallel", "parallel"),
        name="mla_attention",
    )(p_lat, p_lat, kb_lat, vbt_lat, kb_ctx, vbt_ctx)


def _merge_kernel(oa_ref, ob_ref, ga_ref, gb_ref, x_ref, g1_ref, sh_ref, sc_ref, gn_ref,
                  wa_ref, wb_ref, wo_ref, wr_ref, br_ref, xn_ref, h2_ref, rt_ref, rtt_ref, cnt_ref):
    ya = _dot(oa_ref[0], wa_ref[...])
    yb = _dot(ob_ref[0], wb_ref[...])
    mix = (ga_ref[0].astype(F32) * ya + gb_ref[0].astype(F32) * yb).astype(BF16)
    xn = x_ref[0] + g1_ref[0] * _dot(mix, wo_ref[...])
    xn_ref[0] = xn
    h2 = _rms(xn, gn_ref[...]) * (1.0 + sc_ref[0]) + sh_ref[0]
    h2_ref[0] = h2
    logits = _dot(h2.astype(BF16), wr_ref[...]) + br_ref[...]
    lt = logits.T
    gl = [lt[g:g + 1, :] for g in range(N_GROUPS)]
    gmax = functools.reduce(jnp.maximum, gl)
    gsum = functools.reduce(lambda a, b_: a + b_, [jnp.exp(v - gmax) for v in gl])
    g_val = 1.0 / gsum
    g_idx = jnp.full(gmax.shape, N_GROUPS - 1, jnp.int32)
    for g in range(N_GROUPS - 2, -1, -1):
        g_idx = jnp.where(gl[g] == gmax, g, g_idx)
    el = []
    for e in range(EXPERTS_PER_GROUP):
        v = lt[N_GROUPS + e:N_GROUPS + e + 1, :]
        for g in range(1, N_GROUPS):
            row = N_GROUPS + g * EXPERTS_PER_GROUP + e
            v = jnp.where(g_idx == g, lt[row:row + 1, :], v)
        el.append(v)
    emax = functools.reduce(jnp.maximum, el)
    i1 = jnp.full(emax.shape, EXPERTS_PER_GROUP - 1, jnp.int32)
    for e in range(EXPERTS_PER_GROUP - 2, -1, -1):
        i1 = jnp.where(el[e] == emax, e, i1)
    neg = jnp.full(emax.shape, -jnp.inf, F32)
    el2 = [jnp.where(i1 == e, neg, el[e]) for e in range(EXPERTS_PER_GROUP)]
    emax2 = functools.reduce(jnp.maximum, el2)
    i2 = jnp.full(emax.shape, EXPERTS_PER_GROUP - 1, jnp.int32)
    for e in range(EXPERTS_PER_GROUP - 2, -1, -1):
        i2 = jnp.where(el2[e] == emax2, e, i2)
    p2 = jnp.exp(emax2 - emax)
    w1 = g_val / (1.0 + p2)
    w2 = g_val * p2 / (1.0 + p2)
    e1 = g_idx * EXPERTS_PER_GROUP + i1
    e2 = g_idx * EXPERTS_PER_GROUP + i2
    zero = jnp.zeros_like(w1)
    rt = jnp.concatenate([e1.astype(F32), e2.astype(F32), w1, w2, zero, zero, zero, zero], axis=0)
    rt_ref[...] = rt
    rtt_ref[...] = jnp.concatenate([rt, jnp.zeros((LANES - SUBLANES, rt.shape[1]), F32)], axis=0).T
    lane = lax.broadcasted_iota(jnp.int32, (1, LANES), 1)
    counts = jnp.zeros((1, LANES), F32)
    for e in range(N_EXPERTS):
        hits = jnp.where(e1 == e, 1.0, 0.0) + jnp.where(e2 == e, 1.0, 0.0)
        counts = counts + jnp.where(lane == e, jnp.sum(hits, axis=1, keepdims=True), 0.0)
    cnt_ref[0] = jnp.broadcast_to(counts, (SUBLANES, LANES))


def _merge_route(oa, ob, p_lat, x, g1, sh2, sc2, g_ffn, w_a, w_b, w_o, w_r, b_r, tm):
    b, s, d = x.shape
    n_i = s // tm
    row = lambda bb, i: (bb, i, 0)
    per_b = lambda bb, i: (bb, 0, 0)
    fixed = lambda bb, i: (0, 0)
    return pl.pallas_call(
        _merge_kernel,
        grid=(b, n_i),
        in_specs=[pl.BlockSpec((1, tm, oa.shape[2]), row),
                  pl.BlockSpec((1, tm, ob.shape[2]), row),
                  pl.BlockSpec((1, tm, d), lambda bb, i: (bb, i, P_GA // d)),
                  pl.BlockSpec((1, tm, d), lambda bb, i: (bb, i, P_GB // d)),
                  pl.BlockSpec((1, tm, d), row),
                  pl.BlockSpec((1, 1, d), per_b),
                  pl.BlockSpec((1, 1, d), per_b),
                  pl.BlockSpec((1, 1, d), per_b),
                  pl.BlockSpec((1, d), fixed),
                  _resident(w_a.shape),
                  _resident(w_b.shape),
                  _resident(w_o.shape),
                  _resident(w_r.shape),
                  pl.BlockSpec((1, LANES), fixed)],
        out_specs=[pl.BlockSpec((1, tm, d), row),
                   pl.BlockSpec((1, tm, d), row),
                   pl.BlockSpec((SUBLANES, tm), lambda bb, i: (0, bb * n_i + i)),
                   pl.BlockSpec((tm, LANES), lambda bb, i: (bb * n_i + i, 0)),
                   pl.BlockSpec((1, SUBLANES, LANES), lambda bb, i: (bb * n_i + i, 0, 0))],
        out_shape=[jax.ShapeDtypeStruct((b, s, d), F32),
                   jax.ShapeDtypeStruct((b, s, d), F32),
                   jax.ShapeDtypeStruct((SUBLANES, b * s), F32),
                   jax.ShapeDtypeStruct((b * s, LANES), F32),
                   jax.ShapeDtypeStruct((b * n_i, SUBLANES, LANES), F32)],
        compiler_params=_cparams("parallel", "parallel"),
        name="merge_route",
    )(oa, ob, p_lat, p_lat, x, g1, sh2, sc2, g_ffn, w_a, w_b, w_o, w_r, b_r)


def _moe_kernel(tm, na_ref, te_ref, src_ref, dst_ref, h_hbm, wg_ref, wu_ref, wd_ref,
                y_hbm, xbuf, ybuf, gsem, ssem):
    t = pl.program_id(0)
    n_active = na_ref[0]

    def gather_start(tile, slot):
        for r in range(tm):
            pltpu.make_async_copy(h_hbm.at[pl.ds(src_ref[tile * tm + r], 1)],
                                  xbuf.at[slot, pl.ds(r, 1)], gsem.at[slot]).start()

    def gather_wait(slot):
        pltpu.make_async_copy(h_hbm.at[pl.ds(0, tm)], xbuf.at[slot], gsem.at[slot]).wait()

    def scatter_start(tile, slot):
        for r in range(tm):
            pltpu.make_async_copy(ybuf.at[slot, pl.ds(r, 1)],
                                  y_hbm.at[pl.ds(dst_ref[(tile + 1) * tm + r], 1)],
                                  ssem.at[slot]).start()

    def scatter_wait(slot):
        pltpu.make_async_copy(ybuf.at[slot], y_hbm.at[pl.ds(0, tm)], ssem.at[slot]).wait()

    @pl.when(t == 0)
    def _():
        ybuf[1] = jnp.zeros(ybuf.shape[1:], F32)
        gather_start(0, 0)

    def step(slot):
        other = 1 - slot
        gather_wait(slot)
        gate = _dot(xbuf[slot].astype(BF16), wg_ref[0])
        gather_start(t + 1, other)
        up = _dot(xbuf[slot].astype(BF16), wu_ref[0])
        scatter_start(t - 1, other)
        hid = (gate * jax.nn.sigmoid(gate) * up).astype(BF16)
        ybuf[slot] = _dot(hid, wd_ref[0])
        scatter_wait(other)

    def drain(slot):
        scatter_start(t, slot)
        scatter_wait(slot)
        gather_wait(1 - slot)

    for slot in range(2):
        mine = (t & 1) == slot
        pl.when(mine & (t < n_active))(functools.partial(step, slot))
        pl.when(mine & (t == n_active - 1))(functools.partial(drain, slot))


def _moe(h2, n_active, tile_expert, src_tok, dst_row, w_gate, w_up, w_down, tm):
    n, d = h2.shape
    n_tiles = tile_expert.shape[0]
    w_in_map = lambda t, na, te, sr, ds: (te[t], 0, 0)
    grid_spec = pltpu.PrefetchScalarGridSpec(
        num_scalar_prefetch=4,
        grid=(n_tiles,),
        in_specs=[pl.BlockSpec(memory_space=pl.ANY),
                  pl.BlockSpec((1, d, D_EXPERT), w_in_map),
                  pl.BlockSpec((1, d, D_EXPERT), w_in_map),
                  pl.BlockSpec((1, D_EXPERT, d), w_in_map)],
        out_specs=pl.BlockSpec(memory_space=pl.ANY),
        scratch_shapes=[pltpu.VMEM((2, tm, d), F32),
                        pltpu.VMEM((2, tm, d), F32),
                        pltpu.SemaphoreType.DMA((2,)),
                        pltpu.SemaphoreType.DMA((2,))],
    )
    return pl.pallas_call(
        functools.partial(_moe_kernel, tm),
        grid_spec=grid_spec,
        out_shape=jax.ShapeDtypeStruct((2 * n + tm, d), F32),
        compiler_params=_cparams("arbitrary"),
        name="moe_experts",
    )(n_active, tile_expert, src_tok, dst_row, h2, w_gate, w_up, w_down)


def _moe_plan(eid, counts, tm):
    n = eid.shape[1]
    pairs = 2 * n
    n_tiles = pairs // tm + N_EXPERTS
    _, order = lax.sort((eid.reshape(pairs), lax.iota(jnp.int32, pairs)), num_keys=1)
    padded = ((counts + tm - 1) // tm) * tm
    pad_end = jnp.cumsum(padded)
    pad_start = pad_end - padded
    raw_start = jnp.cumsum(counts) - counts
    tile_row0 = jnp.arange(n_tiles, dtype=jnp.int32) * tm
    tile_expert = jnp.minimum(
        jnp.sum(tile_row0[:, None] >= pad_end[None, :], axis=1), N_EXPERTS - 1).astype(jnp.int32)
    shift = (raw_start - pad_start)[tile_expert]
    limit = (pad_start + counts)[tile_expert]
    pos = jnp.arange(n_tiles * tm, dtype=jnp.int32).reshape(n_tiles, tm)
    valid = pos < limit[:, None]
    pair = jnp.where(valid, order[jnp.clip(pos + shift[:, None], 0, pairs - 1)], 0).reshape(-1)
    valid = valid.reshape(-1)
    dummy = pairs + jnp.arange(tm, dtype=jnp.int32)
    src_tok = jnp.concatenate([pair % n, jnp.zeros((tm,), jnp.int32)]).astype(jnp.int32)
    dst_row = jnp.concatenate([dummy, jnp.where(valid, pair, jnp.tile(dummy, n_tiles))]).astype(jnp.int32)
    n_active = (pad_end[-1:] // tm).astype(jnp.int32)
    return n_active, tile_expert, src_tok, dst_row


def _final_kernel(x_ref, y0_ref, y1_ref, rtt_ref, g2_ref, gn_ref, o_ref):
    wt = rtt_ref[...]
    moe = wt[:, 2:3] * y0_ref[...] + wt[:, 3:4] * y1_ref[...]
    o_ref[0] = _rms(x_ref[0] + g2_ref[0] * moe, gn_ref[...])


def _final(x_new, y2, route, g2, g_final, tm):
    b, s, d = x_new.shape
    row = lambda bb, i: (bb, i, 0)
    n_i = s // tm
    return pl.pallas_call(
        _final_kernel,
        grid=(b, n_i),
        in_specs=[pl.BlockSpec((1, tm, d), row),
                  pl.BlockSpec((tm, d), lambda bb, i: (bb * n_i + i, 0)),
                  pl.BlockSpec((tm, d), lambda bb, i: (b * n_i + bb * n_i + i, 0)),
                  pl.BlockSpec((tm, LANES), lambda bb, i: (bb * n_i + i, 0)),
                  pl.BlockSpec((1, 1, d), lambda bb, i: (bb, 0, 0)),
                  pl.BlockSpec((1, d), lambda bb, i: (0, 0))],
        out_specs=pl.BlockSpec((1, tm, d), row),
        out_shape=jax.ShapeDtypeStruct((b, s, d), F32),
        compiler_params=_cparams("parallel", "parallel"),
        name="final_norm",
    )(x_new, y2, y2, route, g2, g_final)


def _rope_tables(n_tokens, rot_dim):
    rows = n_tokens // GRID_W
    row = jnp.repeat(jnp.arange(rows), GRID_W).astype(F32)
    col = jnp.tile(jnp.arange(GRID_W), rows).astype(F32)
    n_freq = rot_dim // 4
    freqs = ROPE_THETA ** (-jnp.arange(n_freq, dtype=F32) / n_freq)
    ang = jnp.concatenate([row[:, None] * freqs, col[:, None] * freqs], axis=-1)
    cos, sin = jnp.cos(ang), jnp.sin(ang)
    reps = LANES // rot_dim
    cos_t = jnp.tile(jnp.concatenate([cos, cos], axis=-1), (1, reps))
    sin_t = jnp.tile(jnp.concatenate([-sin, sin], axis=-1), (1, reps))
    return cos_t, sin_t


def kernel(x, c, ctx, c_ctx, w_mod, b_mod, norm_mix, norm_ffn, w_in, a_q_norm, a_k_norm, b_kv_norm, w_ukv, w_br_a, w_br_b, w_out, w_group, b_group, w_router, b_router, w_e_gate, w_e_up, w_e_down, norm_final):
    b, s, d = x.shape
    n_ctx = ctx.shape[1]
    assert w_mod.shape[0] == 1, "single-layer block"
    assert s % GRID_W == 0

    wi = w_in[0]
    qa_w = A_HEADS * A_HEAD_DIM
    kv_w = A_KV_HEADS * A_HEAD_DIM
    o_qb = qa_w + 2 * kv_w
    qb_w = B_HEADS * (B_NOPE + B_ROPE)
    o_ckv = o_qb + qb_w
    o_kr = o_ckv + B_KV_RANK
    o_gl = o_kr + B_ROPE
    w_qb = wi[:, o_qb:o_ckv].reshape(d, B_HEADS, B_NOPE + B_ROPE)
    w_p = jnp.concatenate(
        [wi[:, o_gl:], wi[:, :o_qb],
         w_qb[:, :, :B_NOPE].reshape(d, B_HEADS * B_NOPE),
         w_qb[:, :, B_NOPE:].reshape(d, B_HEADS * B_ROPE)], axis=1).astype(BF16)
    w_ck = jnp.concatenate([wi[:, o_ckv:o_kr], wi[:, o_kr:o_gl], wi[:, o_kr:o_gl]], axis=1).astype(BF16)
    w_kv = w_ukv[0].reshape(B_KV_RANK, B_HEADS, B_NOPE + B_V)
    w_k = w_kv[:, :, :B_NOPE].reshape(B_KV_RANK, B_HEADS * B_NOPE).astype(BF16)
    w_vt = w_kv[:, :, B_NOPE:].reshape(B_KV_RANK, B_HEADS * B_V).T.astype(BF16)
    w_r = jnp.concatenate(
        [w_group[0], jnp.transpose(w_router[0], (1, 0, 2)).reshape(d, N_EXPERTS),
         jnp.zeros((d, LANES - N_GROUPS - N_EXPERTS), F32)], axis=1).astype(BF16)
    b_r = jnp.concatenate([b_group[0], b_router[0].reshape(N_EXPERTS),
                           jnp.zeros((LANES - N_GROUPS - N_EXPERTS,), F32)])[None, :]

    cond = jnp.concatenate([c, c_ctx[None, :], jnp.zeros((SUBLANES - b - 1, d), F32)], axis=0)
    mod = _modulation(cond, w_mod[0], b_mod[0])
    mx = mod[:b].reshape(b, N_MOD, 1, d)
    sh1, sc1, g1, sh2, sc2, g2 = [mx[:, k] for k in range(N_MOD)]
    mc = jnp.broadcast_to(mod[b].reshape(N_MOD, 1, 1, d), (N_MOD, b, 1, d))
    csh1, csc1 = mc[0], mc[1]

    cos_a, sin_a = _rope_tables(s, A_HEAD_DIM)
    cos_b, sin_b = _rope_tables(s, B_ROPE)
    g_mix = norm_mix[0][None, :]
    g_kv = b_kv_norm[0][None, :]
    g_q = a_q_norm[0][None, :]
    g_k = a_k_norm[0][None, :]

    tm = min(TOKEN_TM, s)
    h, kb_lat, vbt_lat = _prologue(x, sh1, sc1, g_mix, w_ck, g_kv, w_k, w_vt, cos_b, sin_b, True, tm)
    p_lat, vat_lat = _projection(h, w_p, g_q, g_k, cos_a, sin_a, cos_b, sin_b, True, tm)

    tab_c = jnp.zeros((n_ctx, LANES), F32)
    hc, kb_ctx, vbt_ctx = _prologue(ctx, csh1, csc1, g_mix, w_ck, g_kv, w_k, w_vt, tab_c, tab_c,
                                    False, n_ctx)
    p_ctx, vat_ctx = _projection(hc, w_p, g_q, g_k, tab_c, tab_c, tab_c, tab_c, False, n_ctx)

    tk = min(FLASH_TK, s)
    oa = _gqa_attention(p_lat, vat_lat, p_ctx, vat_ctx, min(GQA_TQ, s), tk)
    ob = _mla_attention(p_lat, kb_lat, vbt_lat, kb_ctx, vbt_ctx, min(MLA_TQ, s), tk)

    x_new, h2, route, route_t, cnt = _merge_route(
        oa, ob, p_lat, x, g1, sh2, sc2, norm_ffn[0][None, :],
        w_br_a[0].astype(BF16), w_br_b[0].astype(BF16), w_out[0].astype(BF16), w_r, b_r,
        min(MERGE_TM, s))

    n = b * s
    counts = jnp.sum(cnt[:, 0, :N_EXPERTS], axis=0).astype(jnp.int32)
    plan = _moe_plan(route[0:2].astype(jnp.int32), counts, MOE_TM)
    y2 = _moe(h2.reshape(n, d), *plan, w_e_gate[0].astype(BF16), w_e_up[0].astype(BF16),
              w_e_down[0].astype(BF16), MOE_TM)

    return _final(x_new, y2, route_t, g2, norm_final[None, :], tm)
```

```python
import functools
import math

import jax
import jax.numpy as jnp
from jax import lax
from jax.experimental import pallas as pl
from jax.experimental.pallas import tpu as pltpu

GRID_W = 64
ROPE_THETA = 10000.0
EPS = 1e-6
A_HEADS = 8
A_KV_HEADS = 2
A_HEAD_DIM = 128
B_HEADS = 8
B_NOPE = 128
B_ROPE = 64
B_V = 128
B_KV_RANK = 512
N_GROUPS = 4
EXPERTS_PER_GROUP = 4
N_EXPERTS = N_GROUPS * EXPERTS_PER_GROUP
D_EXPERT = 1024
N_MOD = 6

LANES = 128
SUBLANES = 8
V7X_VMEM_LIMIT_BYTES = 56 * 1024 * 1024

BF16 = jnp.bfloat16
F32 = jnp.float32

P_GA = 0
P_GB = 2048
P_QA = 4096
P_KA = 5120
P_VA = 5376
P_QBN = 5632
P_QBR = 6656
P_COLS = 7168
PROJ_TN = 512

LOG2_E = math.log2(math.e)
A_SCORE_SCALE = LOG2_E / math.sqrt(A_HEAD_DIM)
B_SCORE_SCALE = LOG2_E / math.sqrt(B_NOPE + B_ROPE)
FLASH_GROUP = 512

TOKEN_TM = 512
MERGE_TM = 256
GQA_TQ = 512
MLA_TQ = 1024
GQA_TK = 512
MLA_TK = 1024
MOE_TM = 256


def _cparams(*sem):
    return pltpu.CompilerParams(dimension_semantics=sem, vmem_limit_bytes=V7X_VMEM_LIMIT_BYTES)


def _dot(a, b):
    return jnp.dot(a, b, preferred_element_type=F32)


def _dot_nt(a, b):
    return lax.dot_general(a, b, (((1,), (1,)), ((), ())), preferred_element_type=F32)


def _resident(shape):
    return pl.BlockSpec(shape, lambda *_: (0,) * len(shape), pipeline_mode=pl.Buffered(1))


def _rms(v, gain):
    return v * lax.rsqrt(jnp.mean(v * v, axis=-1, keepdims=True) + EPS) * gain


def _mod_kernel(c_ref, w_ref, b_ref, o_ref):
    c = c_ref[...]
    s = (c * jax.nn.sigmoid(c)).astype(BF16)
    o_ref[...] = _dot(s, w_ref[...].astype(BF16)) + b_ref[...]


def _modulation(cond, w_mod, b_mod):
    rows, d = cond.shape
    n = w_mod.shape[1]
    tn = 1024
    return pl.pallas_call(
        _mod_kernel,
        grid=(n // tn,),
        in_specs=[pl.BlockSpec((rows, d), lambda j: (0, 0)),
                  pl.BlockSpec((d, tn), lambda j: (0, j)),
                  pl.BlockSpec((1, tn), lambda j: (0, j))],
        out_specs=pl.BlockSpec((rows, tn), lambda j: (0, j)),
        out_shape=jax.ShapeDtypeStruct((rows, n), F32),
        compiler_params=_cparams("parallel"),
        name="modulation",
    )(cond, w_mod, b_mod.reshape(1, n))


def _swap_halves_64(v):
    lane = lax.broadcasted_iota(jnp.int32, v.shape, 1)
    return jnp.where((lane & 63) < 32, pltpu.roll(v, LANES - 32, 1), pltpu.roll(v, 32, 1))


def _pre_kernel(use_rope, x_ref, sh_ref, sc_ref, g_ref, wck_ref, gkv_ref, wk_ref, wvt_ref,
                cb_ref, sb_ref, h_ref, kb_ref, vbt_ref):
    xf = x_ref[0]
    h = _rms(xf, g_ref[...]) * (1.0 + sc_ref[0]) + sh_ref[0]
    hb = h.astype(BF16)
    h_ref[0] = hb
    p = _dot_nt(hb, wck_ref[...])
    cn = _rms(p[:, :B_KV_RANK], gkv_ref[...]).astype(BF16)
    kr2 = p[:, B_KV_RANK:]
    if use_rope:
        kr2 = kr2 * cb_ref[...] + _swap_halves_64(kr2) * sb_ref[...]
    kr2 = kr2.astype(BF16)
    kbn = _dot(cn, wk_ref[...]).astype(BF16)
    vbt_ref[0] = _dot_nt(wvt_ref[...], cn).astype(BF16).reshape(vbt_ref.shape[1:])
    for hd in range(B_HEADS):
        kb_ref[0, :, hd * 256:hd * 256 + B_NOPE] = kbn[:, hd * B_NOPE:(hd + 1) * B_NOPE]
        kb_ref[0, :, hd * 256 + B_NOPE:(hd + 1) * 256] = kr2


def _prologue(x, shift, scale, gain, w_ck, g_kv, w_k, w_vt, cos_b, sin_b, use_rope, tm):
    bt, st, d = x.shape
    grid = (bt, st // tm)
    row = lambda b, i: (b, i, 0)
    per_b = lambda b, i: (b, 0, 0)
    fixed = lambda b, i: (0, 0)
    return pl.pallas_call(
        functools.partial(_pre_kernel, use_rope),
        grid=grid,
        in_specs=[pl.BlockSpec((1, tm, d), row),
                  pl.BlockSpec((1, 1, d), per_b),
                  pl.BlockSpec((1, 1, d), per_b),
                  pl.BlockSpec((1, d), fixed),
                  _resident(w_ck.shape),
                  pl.BlockSpec((1, B_KV_RANK), fixed),
                  _resident(w_k.shape),
                  _resident(w_vt.shape),
                  pl.BlockSpec((tm, LANES), lambda b, i: (i, 0)),
                  pl.BlockSpec((tm, LANES), lambda b, i: (i, 0))],
        out_specs=[pl.BlockSpec((1, tm, d), row),
                   pl.BlockSpec((1, tm, B_HEADS * 256), row),
                   pl.BlockSpec((1, B_HEADS, B_V, tm), lambda b, i: (b, 0, 0, i))],
        out_shape=[jax.ShapeDtypeStruct((bt, st, d), BF16),
                   jax.ShapeDtypeStruct((bt, st, B_HEADS * 256), BF16),
                   jax.ShapeDtypeStruct((bt, B_HEADS, B_V, st), BF16)],
        compiler_params=_cparams("parallel", "parallel"),
        name="prologue_rope" if use_rope else "prologue_ctx",
    )(x, shift, scale, gain, w_ck, g_kv, w_k, w_vt, cos_b, sin_b)


def _proj_kernel(use_rope, h_ref, w_ref, gq_ref, gk_ref, ca_ref, sa_ref, cb_ref, sb_ref,
                 o_ref, vat_ref):
    h = h_ref[0]
    n_blk = PROJ_TN // LANES

    def rope_a(v):
        if not use_rope:
            return v
        return v * ca_ref[...] + pltpu.roll(v, A_HEAD_DIM // 2, 1) * sa_ref[...]

    def rope_b(v):
        if not use_rope:
            return v
        return v * cb_ref[...] + _swap_halves_64(v) * sb_ref[...]

    for j in range(P_COLS // PROJ_TN):
        c0 = j * PROJ_TN
        acc = _dot_nt(h, w_ref[c0:c0 + PROJ_TN, :])

        def blk(k):
            return acc[:, k * LANES:(k + 1) * LANES]

        def put(k, v):
            o_ref[0, :, c0 + k * LANES:c0 + (k + 1) * LANES] = v.astype(BF16)

        if c0 < P_QA:
            o_ref[0, :, c0:c0 + PROJ_TN] = jax.nn.sigmoid(acc).astype(BF16)
        elif c0 < P_KA:
            for k in range(n_blk):
                put(k, rope_a(_rms(blk(k), gq_ref[...])) * A_SCORE_SCALE)
        elif c0 < P_QBN:
            for k in range(A_KV_HEADS):
                put(k, rope_a(_rms(blk(k), gk_ref[...])))
            for k in range(A_KV_HEADS, n_blk):
                put(k, blk(k))
                vat_ref[0, k - A_KV_HEADS] = blk(k).T.astype(BF16)
        elif c0 < P_QBR:
            o_ref[0, :, c0:c0 + PROJ_TN] = (acc * B_SCORE_SCALE).astype(BF16)
        else:
            for k in range(n_blk):
                put(k, rope_b(blk(k)) * B_SCORE_SCALE)


def _projection(h, w_p, g_q, g_k, cos_a, sin_a, cos_b, sin_b, use_rope, tm):
    bt, st, d = h.shape
    fixed = lambda b, i: (0, 0)
    tab = lambda b, i: (i, 0)
    return pl.pallas_call(
        functools.partial(_proj_kernel, use_rope),
        grid=(bt, st // tm),
        in_specs=[pl.BlockSpec((1, tm, d), lambda b, i: (b, i, 0)),
                  _resident(w_p.shape),
                  pl.BlockSpec((1, LANES), fixed),
                  pl.BlockSpec((1, LANES), fixed),
                  pl.BlockSpec((tm, LANES), tab),
                  pl.BlockSpec((tm, LANES), tab),
                  pl.BlockSpec((tm, LANES), tab),
                  pl.BlockSpec((tm, LANES), tab)],
        out_specs=[pl.BlockSpec((1, tm, P_COLS), lambda b, i: (b, i, 0)),
                   pl.BlockSpec((1, A_KV_HEADS, A_HEAD_DIM, tm), lambda b, i: (b, 0, 0, i))],
        out_shape=[jax.ShapeDtypeStruct((bt, st, P_COLS), BF16),
                   jax.ShapeDtypeStruct((bt, A_KV_HEADS, A_HEAD_DIM, st), BF16)],
        compiler_params=_cparams("parallel", "parallel"),
        name="projection_rope" if use_rope else "projection_ctx",
    )(h, w_p, g_q, g_k, cos_a, sin_a, cos_b, sin_b)


def _flash(q_ref, kc_ref, vct_ref, kl_ref, vlt_ref, st_ref, acc_ref, tk):
    n_chunks = kl_ref.shape[1] // tk
    m_rows = q_ref.shape[0]
    spans = [(i, i + FLASH_GROUP) for i in range(0, m_rows, FLASH_GROUP)]

    def latent_scores(c, slot):
        k = kl_ref[0, c * tk:(c + 1) * tk, :]
        for lo, hi in spans:
            st_ref[slot, :, lo:hi] = _dot_nt(k, q_ref[lo:hi, :])

    def update(c, slot, stats):
        vt = vlt_ref[0, 0, :, c * tk:(c + 1) * tk]
        out = []
        for (lo, hi), (m, l) in zip(spans, stats):
            st = st_ref[slot, :, lo:hi]
            m_new = jnp.maximum(m, jnp.max(st, axis=0, keepdims=True))
            alpha = jnp.exp2(m - m_new)
            pt = jnp.exp2(st - m_new)
            out.append((m_new, alpha * l + jnp.sum(pt, axis=0, keepdims=True)))
            acc_ref[:, lo:hi] = alpha * acc_ref[:, lo:hi] + _dot(vt, pt.astype(BF16))
        return tuple(out)

    latent_scores(0, 0)
    stats = []
    for lo, hi in spans:
        st = _dot_nt(kc_ref[0], q_ref[lo:hi, :])
        m0 = jnp.max(st, axis=0, keepdims=True)
        pt = jnp.exp2(st - m0)
        stats.append((m0, jnp.sum(pt, axis=0, keepdims=True)))
        acc_ref[:, lo:hi] = _dot(vct_ref[0, 0], pt.astype(BF16))

    stats = tuple(stats)
    for c in range(n_chunks):
        if c + 1 < n_chunks:
            latent_scores(c + 1, (c + 1) & 1)
        stats = update(c, c & 1, stats)
    return jnp.concatenate(
        [(acc_ref[:, lo:hi] / l).T for (lo, hi), (_, l) in zip(spans, stats)], axis=0)


def _flash_scratch(m_rows, dk, dv, tk):
    return [pltpu.VMEM((m_rows, dk), BF16),
            pltpu.VMEM((2, tk, m_rows), F32),
            pltpu.VMEM((dv, m_rows), F32)]


def _gqa_kernel(tk, q_ref, kl_ref, vlt_ref, kc_ref, vct_ref, o_ref, qs_ref, st_ref, acc_ref):
    group = A_HEADS // A_KV_HEADS
    tq = q_ref.shape[1]
    for g in range(group):
        qs_ref[g * tq:(g + 1) * tq, :] = q_ref[0, :, g * A_HEAD_DIM:(g + 1) * A_HEAD_DIM]
    out = _flash(qs_ref, kc_ref, vct_ref, kl_ref, vlt_ref, st_ref, acc_ref, tk)
    for g in range(group):
        o_ref[0, :, g * A_HEAD_DIM:(g + 1) * A_HEAD_DIM] = out[g * tq:(g + 1) * tq].astype(BF16)


def _gqa_attention(p_lat, vat_lat, p_ctx, vat_ctx, tq, tk):
    b, s, _ = p_lat.shape
    n_ctx = p_ctx.shape[1]
    group_w = (A_HEADS // A_KV_HEADS) * A_HEAD_DIM
    return pl.pallas_call(
        functools.partial(_gqa_kernel, tk),
        grid=(b, A_KV_HEADS, s // tq),
        in_specs=[pl.BlockSpec((1, tq, group_w), lambda bb, k, i: (bb, i, P_QA // group_w + k)),
                  pl.BlockSpec((1, s, LANES), lambda bb, k, i: (bb, 0, P_KA // LANES + k)),
                  pl.BlockSpec((1, 1, A_HEAD_DIM, s), lambda bb, k, i: (bb, k, 0, 0)),
                  pl.BlockSpec((1, n_ctx, LANES), lambda bb, k, i: (bb, 0, P_KA // LANES + k)),
                  pl.BlockSpec((1, 1, A_HEAD_DIM, n_ctx), lambda bb, k, i: (bb, k, 0, 0))],
        out_specs=pl.BlockSpec((1, tq, group_w), lambda bb, k, i: (bb, i, k)),
        out_shape=jax.ShapeDtypeStruct((b, s, A_HEADS * A_HEAD_DIM), BF16),
        scratch_shapes=_flash_scratch((A_HEADS // A_KV_HEADS) * tq, A_HEAD_DIM, A_HEAD_DIM, tk),
        compiler_params=_cparams("parallel", "parallel", "parallel"),
        name="gqa_attention",
    )(p_lat, p_lat, vat_lat, p_ctx, vat_ctx)


def _mla_kernel(tk, qn_ref, qr_ref, kl_ref, vlt_ref, kc_ref, vct_ref, o_ref, qs_ref, st_ref, acc_ref):
    hd = pl.program_id(1)
    qr = qr_ref[0]
    lane = lax.broadcasted_iota(jnp.int32, qr.shape, 1)
    qs_ref[:, :B_NOPE] = qn_ref[0]
    qs_ref[:, B_NOPE:] = jnp.where((lane >> 6) == (hd & 1), qr, jnp.zeros_like(qr))
    o_ref[0] = _flash(qs_ref, kc_ref, vct_ref, kl_ref, vlt_ref, st_ref, acc_ref, tk).astype(BF16)


def _mla_attention(p_lat, kb_lat, vbt_lat, kb_ctx, vbt_ctx, tq, tk):
    b, s, _ = p_lat.shape
    n_ctx = kb_ctx.shape[1]
    return pl.pallas_call(
        functools.partial(_mla_kernel, tk),
        grid=(b, B_HEADS, s // tq),
        in_specs=[pl.BlockSpec((1, tq, LANES), lambda bb, h, i: (bb, i, P_QBN // LANES + h)),
                  pl.BlockSpec((1, tq, LANES), lambda bb, h, i: (bb, i, P_QBR // LANES + h // 2)),
                  pl.BlockSpec((1, s, 256), lambda bb, h, i: (bb, 0, h)),
                  pl.BlockSpec((1, 1, B_V, s), lambda bb, h, i: (bb, h, 0, 0)),
                  pl.BlockSpec((1, n_ctx, 256), lambda bb, h, i: (bb, 0, h)),
                  pl.BlockSpec((1, 1, B_V, n_ctx), lambda bb, h, i: (bb, h, 0, 0))],
        out_specs=pl.BlockSpec((1, tq, B_V), lambda bb, h, i: (bb, i, h)),
        out_shape=jax.ShapeDtypeStruct((b, s, B_HEADS * B_V), BF16),
        scratch_shapes=_flash_scratch(tq, 2 * LANES, B_V, tk),
        compiler_params=_cparams("parallel", "parallel", "parallel"),
        name="mla_attention",
    )(p_lat, p_lat, kb_lat, vbt_lat, kb_ctx, vbt_ctx)


def _merge_kernel(oa_ref, ob_ref, ga_ref, gb_ref, x_ref, g1_ref, sh_ref, sc_ref, gn_ref,
                  wa_ref, wb_ref, wo_ref, wr_ref, br_ref, xn_ref, h2_ref, rt_ref, rtt_ref, cnt_ref):
    ya = _dot(oa_ref[0], wa_ref[...])
    yb = _dot(ob_ref[0], wb_ref[...])
    mix = (ga_ref[0].astype(F32) * ya + gb_ref[0].astype(F32) * yb).astype(BF16)
    xn = x_ref[0] + g1_ref[0] * _dot(mix, wo_ref[...])
    xn_ref[0] = xn
    h2 = _rms(xn, gn_ref[...]) * (1.0 + sc_ref[0]) + sh_ref[0]
    h2_ref[0] = h2
    logits = _dot(h2.astype(BF16), wr_ref[...]) + br_ref[...]
    lt = logits.T
    gl = [lt[g:g + 1, :] for g in range(N_GROUPS)]
    gmax = functools.reduce(jnp.maximum, gl)
    gsum = functools.reduce(lambda a, b_: a + b_, [jnp.exp(v - gmax) for v in gl])
    g_val = 1.0 / gsum
    g_idx = jnp.full(gmax.shape, N_GROUPS - 1, jnp.int32)
    for g in range(N_GROUPS - 2, -1, -1):
        g_idx = jnp.where(gl[g] == gmax, g, g_idx)
    el = []
    for e in range(EXPERTS_PER_GROUP):
        v = lt[N_GROUPS + e:N_GROUPS + e + 1, :]
        for g in range(1, N_GROUPS):
            row = N_GROUPS + g * EXPERTS_PER_GROUP + e
            v = jnp.where(g_idx == g, lt[row:row + 1, :], v)
        el.append(v)
    emax = functools.reduce(jnp.maximum, el)
    i1 = jnp.full(emax.shape, EXPERTS_PER_GROUP - 1, jnp.int32)
    for e in range(EXPERTS_PER_GROUP - 2, -1, -1):
        i1 = jnp.where(el[e] == emax, e, i1)
    neg = jnp.full(emax.shape, -jnp.inf, F32)
    el2 = [jnp.where(i1 == e, neg, el[e]) for e in range(EXPERTS_PER_GROUP)]
    emax2 = functools.reduce(jnp.maximum, el2)
    i2 = jnp.full(emax.shape, EXPERTS_PER_GROUP - 1, jnp.int32)
    for e in range(EXPERTS_PER_GROUP - 2, -1, -1):
        i2 = jnp.where(el2[e] == emax2, e, i2)
    p2 = jnp.exp(emax2 - emax)
    w1 = g_val / (1.0 + p2)
    w2 = g_val * p2 / (1.0 + p2)
    e1 = g_idx * EXPERTS_PER_GROUP + i1
    e2 = g_idx * EXPERTS_PER_GROUP + i2
    zero = jnp.zeros_like(w1)
    rt = jnp.concatenate([e1.astype(F32), e2.astype(F32), w1, w2, zero, zero, zero, zero], axis=0)
    rt_ref[...] = rt
    rtt_ref[...] = jnp.concatenate([rt, jnp.zeros((LANES - SUBLANES, rt.shape[1]), F32)], axis=0).T
    lane = lax.broadcasted_iota(jnp.int32, (1, LANES), 1)
    counts = jnp.zeros((1, LANES), F32)
    for e in range(N_EXPERTS):
        hits = jnp.where(e1 == e, 1.0, 0.0) + jnp.where(e2 == e, 1.0, 0.0)
        counts = counts + jnp.where(lane == e, jnp.sum(hits, axis=1, keepdims=True), 0.0)
    cnt_ref[0] = jnp.broadcast_to(counts, (SUBLANES, LANES))


def _merge_route(oa, ob, p_lat, x, g1, sh2, sc2, g_ffn, w_a, w_b, w_o, w_r, b_r, tm):
    b, s, d = x.shape
    n_i = s // tm
    row = lambda bb, i: (bb, i, 0)
    per_b = lambda bb, i: (bb, 0, 0)
    fixed = lambda bb, i: (0, 0)
    return pl.pallas_call(
        _merge_kernel,
        grid=(b, n_i),
        in_specs=[pl.BlockSpec((1, tm, oa.shape[2]), row),
                  pl.BlockSpec((1, tm, ob.shape[2]), row),
                  pl.BlockSpec((1, tm, d), lambda bb, i: (bb, i, P_GA // d)),
                  pl.BlockSpec((1, tm, d), lambda bb, i: (bb, i, P_GB // d)),
                  pl.BlockSpec((1, tm, d), row),
                  pl.BlockSpec((1, 1, d), per_b),
                  pl.BlockSpec((1, 1, d), per_b),
                  pl.BlockSpec((1, 1, d), per_b),
                  pl.BlockSpec((1, d), fixed),
                  _resident(w_a.shape),
                  _resident(w_b.shape),
                  _resident(w_o.shape),
                  _resident(w_r.shape),
                  pl.BlockSpec((1, LANES), fixed)],
        out_specs=[pl.BlockSpec((1, tm, d), row),
                   pl.BlockSpec((1, tm, d), row),
                   pl.BlockSpec((SUBLANES, tm), lambda bb, i: (0, bb * n_i + i)),
                   pl.BlockSpec((tm, LANES), lambda bb, i: (bb * n_i + i, 0)),
                   pl.BlockSpec((1, SUBLANES, LANES), lambda bb, i: (bb * n_i + i, 0, 0))],
        out_shape=[jax.ShapeDtypeStruct((b, s, d), F32),
                   jax.ShapeDtypeStruct((b, s, d), F32),
                   jax.ShapeDtypeStruct((SUBLANES, b * s), F32),
                   jax.ShapeDtypeStruct((b * s, LANES), F32),
                   jax.ShapeDtypeStruct((b * n_i, SUBLANES, LANES), F32)],
        compiler_params=_cparams("parallel", "parallel"),
        name="merge_route",
    )(oa, ob, p_lat, p_lat, x, g1, sh2, sc2, g_ffn, w_a, w_b, w_o, w_r, b_r)


def _moe_kernel(tm, na_ref, te_ref, src_ref, dst_ref, h_hbm, wg_ref, wu_ref, wd_ref,
                y_hbm, xbuf, ybuf, gsem, ssem):
    t = pl.program_id(0)
    n_active = na_ref[0]

    def gather_start(tile, slot):
        for r in range(tm):
            pltpu.make_async_copy(h_hbm.at[pl.ds(src_ref[tile * tm + r], 1)],
                                  xbuf.at[slot, pl.ds(r, 1)], gsem.at[slot]).start()

    def gather_wait(slot):
        pltpu.make_async_copy(h_hbm.at[pl.ds(0, tm)], xbuf.at[slot], gsem.at[slot]).wait()

    def scatter_start(tile, slot):
        for r in range(tm):
            pltpu.make_async_copy(ybuf.at[slot, pl.ds(r, 1)],
                                  y_hbm.at[pl.ds(dst_ref[(tile + 1) * tm + r], 1)],
                                  ssem.at[slot]).start()

    def scatter_wait(slot):
        pltpu.make_async_copy(ybuf.at[slot], y_hbm.at[pl.ds(0, tm)], ssem.at[slot]).wait()

    @pl.when(t == 0)
    def _():
        ybuf[1] = jnp.zeros(ybuf.shape[1:], F32)
        gather_start(0, 0)

    def step(slot):
        other = 1 - slot
        gather_wait(slot)
        gate = _dot(xbuf[slot].astype(BF16), wg_ref[0])
        gather_start(t + 1, other)
        up = _dot(xbuf[slot].astype(BF16), wu_ref[0])
        scatter_start(t - 1, other)
        hid = (gate * jax.nn.sigmoid(gate) * up).astype(BF16)
        ybuf[slot] = _dot(hid, wd_ref[0])
        scatter_wait(other)

    def drain(slot):
        scatter_start(t, slot)
        scatter_wait(slot)
        gather_wait(1 - slot)

    for slot in range(2):
        mine = (t & 1) == slot
        pl.when(mine & (t < n_active))(functools.partial(step, slot))
        pl.when(mine & (t == n_active - 1))(functools.partial(drain, slot))


def _moe(h2, n_active, tile_expert, src_tok, dst_row, w_gate, w_up, w_down, tm):
    n, d = h2.shape
    n_tiles = tile_expert.shape[0]
    w_in_map = lambda t, na, te, sr, ds: (te[t], 0, 0)
    grid_spec = pltpu.PrefetchScalarGridSpec(
        num_scalar_prefetch=4,
        grid=(n_tiles,),
        in_specs=[pl.BlockSpec(memory_space=pl.ANY),
                  pl.BlockSpec((1, d, D_EXPERT), w_in_map),
                  pl.BlockSpec((1, d, D_EXPERT), w_in_map),
                  pl.BlockSpec((1, D_EXPERT, d), w_in_map)],
        out_specs=pl.BlockSpec(memory_space=pl.ANY),
        scratch_shapes=[pltpu.VMEM((2, tm, d), F32),
                        pltpu.VMEM((2, tm, d), F32),
                        pltpu.SemaphoreType.DMA((2,)),
                        pltpu.SemaphoreType.DMA((2,))],
    )
    return pl.pallas_call(
        functools.partial(_moe_kernel, tm),
        grid_spec=grid_spec,
        out_shape=jax.ShapeDtypeStruct((2 * n + tm, d), F32),
        compiler_params=_cparams("arbitrary"),
        name="moe_experts",
    )(n_active, tile_expert, src_tok, dst_row, h2, w_gate, w_up, w_down)


def _moe_plan(eid, counts, tm):
    n = eid.shape[1]
    pairs = 2 * n
    n_tiles = pairs // tm + N_EXPERTS
    _, order = lax.sort((eid.reshape(pairs), lax.iota(jnp.int32, pairs)), num_keys=1)
    padded = ((counts + tm - 1) // tm) * tm
    pad_end = jnp.cumsum(padded)
    pad_start = pad_end - padded
    raw_start = jnp.cumsum(counts) - counts
    tile_row0 = jnp.arange(n_tiles, dtype=jnp.int32) * tm
    tile_expert = jnp.minimum(
        jnp.sum(tile_row0[:, None] >= pad_end[None, :], axis=1), N_EXPERTS - 1).astype(jnp.int32)
    shift = (raw_start - pad_start)[tile_expert]
    limit = (pad_start + counts)[tile_expert]
    pos = jnp.arange(n_tiles * tm, dtype=jnp.int32).reshape(n_tiles, tm)
    valid = pos < limit[:, None]
    pair = jnp.where(valid, order[jnp.clip(pos + shift[:, None], 0, pairs - 1)], 0).reshape(-1)
    valid = valid.reshape(-1)
    dummy = pairs + jnp.arange(tm, dtype=jnp.int32)
    src_tok = jnp.concatenate([pair % n, jnp.zeros((tm,), jnp.int32)]).astype(jnp.int32)
    dst_row = jnp.concatenate([dummy, jnp.where(valid, pair, jnp.tile(dummy, n_tiles))]).astype(jnp.int32)
    n_active = (pad_end[-1:] // tm).astype(jnp.int32)
    return n_active, tile_expert, src_tok, dst_row


def _final_kernel(x_ref, y0_ref, y1_ref, rtt_ref, g2_ref, gn_ref, o_ref):
    wt = rtt_ref[...]
    moe = wt[:, 2:3] * y0_ref[...] + wt[:, 3:4] * y1_ref[...]
    o_ref[0] = _rms(x_ref[0] + g2_ref[0] * moe, gn_ref[...])


def _final(x_new, y2, route, g2, g_final, tm):
    b, s, d = x_new.shape
    row = lambda bb, i: (bb, i, 0)
    n_i = s // tm
    return pl.pallas_call(
        _final_kernel,
        grid=(b, n_i),
        in_specs=[pl.BlockSpec((1, tm, d), row),
                  pl.BlockSpec((tm, d), lambda bb, i: (bb * n_i + i, 0)),
                  pl.BlockSpec((tm, d), lambda bb, i: (b * n_i + bb * n_i + i, 0)),
                  pl.BlockSpec((tm, LANES), lambda bb, i: (bb * n_i + i, 0)),
                  pl.BlockSpec((1, 1, d), lambda bb, i: (bb, 0, 0)),
                  pl.BlockSpec((1, d), lambda bb, i: (0, 0))],
        out_specs=pl.BlockSpec((1, tm, d), row),
        out_shape=jax.ShapeDtypeStruct((b, s, d), F32),
        compiler_params=_cparams("parallel", "parallel"),
        name="final_norm",
    )(x_new, y2, y2, route, g2, g_final)


def _rope_tables(n_tokens, rot_dim):
    rows = n_tokens // GRID_W
    row = jnp.repeat(jnp.arange(rows), GRID_W).astype(F32)
    col = jnp.tile(jnp.arange(GRID_W), rows).astype(F32)
    n_freq = rot_dim // 4
    freqs = ROPE_THETA ** (-jnp.arange(n_freq, dtype=F32) / n_freq)
    ang = jnp.concatenate([row[:, None] * freqs, col[:, None] * freqs], axis=-1)
    cos, sin = jnp.cos(ang), jnp.sin(ang)
    reps = LANES // rot_dim
    cos_t = jnp.tile(jnp.concatenate([cos, cos], axis=-1), (1, reps))
    sin_t = jnp.tile(jnp.concatenate([-sin, sin], axis=-1), (1, reps))
    return cos_t, sin_t


def kernel(x, c, ctx, c_ctx, w_mod, b_mod, norm_mix, norm_ffn, w_in, a_q_norm, a_k_norm, b_kv_norm, w_ukv, w_br_a, w_br_b, w_out, w_group, b_group, w_router, b_router, w_e_gate, w_e_up, w_e_down, norm_final):
    b, s, d = x.shape
    n_ctx = ctx.shape[1]
    assert w_mod.shape[0] == 1, "single-layer block"
    assert s % GRID_W == 0

    wt = jnp.transpose(w_in[0])
    qa_w = A_HEADS * A_HEAD_DIM
    kv_w = A_KV_HEADS * A_HEAD_DIM
    o_qb = qa_w + 2 * kv_w
    qb_w = B_HEADS * (B_NOPE + B_ROPE)
    o_ckv = o_qb + qb_w
    o_kr = o_ckv + B_KV_RANK
    o_gl = o_kr + B_ROPE
    w_qb = wt[o_qb:o_ckv].reshape(B_HEADS, B_NOPE + B_ROPE, d)
    w_p = jnp.concatenate(
        [wt[o_gl:], wt[:o_qb],
         w_qb[:, :B_NOPE].reshape(B_HEADS * B_NOPE, d),
         w_qb[:, B_NOPE:].reshape(B_HEADS * B_ROPE, d)], axis=0).astype(BF16)
    w_ck = jnp.concatenate([wt[o_ckv:o_kr], wt[o_kr:o_gl], wt[o_kr:o_gl]], axis=0).astype(BF16)
    w_kv = w_ukv[0].reshape(B_KV_RANK, B_HEADS, B_NOPE + B_V)
    w_k = w_kv[:, :, :B_NOPE].reshape(B_KV_RANK, B_HEADS * B_NOPE).astype(BF16)
    w_vt = w_kv[:, :, B_NOPE:].reshape(B_KV_RANK, B_HEADS * B_V).T.astype(BF16)
    w_r = jnp.concatenate(
        [w_group[0], jnp.transpose(w_router[0], (1, 0, 2)).reshape(d, N_EXPERTS),
         jnp.zeros((d, LANES - N_GROUPS - N_EXPERTS), F32)], axis=1).astype(BF16)
    b_r = jnp.concatenate([b_group[0], b_router[0].reshape(N_EXPERTS),
                           jnp.zeros((LANES - N_GROUPS - N_EXPERTS,), F32)])[None, :]

    cond = jnp.concatenate([c, c_ctx[None, :], jnp.zeros((SUBLANES - b - 1, d), F32)], axis=0)
    mod = _modulation(cond, w_mod[0], b_mod[0])
    mx = mod[:b].reshape(b, N_MOD, 1, d)
    sh1, sc1, g1, sh2, sc2, g2 = [mx[:, k] for k in range(N_MOD)]
    mc = jnp.broadcast_to(mod[b].reshape(N_MOD, 1, 1, d), (N_MOD, b, 1, d))
    csh1, csc1 = mc[0], mc[1]

    cos_a, sin_a = _rope_tables(s, A_HEAD_DIM)
    cos_b, sin_b = _rope_tables(s, B_ROPE)
    g_mix = norm_mix[0][None, :]
    g_kv = b_kv_norm[0][None, :]
    g_q = a_q_norm[0][None, :]
    g_k = a_k_norm[0][None, :]

    tm = min(TOKEN_TM, s)
    h, kb_lat, vbt_lat = _prologue(x, sh1, sc1, g_mix, w_ck, g_kv, w_k, w_vt, cos_b, sin_b, True, tm)
    p_lat, vat_lat = _projection(h, w_p, g_q, g_k, cos_a, sin_a, cos_b, sin_b, True, tm)

    tab_c = jnp.zeros((n_ctx, LANES), F32)
    hc, kb_ctx, vbt_ctx = _prologue(ctx, csh1, csc1, g_mix, w_ck, g_kv, w_k, w_vt, tab_c, tab_c,
                                    False, n_ctx)
    p_ctx, vat_ctx = _projection(hc, w_p, g_q, g_k, tab_c, tab_c, tab_c, tab_c, False, n_ctx)

    oa = _gqa_attention(p_lat, vat_lat, p_ctx, vat_ctx, min(GQA_TQ, s), min(GQA_TK, s))
    ob = _mla_attention(p_lat, kb_lat, vbt_lat, kb_ctx, vbt_ctx, min(MLA_TQ, s), min(MLA_TK, s))

    x_new, h2, route, route_t, cnt = _merge_route(
        oa, ob, p_lat, x, g1, sh2, sc2, norm_ffn[0][None, :],
        w_br_a[0].astype(BF16), w_br_b[0].astype(BF16), w_out[0].astype(BF16), w_r, b_r,
        min(MERGE_TM, s))

    n = b * s
    counts = jnp.sum(cnt[:, 0, :N_EXPERTS], axis=0).astype(jnp.int32)
    plan = _moe_plan(route[0:2].astype(jnp.int32), counts, MOE_TM)
    y2 = _moe(h2.reshape(n, d), *plan, w_e_gate[0].astype(BF16), w_e_up[0].astype(BF16),
              w_e_down[0].astype(BF16), MOE_TM)

    return _final(x_new, y2, route_t, g2, norm_final[None, :], tm)
```

```python
import functools
import math

import jax
import jax.numpy as jnp
from jax import lax
from jax.experimental import pallas as pl
from jax.experimental.pallas import tpu as pltpu

GRID_W = 64
ROPE_THETA = 10000.0
EPS = 1e-6
A_HEADS = 8
A_KV_HEADS = 2
A_HEAD_DIM = 128
B_HEADS = 8
B_NOPE = 128
B_ROPE = 64
B_V = 128
B_KV_RANK = 512
N_GROUPS = 4
EXPERTS_PER_GROUP = 4
N_EXPERTS = N_GROUPS * EXPERTS_PER_GROUP
D_EXPERT = 1024
N_MOD = 6

LANES = 128
SUBLANES = 8
V7X_VMEM_LIMIT_BYTES = 56 * 1024 * 1024

BF16 = jnp.bfloat16
F32 = jnp.float32

P_GA = 0
P_GB = 2048
P_QA = 4096
P_KA = 5120
P_VA = 5376
P_QBN = 5632
P_QBR = 6656
P_COLS = 7168
PROJ_TN = 512

LOG2_E = math.log2(math.e)
A_SCORE_SCALE = LOG2_E / math.sqrt(A_HEAD_DIM)
B_SCORE_SCALE = LOG2_E / math.sqrt(B_NOPE + B_ROPE)
FLASH_GROUP = 512

TOKEN_TM = 512
MERGE_TM = 256
GQA_TQ = 512
MLA_TQ = 1024
GQA_TK = 512
MLA_TK = 1024
MOE_TM = 256


def _cparams(*sem):
    return pltpu.CompilerParams(dimension_semantics=sem, vmem_limit_bytes=V7X_VMEM_LIMIT_BYTES)


def _dot(a, b):
    return jnp.dot(a, b, preferred_element_type=F32)


def _dot_nt(a, b):
    return lax.dot_general(a, b, (((1,), (1,)), ((), ())), preferred_element_type=F32)


def _resident(shape):
    return pl.BlockSpec(shape, lambda *_: (0,) * len(shape), pipeline_mode=pl.Buffered(1))


def _rms(v, gain):
    return v * lax.rsqrt(jnp.mean(v * v, axis=-1, keepdims=True) + EPS) * gain


def _mod_kernel(c_ref, w_ref, b_ref, o_ref):
    c = c_ref[...]
    s = (c * jax.nn.sigmoid(c)).astype(BF16)
    o_ref[...] = _dot(s, w_ref[...].astype(BF16)) + b_ref[...]


def _modulation(cond, w_mod, b_mod):
    rows, d = cond.shape
    n = w_mod.shape[1]
    tn = 1024
    return pl.pallas_call(
        _mod_kernel,
        grid=(n // tn,),
        in_specs=[pl.BlockSpec((rows, d), lambda j: (0, 0)),
                  pl.BlockSpec((d, tn), lambda j: (0, j)),
                  pl.BlockSpec((1, tn), lambda j: (0, j))],
        out_specs=pl.BlockSpec((rows, tn), lambda j: (0, j)),
        out_shape=jax.ShapeDtypeStruct((rows, n), F32),
        compiler_params=_cparams("parallel"),
        name="modulation",
    )(cond, w_mod, b_mod.reshape(1, n))


def _swap_halves_64(v):
    lane = lax.broadcasted_iota(jnp.int32, v.shape, 1)
    return jnp.where((lane & 63) < 32, pltpu.roll(v, LANES - 32, 1), pltpu.roll(v, 32, 1))


def _pre_kernel(use_rope, x_ref, sh_ref, sc_ref, g_ref, wck_ref, gkv_ref, wk_ref, wvt_ref,
                cb_ref, sb_ref, h_ref, kb_ref, vbt_ref):
    xf = x_ref[0]
    h = _rms(xf, g_ref[...]) * (1.0 + sc_ref[0]) + sh_ref[0]
    hb = h.astype(BF16)
    h_ref[0] = hb
    p = _dot_nt(hb, wck_ref[...])
    cn = _rms(p[:, :B_KV_RANK], gkv_ref[...]).astype(BF16)
    kr2 = p[:, B_KV_RANK:]
    if use_rope:
        kr2 = kr2 * cb_ref[...] + _swap_halves_64(kr2) * sb_ref[...]
    kr2 = kr2.astype(BF16)
    kbn = _dot(cn, wk_ref[...]).astype(BF16)
    vbt_ref[0] = _dot_nt(wvt_ref[...], cn).astype(BF16).reshape(vbt_ref.shape[1:])
    for hd in range(B_HEADS):
        kb_ref[0, :, hd * 256:hd * 256 + B_NOPE] = kbn[:, hd * B_NOPE:(hd + 1) * B_NOPE]
        kb_ref[0, :, hd * 256 + B_NOPE:(hd + 1) * 256] = kr2


def _prologue(x, shift, scale, gain, w_ck, g_kv, w_k, w_vt, cos_b, sin_b, use_rope, tm):
    bt, st, d = x.shape
    grid = (bt, st // tm)
    row = lambda b, i: (b, i, 0)
    per_b = lambda b, i: (b, 0, 0)
    fixed = lambda b, i: (0, 0)
    return pl.pallas_call(
        functools.partial(_pre_kernel, use_rope),
        grid=grid,
        in_specs=[pl.BlockSpec((1, tm, d), row),
                  pl.BlockSpec((1, 1, d), per_b),
                  pl.BlockSpec((1, 1, d), per_b),
                  pl.BlockSpec((1, d), fixed),
                  _resident(w_ck.shape),
                  pl.BlockSpec((1, B_KV_RANK), fixed),
                  _resident(w_k.shape),
                  _resident(w_vt.shape),
                  pl.BlockSpec((tm, LANES), lambda b, i: (i, 0)),
                  pl.BlockSpec((tm, LANES), lambda b, i: (i, 0))],
        out_specs=[pl.BlockSpec((1, tm, d), row),
                   pl.BlockSpec((1, tm, B_HEADS * 256), row),
                   pl.BlockSpec((1, B_HEADS, B_V, tm), lambda b, i: (b, 0, 0, i))],
        out_shape=[jax.ShapeDtypeStruct((bt, st, d), BF16),
                   jax.ShapeDtypeStruct((bt, st, B_HEADS * 256), BF16),
                   jax.ShapeDtypeStruct((bt, B_HEADS, B_V, st), BF16)],
        compiler_params=_cparams("parallel", "parallel"),
        name="prologue_rope" if use_rope else "prologue_ctx",
    )(x, shift, scale, gain, w_ck, g_kv, w_k, w_vt, cos_b, sin_b)


def _proj_kernel(use_rope, h_ref, w_ref, gq_ref, gk_ref, ca_ref, sa_ref, cb_ref, sb_ref,
                 o_ref, vat_ref):
    h = h_ref[0]
    n_blk = PROJ_TN // LANES

    def rope_a(v):
        if not use_rope:
            return v
        return v * ca_ref[...] + pltpu.roll(v, A_HEAD_DIM // 2, 1) * sa_ref[...]

    def rope_b(v):
        if not use_rope:
            return v
        return v * cb_ref[...] + _swap_halves_64(v) * sb_ref[...]

    for j in range(P_COLS // PROJ_TN):
        c0 = j * PROJ_TN
        acc = _dot_nt(h, w_ref[c0:c0 + PROJ_TN, :])

        def blk(k):
            return acc[:, k * LANES:(k + 1) * LANES]

        def put(k, v):
            o_ref[0, :, c0 + k * LANES:c0 + (k + 1) * LANES] = v.astype(BF16)

        if c0 < P_QA:
            o_ref[0, :, c0:c0 + PROJ_TN] = jax.nn.sigmoid(acc).astype(BF16)
        elif c0 < P_KA:
            for k in range(n_blk):
                put(k, rope_a(_rms(blk(k), gq_ref[...])) * A_SCORE_SCALE)
        elif c0 < P_QBN:
            for k in range(A_KV_HEADS):
                put(k, rope_a(_rms(blk(k), gk_ref[...])))
            for k in range(A_KV_HEADS, n_blk):
                put(k, blk(k))
                vat_ref[0, k - A_KV_HEADS] = blk(k).T.astype(BF16)
        elif c0 < P_QBR:
            o_ref[0, :, c0:c0 + PROJ_TN] = (acc * B_SCORE_SCALE).astype(BF16)
        else:
            for k in range(n_blk):
                put(k, rope_b(blk(k)) * B_SCORE_SCALE)


def _projection(h, w_p, g_q, g_k, cos_a, sin_a, cos_b, sin_b, use_rope, tm):
    bt, st, d = h.shape
    fixed = lambda b, i: (0, 0)
    tab = lambda b, i: (i, 0)
    return pl.pallas_call(
        functools.partial(_proj_kernel, use_rope),
        grid=(bt, st // tm),
        in_specs=[pl.BlockSpec((1, tm, d), lambda b, i: (b, i, 0)),
                  _resident(w_p.shape),
                  pl.BlockSpec((1, LANES), fixed),
                  pl.BlockSpec((1, LANES), fixed),
                  pl.BlockSpec((tm, LANES), tab),
                  pl.BlockSpec((tm, LANES), tab),
                  pl.BlockSpec((tm, LANES), tab),
                  pl.BlockSpec((tm, LANES), tab)],
        out_specs=[pl.BlockSpec((1, tm, P_COLS), lambda b, i: (b, i, 0)),
                   pl.BlockSpec((1, A_KV_HEADS, A_HEAD_DIM, tm), lambda b, i: (b, 0, 0, i))],
        out_shape=[jax.ShapeDtypeStruct((bt, st, P_COLS), BF16),
                   jax.ShapeDtypeStruct((bt, A_KV_HEADS, A_HEAD_DIM, st), BF16)],
        compiler_params=_cparams("parallel", "parallel"),
        name="projection_rope" if use_rope else "projection_ctx",
    )(h, w_p, g_q, g_k, cos_a, sin_a, cos_b, sin_b)


def _flash(q_ref, kc_ref, vct_ref, kl_ref, vlt_ref, st_ref, acc_ref, tk):
    n_chunks = kl_ref.shape[1] // tk
    m_rows = q_ref.shape[0]
    spans = [(i, i + FLASH_GROUP) for i in range(0, m_rows, FLASH_GROUP)]

    def latent_scores(c, slot):
        k = kl_ref[0, c * tk:(c + 1) * tk, :]
        for lo, hi in spans:
            st_ref[slot, :, lo:hi] = _dot_nt(k, q_ref[lo:hi, :])

    def update(c, slot, stats):
        vt = vlt_ref[0, 0, :, c * tk:(c + 1) * tk]
        out = []
        for (lo, hi), (m, l) in zip(spans, stats):
            st = st_ref[slot, :, lo:hi]
            m_new = jnp.maximum(m, jnp.max(st, axis=0, keepdims=True))
            alpha = jnp.exp2(m - m_new)
            pt = jnp.exp2(st - m_new)
            out.append((m_new, alpha * l + jnp.sum(pt, axis=0, keepdims=True)))
            acc_ref[:, lo:hi] = alpha * acc_ref[:, lo:hi] + _dot(vt, pt.astype(BF16))
        return tuple(out)

    latent_scores(0, 0)
    stats = []
    for lo, hi in spans:
        st = _dot_nt(kc_ref[0], q_ref[lo:hi, :])
        m0 = jnp.max(st, axis=0, keepdims=True)
        pt = jnp.exp2(st - m0)
        stats.append((m0, jnp.sum(pt, axis=0, keepdims=True)))
        acc_ref[:, lo:hi] = _dot(vct_ref[0, 0], pt.astype(BF16))

    stats = tuple(stats)
    for c in range(n_chunks):
        if c + 1 < n_chunks:
            latent_scores(c + 1, (c + 1) & 1)
        stats = update(c, c & 1, stats)
    return jnp.concatenate(
        [(acc_ref[:, lo:hi] / l).T for (lo, hi), (_, l) in zip(spans, stats)], axis=0)


def _flash_scratch(m_rows, dk, dv, tk):
    return [pltpu.VMEM((m_rows, dk), BF16),
            pltpu.VMEM((2, tk, m_rows), F32),
            pltpu.VMEM((dv, m_rows), F32)]


class _SideCasts:
    def __init__(self, arrays, grid):
        self.n_steps = math.prod(grid)
        strides = [math.prod(grid[k + 1:]) for k in range(len(grid))]
        step = lambda *g: sum(i * st for i, st in zip(g, strides))
        self.shapes = [a.shape for a in arrays]
        self.views, self.in_specs, self.out_specs, self.out_shapes = [], [], [], []
        for a in arrays:
            cols = a.shape[-1]
            rows = math.prod(a.shape[:-1]) // self.n_steps
            assert rows * self.n_steps == math.prod(a.shape[:-1]) and rows % 16 == 0
            self.views.append(a.reshape(self.n_steps, rows, cols))
            spec = pl.BlockSpec((1, rows, cols), lambda *g: (step(*g), 0, 0))
            self.in_specs.append(spec)
            self.out_specs.append(spec)
            self.out_shapes.append(jax.ShapeDtypeStruct((self.n_steps, rows, cols), BF16))

    @staticmethod
    def run(src_refs, dst_refs):
        for src, dst in zip(src_refs, dst_refs):
            dst[...] = src[...].astype(BF16)

    def restore(self, outs):
        return [o.reshape(shape) for o, shape in zip(outs, self.shapes)]


def _gqa_kernel(tk, n_side, q_ref, kl_ref, vlt_ref, kc_ref, vct_ref, *refs):
    side_src, (o_ref, *side_dst) = refs[:n_side], refs[n_side:2 * n_side + 1]
    qs_ref, st_ref, acc_ref = refs[2 * n_side + 1:]
    _SideCasts.run(side_src, side_dst)
    group = A_HEADS // A_KV_HEADS
    tq = q_ref.shape[1]
    for g in range(group):
        qs_ref[g * tq:(g + 1) * tq, :] = q_ref[0, :, g * A_HEAD_DIM:(g + 1) * A_HEAD_DIM]
    out = _flash(qs_ref, kc_ref, vct_ref, kl_ref, vlt_ref, st_ref, acc_ref, tk)
    for g in range(group):
        o_ref[0, :, g * A_HEAD_DIM:(g + 1) * A_HEAD_DIM] = out[g * tq:(g + 1) * tq].astype(BF16)


def _gqa_attention(p_lat, vat_lat, p_ctx, vat_ctx, tq, tk, cast_arrays):
    b, s, _ = p_lat.shape
    n_ctx = p_ctx.shape[1]
    group_w = (A_HEADS // A_KV_HEADS) * A_HEAD_DIM
    grid = (b, A_KV_HEADS, s // tq)
    side = _SideCasts(cast_arrays, grid)
    oa, *cast = pl.pallas_call(
        functools.partial(_gqa_kernel, tk, len(cast_arrays)),
        grid=grid,
        in_specs=[pl.BlockSpec((1, tq, group_w), lambda bb, k, i: (bb, i, P_QA // group_w + k)),
                  pl.BlockSpec((1, s, LANES), lambda bb, k, i: (bb, 0, P_KA // LANES + k)),
                  pl.BlockSpec((1, 1, A_HEAD_DIM, s), lambda bb, k, i: (bb, k, 0, 0)),
                  pl.BlockSpec((1, n_ctx, LANES), lambda bb, k, i: (bb, 0, P_KA // LANES + k)),
                  pl.BlockSpec((1, 1, A_HEAD_DIM, n_ctx), lambda bb, k, i: (bb, k, 0, 0)),
                  *side.in_specs],
        out_specs=[pl.BlockSpec((1, tq, group_w), lambda bb, k, i: (bb, i, k)), *side.out_specs],
        out_shape=[jax.ShapeDtypeStruct((b, s, A_HEADS * A_HEAD_DIM), BF16), *side.out_shapes],
        scratch_shapes=_flash_scratch((A_HEADS // A_KV_HEADS) * tq, A_HEAD_DIM, A_HEAD_DIM, tk),
        compiler_params=_cparams("parallel", "parallel", "parallel"),
        name="gqa_attention",
    )(p_lat, p_lat, vat_lat, p_ctx, vat_ctx, *side.views)
    return oa, side.restore(cast)


def _mla_kernel(tk, n_side, qn_ref, qr_ref, kl_ref, vlt_ref, kc_ref, vct_ref, *refs):
    side_src, (o_ref, *side_dst) = refs[:n_side], refs[n_side:2 * n_side + 1]
    qs_ref, st_ref, acc_ref = refs[2 * n_side + 1:]
    _SideCasts.run(side_src, side_dst)
    hd = pl.program_id(1)
    qr = qr_ref[0]
    lane = lax.broadcasted_iota(jnp.int32, qr.shape, 1)
    qs_ref[:, :B_NOPE] = qn_ref[0]
    qs_ref[:, B_NOPE:] = jnp.where((lane >> 6) == (hd & 1), qr, jnp.zeros_like(qr))
    o_ref[0] = _flash(qs_ref, kc_ref, vct_ref, kl_ref, vlt_ref, st_ref, acc_ref, tk).astype(BF16)


def _mla_attention(p_lat, kb_lat, vbt_lat, kb_ctx, vbt_ctx, tq, tk, cast_arrays):
    b, s, _ = p_lat.shape
    n_ctx = kb_ctx.shape[1]
    grid = (b, B_HEADS, s // tq)
    side = _SideCasts(cast_arrays, grid)
    ob, *cast = pl.pallas_call(
        functools.partial(_mla_kernel, tk, len(cast_arrays)),
        grid=grid,
        in_specs=[pl.BlockSpec((1, tq, LANES), lambda bb, h, i: (bb, i, P_QBN // LANES + h)),
                  pl.BlockSpec((1, tq, LANES), lambda bb, h, i: (bb, i, P_QBR // LANES + h // 2)),
                  pl.BlockSpec((1, s, 256), lambda bb, h, i: (bb, 0, h)),
                  pl.BlockSpec((1, 1, B_V, s), lambda bb, h, i: (bb, h, 0, 0)),
                  pl.BlockSpec((1, n_ctx, 256), lambda bb, h, i: (bb, 0, h)),
                  pl.BlockSpec((1, 1, B_V, n_ctx), lambda bb, h, i: (bb, h, 0, 0)),
                  *side.in_specs],
        out_specs=[pl.BlockSpec((1, tq, B_V), lambda bb, h, i: (bb, i, h)), *side.out_specs],
        out_shape=[jax.ShapeDtypeStruct((b, s, B_HEADS * B_V), BF16), *side.out_shapes],
        scratch_shapes=_flash_scratch(tq, 2 * LANES, B_V, tk),
        compiler_params=_cparams("parallel", "parallel", "parallel"),
        name="mla_attention",
    )(p_lat, p_lat, kb_lat, vbt_lat, kb_ctx, vbt_ctx, *side.views)
    return ob, side.restore(cast)


def _merge_kernel(oa_ref, ob_ref, ga_ref, gb_ref, x_ref, g1_ref, sh_ref, sc_ref, gn_ref,
                  wa_ref, wb_ref, wo_ref, wr_ref, br_ref, xn_ref, h2_ref, rt_ref, rtt_ref, cnt_ref):
    ya = _dot(oa_ref[0], wa_ref[...])
    yb = _dot(ob_ref[0], wb_ref[...])
    mix = (ga_ref[0].astype(F32) * ya + gb_ref[0].astype(F32) * yb).astype(BF16)
    xn = x_ref[0] + g1_ref[0] * _dot(mix, wo_ref[...])
    xn_ref[0] = xn
    h2 = _rms(xn, gn_ref[...]) * (1.0 + sc_ref[0]) + sh_ref[0]
    h2_ref[0] = h2
    logits = _dot(h2.astype(BF16), wr_ref[...]) + br_ref[...]
    lt = logits.T
    gl = [lt[g:g + 1, :] for g in range(N_GROUPS)]
    gmax = functools.reduce(jnp.maximum, gl)
    gsum = functools.reduce(lambda a, b_: a + b_, [jnp.exp(v - gmax) for v in gl])
    g_val = 1.0 / gsum
    g_idx = jnp.full(gmax.shape, N_GROUPS - 1, jnp.int32)
    for g in range(N_GROUPS - 2, -1, -1):
        g_idx = jnp.where(gl[g] == gmax, g, g_idx)
    el = []
    for e in range(EXPERTS_PER_GROUP):
        v = lt[N_GROUPS + e:N_GROUPS + e + 1, :]
        for g in range(1, N_GROUPS):
            row = N_GROUPS + g * EXPERTS_PER_GROUP + e
            v = jnp.where(g_idx == g, lt[row:row + 1, :], v)
        el.append(v)
    emax = functools.reduce(jnp.maximum, el)
    i1 = jnp.full(emax.shape, EXPERTS_PER_GROUP - 1, jnp.int32)
    for e in range(EXPERTS_PER_GROUP - 2, -1, -1):
        i1 = jnp.where(el[e] == emax, e, i1)
    neg = jnp.full(emax.shape, -jnp.inf, F32)
    el2 = [jnp.where(i1 == e, neg, el[e]) for e in range(EXPERTS_PER_GROUP)]
    emax2 = functools.reduce(jnp.maximum, el2)
    i2 = jnp.full(emax.shape, EXPERTS_PER_GROUP - 1, jnp.int32)
    for e in range(EXPERTS_PER_GROUP - 2, -1, -1):
        i2 = jnp.where(el2[e] == emax2, e, i2)
    p2 = jnp.exp(emax2 - emax)
    w1 = g_val / (1.0 + p2)
    w2 = g_val * p2 / (1.0 + p2)
    e1 = g_idx * EXPERTS_PER_GROUP + i1
    e2 = g_idx * EXPERTS_PER_GROUP + i2
    zero = jnp.zeros_like(w1)
    rt = jnp.concatenate([e1.astype(F32), e2.astype(F32), w1, w2, zero, zero, zero, zero], axis=0)
    rt_ref[...] = rt
    rtt_ref[...] = jnp.concatenate([rt, jnp.zeros((LANES - SUBLANES, rt.shape[1]), F32)], axis=0).T
    lane = lax.broadcasted_iota(jnp.int32, (1, LANES), 1)
    counts = jnp.zeros((1, LANES), F32)
    for e in range(N_EXPERTS):
        hits = jnp.where(e1 == e, 1.0, 0.0) + jnp.where(e2 == e, 1.0, 0.0)
        counts = counts + jnp.where(lane == e, jnp.sum(hits, axis=1, keepdims=True), 0.0)
    cnt_ref[0] = jnp.broadcast_to(counts, (SUBLANES, LANES))


def _merge_route(oa, ob, p_lat, x, g1, sh2, sc2, g_ffn, w_a, w_b, w_o, w_r, b_r, tm):
    b, s, d = x.shape
    n_i = s // tm
    row = lambda bb, i: (bb, i, 0)
    per_b = lambda bb, i: (bb, 0, 0)
    fixed = lambda bb, i: (0, 0)
    return pl.pallas_call(
        _merge_kernel,
        grid=(b, n_i),
        in_specs=[pl.BlockSpec((1, tm, oa.shape[2]), row),
                  pl.BlockSpec((1, tm, ob.shape[2]), row),
                  pl.BlockSpec((1, tm, d), lambda bb, i: (bb, i, P_GA // d)),
                  pl.BlockSpec((1, tm, d), lambda bb, i: (bb, i, P_GB // d)),
                  pl.BlockSpec((1, tm, d), row),
                  pl.BlockSpec((1, 1, d), per_b),
                  pl.BlockSpec((1, 1, d), per_b),
                  pl.BlockSpec((1, 1, d), per_b),
                  pl.BlockSpec((1, d), fixed),
                  _resident(w_a.shape),
                  _resident(w_b.shape),
                  _resident(w_o.shape),
                  _resident(w_r.shape),
                  pl.BlockSpec((1, LANES), fixed)],
        out_specs=[pl.BlockSpec((1, tm, d), row),
                   pl.BlockSpec((1, tm, d), row),
                   pl.BlockSpec((SUBLANES, tm), lambda bb, i: (0, bb * n_i + i)),
                   pl.BlockSpec((tm, LANES), lambda bb, i: (bb * n_i + i, 0)),
                   pl.BlockSpec((1, SUBLANES, LANES), lambda bb, i: (bb * n_i + i, 0, 0))],
        out_shape=[jax.ShapeDtypeStruct((b, s, d), F32),
                   jax.ShapeDtypeStruct((b, s, d), F32),
                   jax.ShapeDtypeStruct((SUBLANES, b * s), F32),
                   jax.ShapeDtypeStruct((b * s, LANES), F32),
                   jax.ShapeDtypeStruct((b * n_i, SUBLANES, LANES), F32)],
        compiler_params=_cparams("parallel", "parallel"),
        name="merge_route",
    )(oa, ob, p_lat, p_lat, x, g1, sh2, sc2, g_ffn, w_a, w_b, w_o, w_r, b_r)


def _moe_kernel(tm, na_ref, te_ref, src_ref, dst_ref, h_hbm, wg_ref, wu_ref, wd_ref,
                y_hbm, xbuf, ybuf, gsem, ssem):
    t = pl.program_id(0)
    n_active = na_ref[0]

    def gather_start(tile, slot):
        for r in range(tm):
            pltpu.make_async_copy(h_hbm.at[pl.ds(src_ref[tile * tm + r], 1)],
                                  xbuf.at[slot, pl.ds(r, 1)], gsem.at[slot]).start()

    def gather_wait(slot):
        pltpu.make_async_copy(h_hbm.at[pl.ds(0, tm)], xbuf.at[slot], gsem.at[slot]).wait()

    def scatter_start(tile, slot):
        for r in range(tm):
            pltpu.make_async_copy(ybuf.at[slot, pl.ds(r, 1)],
                                  y_hbm.at[pl.ds(dst_ref[(tile + 1) * tm + r], 1)],
                                  ssem.at[slot]).start()

    def scatter_wait(slot):
        pltpu.make_async_copy(ybuf.at[slot], y_hbm.at[pl.ds(0, tm)], ssem.at[slot]).wait()

    @pl.when(t == 0)
    def _():
        ybuf[1] = jnp.zeros(ybuf.shape[1:], F32)
        gather_start(0, 0)

    def step(slot):
        other = 1 - slot
        gather_wait(slot)
        gate = _dot(xbuf[slot].astype(BF16), wg_ref[0])
        gather_start(t + 1, other)
        up = _dot(xbuf[slot].astype(BF16), wu_ref[0])
        scatter_start(t - 1, other)
        hid = (gate * jax.nn.sigmoid(gate) * up).astype(BF16)
        ybuf[slot] = _dot(hid, wd_ref[0])
        scatter_wait(other)

    def drain(slot):
        scatter_start(t, slot)
        scatter_wait(slot)
        gather_wait(1 - slot)

    for slot in range(2):
        mine = (t & 1) == slot
        pl.when(mine & (t < n_active))(functools.partial(step, slot))
        pl.when(mine & (t == n_active - 1))(functools.partial(drain, slot))


def _moe(h2, n_active, tile_expert, src_tok, dst_row, w_gate, w_up, w_down, tm):
    n, d = h2.shape
    n_tiles = tile_expert.shape[0]
    w_in_map = lambda t, na, te, sr, ds: (te[t], 0, 0)
    grid_spec = pltpu.PrefetchScalarGridSpec(
        num_scalar_prefetch=4,
        grid=(n_tiles,),
        in_specs=[pl.BlockSpec(memory_space=pl.ANY),
                  pl.BlockSpec((1, d, D_EXPERT), w_in_map),
                  pl.BlockSpec((1, d, D_EXPERT), w_in_map),
                  pl.BlockSpec((1, D_EXPERT, d), w_in_map)],
        out_specs=pl.BlockSpec(memory_space=pl.ANY),
        scratch_shapes=[pltpu.VMEM((2, tm, d), F32),
                        pltpu.VMEM((2, tm, d), F32),
                        pltpu.SemaphoreType.DMA((2,)),
                        pltpu.SemaphoreType.DMA((2,))],
    )
    return pl.pallas_call(
        functools.partial(_moe_kernel, tm),
        grid_spec=grid_spec,
        out_shape=jax.ShapeDtypeStruct((2 * n + tm, d), F32),
        compiler_params=_cparams("arbitrary"),
        name="moe_experts",
    )(n_active, tile_expert, src_tok, dst_row, h2, w_gate, w_up, w_down)


def _moe_plan(eid, counts, tm):
    n = eid.shape[1]
    pairs = 2 * n
    n_tiles = pairs // tm + N_EXPERTS
    _, order = lax.sort((eid.reshape(pairs), lax.iota(jnp.int32, pairs)), num_keys=1)
    padded = ((counts + tm - 1) // tm) * tm
    pad_end = jnp.cumsum(padded)
    pad_start = pad_end - padded
    raw_start = jnp.cumsum(counts) - counts
    tile_row0 = jnp.arange(n_tiles, dtype=jnp.int32) * tm
    tile_expert = jnp.minimum(
        jnp.sum(tile_row0[:, None] >= pad_end[None, :], axis=1), N_EXPERTS - 1).astype(jnp.int32)
    shift = (raw_start - pad_start)[tile_expert]
    limit = (pad_start + counts)[tile_expert]
    pos = jnp.arange(n_tiles * tm, dtype=jnp.int32).reshape(n_tiles, tm)
    valid = pos < limit[:, None]
    pair = jnp.where(valid, order[jnp.clip(pos + shift[:, None], 0, pairs - 1)], 0).reshape(-1)
    valid = valid.reshape(-1)
    dummy = pairs + jnp.arange(tm, dtype=jnp.int32)
    src_tok = jnp.concatenate([pair % n, jnp.zeros((tm,), jnp.int32)]).astype(jnp.int32)
    dst_row = jnp.concatenate([dummy, jnp.where(valid, pair, jnp.tile(dummy, n_tiles))]).astype(jnp.int32)
    n_active = (pad_end[-1:] // tm).astype(jnp.int32)
    return n_active, tile_expert, src_tok, dst_row


def _final_kernel(x_ref, y0_ref, y1_ref, rtt_ref, g2_ref, gn_ref, o_ref):
    wt = rtt_ref[...]
    moe = wt[:, 2:3] * y0_ref[...] + wt[:, 3:4] * y1_ref[...]
    o_ref[0] = _rms(x_ref[0] + g2_ref[0] * moe, gn_ref[...])


def _final(x_new, y2, route, g2, g_final, tm):
    b, s, d = x_new.shape
    row = lambda bb, i: (bb, i, 0)
    n_i = s // tm
    return pl.pallas_call(
        _final_kernel,
        grid=(b, n_i),
        in_specs=[pl.BlockSpec((1, tm, d), row),
                  pl.BlockSpec((tm, d), lambda bb, i: (bb * n_i + i, 0)),
                  pl.BlockSpec((tm, d), lambda bb, i: (b * n_i + bb * n_i + i, 0)),
                  pl.BlockSpec((tm, LANES), lambda bb, i: (bb * n_i + i, 0)),
                  pl.BlockSpec((1, 1, d), lambda bb, i: (bb, 0, 0)),
                  pl.BlockSpec((1, d), lambda bb, i: (0, 0))],
        out_specs=pl.BlockSpec((1, tm, d), row),
        out_shape=jax.ShapeDtypeStruct((b, s, d), F32),
        compiler_params=_cparams("parallel", "parallel"),
        name="final_norm",
    )(x_new, y2, y2, route, g2, g_final)


def _rope_tables(n_tokens, rot_dim):
    rows = n_tokens // GRID_W
    row = jnp.repeat(jnp.arange(rows), GRID_W).astype(F32)
    col = jnp.tile(jnp.arange(GRID_W), rows).astype(F32)
    n_freq = rot_dim // 4
    freqs = ROPE_THETA ** (-jnp.arange(n_freq, dtype=F32) / n_freq)
    ang = jnp.concatenate([row[:, None] * freqs, col[:, None] * freqs], axis=-1)
    cos, sin = jnp.cos(ang), jnp.sin(ang)
    reps = LANES // rot_dim
    cos_t = jnp.tile(jnp.concatenate([cos, cos], axis=-1), (1, reps))
    sin_t = jnp.tile(jnp.concatenate([-sin, sin], axis=-1), (1, reps))
    return cos_t, sin_t


def kernel(x, c, ctx, c_ctx, w_mod, b_mod, norm_mix, norm_ffn, w_in, a_q_norm, a_k_norm, b_kv_norm, w_ukv, w_br_a, w_br_b, w_out, w_group, b_group, w_router, b_router, w_e_gate, w_e_up, w_e_down, norm_final):
    b, s, d = x.shape
    n_ctx = ctx.shape[1]
    assert w_mod.shape[0] == 1, "single-layer block"
    assert s % GRID_W == 0

    wt = jnp.transpose(w_in[0])
    qa_w = A_HEADS * A_HEAD_DIM
    kv_w = A_KV_HEADS * A_HEAD_DIM
    o_qb = qa_w + 2 * kv_w
    qb_w = B_HEADS * (B_NOPE + B_ROPE)
    o_ckv = o_qb + qb_w
    o_kr = o_ckv + B_KV_RANK
    o_gl = o_kr + B_ROPE
    w_qb = wt[o_qb:o_ckv].reshape(B_HEADS, B_NOPE + B_ROPE, d)
    w_p = jnp.concatenate(
        [wt[o_gl:], wt[:o_qb],
         w_qb[:, :B_NOPE].reshape(B_HEADS * B_NOPE, d),
         w_qb[:, B_NOPE:].reshape(B_HEADS * B_ROPE, d)], axis=0).astype(BF16)
    w_ck = jnp.concatenate([wt[o_ckv:o_kr], wt[o_kr:o_gl], wt[o_kr:o_gl]], axis=0).astype(BF16)
    w_kv = w_ukv[0].reshape(B_KV_RANK, B_HEADS, B_NOPE + B_V)
    w_k = w_kv[:, :, :B_NOPE].reshape(B_KV_RANK, B_HEADS * B_NOPE).astype(BF16)
    w_vt = w_kv[:, :, B_NOPE:].reshape(B_KV_RANK, B_HEADS * B_V).T.astype(BF16)
    w_r = jnp.concatenate(
        [w_group[0], jnp.transpose(w_router[0], (1, 0, 2)).reshape(d, N_EXPERTS),
         jnp.zeros((d, LANES - N_GROUPS - N_EXPERTS), F32)], axis=1).astype(BF16)
    b_r = jnp.concatenate([b_group[0], b_router[0].reshape(N_EXPERTS),
                           jnp.zeros((LANES - N_GROUPS - N_EXPERTS,), F32)])[None, :]

    cond = jnp.concatenate([c, c_ctx[None, :], jnp.zeros((SUBLANES - b - 1, d), F32)], axis=0)
    mod = _modulation(cond, w_mod[0], b_mod[0])
    mx = mod[:b].reshape(b, N_MOD, 1, d)
    sh1, sc1, g1, sh2, sc2, g2 = [mx[:, k] for k in range(N_MOD)]
    mc = jnp.broadcast_to(mod[b].reshape(N_MOD, 1, 1, d), (N_MOD, b, 1, d))
    csh1, csc1 = mc[0], mc[1]

    cos_a, sin_a = _rope_tables(s, A_HEAD_DIM)
    cos_b, sin_b = _rope_tables(s, B_ROPE)
    g_mix = norm_mix[0][None, :]
    g_kv = b_kv_norm[0][None, :]
    g_q = a_q_norm[0][None, :]
    g_k = a_k_norm[0][None, :]

    tm = min(TOKEN_TM, s)
    h, kb_lat, vbt_lat = _prologue(x, sh1, sc1, g_mix, w_ck, g_kv, w_k, w_vt, cos_b, sin_b, True, tm)
    p_lat, vat_lat = _projection(h, w_p, g_q, g_k, cos_a, sin_a, cos_b, sin_b, True, tm)

    tab_c = jnp.zeros((n_ctx, LANES), F32)
    hc, kb_ctx, vbt_ctx = _prologue(ctx, csh1, csc1, g_mix, w_ck, g_kv, w_k, w_vt, tab_c, tab_c,
                                    False, n_ctx)
    p_ctx, vat_ctx = _projection(hc, w_p, g_q, g_k, tab_c, tab_c, tab_c, tab_c, False, n_ctx)

    oa, (w_down,) = _gqa_attention(p_lat, vat_lat, p_ctx, vat_ctx, min(GQA_TQ, s), min(GQA_TK, s),
                                   [w_e_down[0]])
    ob, (w_gate, w_up) = _mla_attention(p_lat, kb_lat, vbt_lat, kb_ctx, vbt_ctx, min(MLA_TQ, s),
                                        min(MLA_TK, s), [w_e_gate[0], w_e_up[0]])

    x_new, h2, route, route_t, cnt = _merge_route(
        oa, ob, p_lat, x, g1, sh2, sc2, norm_ffn[0][None, :],
        w_br_a[0].astype(BF16), w_br_b[0].astype(BF16), w_out[0].astype(BF16), w_r, b_r,
        min(MERGE_TM, s))

    n = b * s
    counts = jnp.sum(cnt[:, 0, :N_EXPERTS], axis=0).astype(jnp.int32)
    plan = _moe_plan(route[0:2].astype(jnp.int32), counts, MOE_TM)
    y2 = _moe(h2.reshape(n, d), *plan, w_gate, w_up, w_down, MOE_TM)

    return _final(x_new, y2, route_t, g2, norm_final[None, :], tm)
```

```python
import functools
import math

import jax
import jax.numpy as jnp
import numpy as np
from jax import lax
from jax.experimental import pallas as pl
from jax.experimental.pallas import tpu as pltpu

GRID_W = 64
ROPE_THETA = 10000.0
EPS = 1e-6
A_HEADS = 8
A_KV_HEADS = 2
A_HEAD_DIM = 128
B_HEADS = 8
B_NOPE = 128
B_ROPE = 64
B_V = 128
B_KV_RANK = 512
N_GROUPS = 4
EXPERTS_PER_GROUP = 4
N_EXPERTS = N_GROUPS * EXPERTS_PER_GROUP
D_EXPERT = 1024
N_MOD = 6

LANES = 128
SUBLANES = 8
V7X_VMEM_LIMIT_BYTES = 56 * 1024 * 1024

BF16 = jnp.bfloat16
F32 = jnp.float32

P_GA = 0
P_GB = 2048
P_QA = 4096
P_KA = 5120
P_VA = 5376
P_QBN = 5632
P_QBR = 6656
P_COLS = 7168
PROJ_TN = 512

LOG2_E = math.log2(math.e)
A_SCORE_SCALE = LOG2_E / math.sqrt(A_HEAD_DIM)
B_SCORE_SCALE = LOG2_E / math.sqrt(B_NOPE + B_ROPE)
FLASH_GROUP = 512

TOKEN_TM = 512
MERGE_TM = 256
GQA_TQ = 512
MLA_TQ = 1024
GQA_TK = 512
MLA_TK = 1024
MOE_TM = 256
MOE_LEAD_TILES = 2
MOE_SLOTS = MOE_LEAD_TILES + 1


def _cparams(*sem):
    return pltpu.CompilerParams(dimension_semantics=sem, vmem_limit_bytes=V7X_VMEM_LIMIT_BYTES)


def _dot(a, b):
    return jnp.dot(a, b, preferred_element_type=F32)


def _dot_nt(a, b):
    return lax.dot_general(a, b, (((1,), (1,)), ((), ())), preferred_element_type=F32)


def _resident(shape):
    return pl.BlockSpec(shape, lambda *_: (0,) * len(shape), pipeline_mode=pl.Buffered(1))


def _rms(v, gain):
    return v * lax.rsqrt(jnp.mean(v * v, axis=-1, keepdims=True) + EPS) * gain


def _mod_kernel(c_ref, w_ref, b_ref, o_ref):
    c = c_ref[...]
    s = (c * jax.nn.sigmoid(c)).astype(BF16)
    o_ref[...] = _dot(s, w_ref[...].astype(BF16)) + b_ref[...]


def _modulation(cond, w_mod, b_mod):
    rows, d = cond.shape
    n = w_mod.shape[1]
    tn = 1024
    return pl.pallas_call(
        _mod_kernel,
        grid=(n // tn,),
        in_specs=[pl.BlockSpec((rows, d), lambda j: (0, 0)),
                  pl.BlockSpec((d, tn), lambda j: (0, j)),
                  pl.BlockSpec((1, tn), lambda j: (0, j))],
        out_specs=pl.BlockSpec((rows, tn), lambda j: (0, j)),
        out_shape=jax.ShapeDtypeStruct((rows, n), F32),
        compiler_params=_cparams("parallel"),
        name="modulation",
    )(cond, w_mod, b_mod.reshape(1, n))


def _swap_halves_64(v):
    lane = lax.broadcasted_iota(jnp.int32, v.shape, 1)
    return jnp.where((lane & 63) < 32, pltpu.roll(v, LANES - 32, 1), pltpu.roll(v, 32, 1))


def _pre_kernel(use_rope, x_ref, sh_ref, sc_ref, g_ref, wck_ref, gkv_ref, wk_ref, wvt_ref,
                cb_ref, sb_ref, h_ref, kb_ref, vbt_ref):
    xf = x_ref[0]
    h = _rms(xf, g_ref[...]) * (1.0 + sc_ref[0]) + sh_ref[0]
    hb = h.astype(BF16)
    h_ref[0] = hb
    p = _dot_nt(hb, wck_ref[...])
    cn = _rms(p[:, :B_KV_RANK], gkv_ref[...]).astype(BF16)
    kr2 = p[:, B_KV_RANK:]
    if use_rope:
        kr2 = kr2 * cb_ref[...] + _swap_halves_64(kr2) * sb_ref[...]
    kr2 = kr2.astype(BF16)
    kbn = _dot(cn, wk_ref[...]).astype(BF16)
    vbt_ref[0] = _dot_nt(wvt_ref[...], cn).astype(BF16).reshape(vbt_ref.shape[1:])
    for hd in range(B_HEADS):
        kb_ref[0, :, hd * 256:hd * 256 + B_NOPE] = kbn[:, hd * B_NOPE:(hd + 1) * B_NOPE]
        kb_ref[0, :, hd * 256 + B_NOPE:(hd + 1) * 256] = kr2


def _prologue(x, shift, scale, gain, w_ck, g_kv, w_k, w_vt, cos_b, sin_b, use_rope, tm):
    bt, st, d = x.shape
    grid = (bt, st // tm)
    row = lambda b, i: (b, i, 0)
    per_b = lambda b, i: (b, 0, 0)
    fixed = lambda b, i: (0, 0)
    return pl.pallas_call(
        functools.partial(_pre_kernel, use_rope),
        grid=grid,
        in_specs=[pl.BlockSpec((1, tm, d), row),
                  pl.BlockSpec((1, 1, d), per_b),
                  pl.BlockSpec((1, 1, d), per_b),
                  pl.BlockSpec((1, d), fixed),
                  _resident(w_ck.shape),
                  pl.BlockSpec((1, B_KV_RANK), fixed),
                  _resident(w_k.shape),
                  _resident(w_vt.shape),
                  pl.BlockSpec((tm, LANES), lambda b, i: (i, 0)),
                  pl.BlockSpec((tm, LANES), lambda b, i: (i, 0))],
        out_specs=[pl.BlockSpec((1, tm, d), row),
                   pl.BlockSpec((1, tm, B_HEADS * 256), row),
                   pl.BlockSpec((1, B_HEADS, B_V, tm), lambda b, i: (b, 0, 0, i))],
        out_shape=[jax.ShapeDtypeStruct((bt, st, d), BF16),
                   jax.ShapeDtypeStruct((bt, st, B_HEADS * 256), BF16),
                   jax.ShapeDtypeStruct((bt, B_HEADS, B_V, st), BF16)],
        compiler_params=_cparams("parallel", "parallel"),
        name="prologue_rope" if use_rope else "prologue_ctx",
    )(x, shift, scale, gain, w_ck, g_kv, w_k, w_vt, cos_b, sin_b)


def _proj_kernel(use_rope, h_ref, w_ref, gq_ref, gk_ref, ca_ref, sa_ref, cb_ref, sb_ref,
                 o_ref, vat_ref):
    h = h_ref[0]
    n_blk = PROJ_TN // LANES

    def rope_a(v):
        if not use_rope:
            return v
        return v * ca_ref[...] + pltpu.roll(v, A_HEAD_DIM // 2, 1) * sa_ref[...]

    def rope_b(v):
        if not use_rope:
            return v
        return v * cb_ref[...] + _swap_halves_64(v) * sb_ref[...]

    for j in range(P_COLS // PROJ_TN):
        c0 = j * PROJ_TN
        acc = _dot_nt(h, w_ref[c0:c0 + PROJ_TN, :])

        def blk(k):
            return acc[:, k * LANES:(k + 1) * LANES]

        def put(k, v):
            o_ref[0, :, c0 + k * LANES:c0 + (k + 1) * LANES] = v.astype(BF16)

        if c0 < P_QA:
            o_ref[0, :, c0:c0 + PROJ_TN] = jax.nn.sigmoid(acc).astype(BF16)
        elif c0 < P_KA:
            for k in range(n_blk):
                put(k, rope_a(_rms(blk(k), gq_ref[...])) * A_SCORE_SCALE)
        elif c0 < P_QBN:
            for k in range(A_KV_HEADS):
                put(k, rope_a(_rms(blk(k), gk_ref[...])))
            for k in range(A_KV_HEADS, n_blk):
                put(k, blk(k))
                vat_ref[0, k - A_KV_HEADS] = blk(k).T.astype(BF16)
        elif c0 < P_QBR:
            o_ref[0, :, c0:c0 + PROJ_TN] = (acc * B_SCORE_SCALE).astype(BF16)
        else:
            for k in range(n_blk):
                put(k, rope_b(blk(k)) * B_SCORE_SCALE)


def _projection(h, w_p, g_q, g_k, cos_a, sin_a, cos_b, sin_b, use_rope, tm):
    bt, st, d = h.shape
    fixed = lambda b, i: (0, 0)
    tab = lambda b, i: (i, 0)
    return pl.pallas_call(
        functools.partial(_proj_kernel, use_rope),
        grid=(bt, st // tm),
        in_specs=[pl.BlockSpec((1, tm, d), lambda b, i: (b, i, 0)),
                  _resident(w_p.shape),
                  pl.BlockSpec((1, LANES), fixed),
                  pl.BlockSpec((1, LANES), fixed),
                  pl.BlockSpec((tm, LANES), tab),
                  pl.BlockSpec((tm, LANES), tab),
                  pl.BlockSpec((tm, LANES), tab),
                  pl.BlockSpec((tm, LANES), tab)],
        out_specs=[pl.BlockSpec((1, tm, P_COLS), lambda b, i: (b, i, 0)),
                   pl.BlockSpec((1, A_KV_HEADS, A_HEAD_DIM, tm), lambda b, i: (b, 0, 0, i))],
        out_shape=[jax.ShapeDtypeStruct((bt, st, P_COLS), BF16),
                   jax.ShapeDtypeStruct((bt, A_KV_HEADS, A_HEAD_DIM, st), BF16)],
        compiler_params=_cparams("parallel", "parallel"),
        name="projection_rope" if use_rope else "projection_ctx",
    )(h, w_p, g_q, g_k, cos_a, sin_a, cos_b, sin_b)


def _flash(q_ref, kc_ref, vct_ref, kl_ref, vlt_ref, st_ref, acc_ref, tk):
    n_chunks = kl_ref.shape[1] // tk
    m_rows = q_ref.shape[0]
    spans = [(i, i + FLASH_GROUP) for i in range(0, m_rows, FLASH_GROUP)]

    def latent_scores(c, slot):
        k = kl_ref[0, c * tk:(c + 1) * tk, :]
        for lo, hi in spans:
            st_ref[slot, :, lo:hi] = _dot_nt(k, q_ref[lo:hi, :])

    def update(c, slot, stats):
        vt = vlt_ref[0, 0, :, c * tk:(c + 1) * tk]
        out = []
        for (lo, hi), (m, l) in zip(spans, stats):
            st = st_ref[slot, :, lo:hi]
            m_new = jnp.maximum(m, jnp.max(st, axis=0, keepdims=True))
            alpha = jnp.exp2(m - m_new)
            pt = jnp.exp2(st - m_new)
            out.append((m_new, alpha * l + jnp.sum(pt, axis=0, keepdims=True)))
            acc_ref[:, lo:hi] = alpha * acc_ref[:, lo:hi] + _dot(vt, pt.astype(BF16))
        return tuple(out)

    latent_scores(0, 0)
    stats = []
    for lo, hi in spans:
        st = _dot_nt(kc_ref[0], q_ref[lo:hi, :])
        m0 = jnp.max(st, axis=0, keepdims=True)
        pt = jnp.exp2(st - m0)
        stats.append((m0, jnp.sum(pt, axis=0, keepdims=True)))
        acc_ref[:, lo:hi] = _dot(vct_ref[0, 0], pt.astype(BF16))

    stats = tuple(stats)
    for c in range(n_chunks):
        if c + 1 < n_chunks:
            latent_scores(c + 1, (c + 1) & 1)
        stats = update(c, c & 1, stats)
    return jnp.concatenate(
        [(acc_ref[:, lo:hi] / l).T for (lo, hi), (_, l) in zip(spans, stats)], axis=0)


def _flash_scratch(m_rows, dk, dv, tk):
    return [pltpu.VMEM((m_rows, dk), BF16),
            pltpu.VMEM((2, tk, m_rows), F32),
            pltpu.VMEM((dv, m_rows), F32)]


class _SideCasts:
    def __init__(self, arrays, grid):
        self.n_steps = math.prod(grid)
        strides = [math.prod(grid[k + 1:]) for k in range(len(grid))]
        step = lambda *g: sum(i * st for i, st in zip(g, strides))
        self.shapes = [a.shape for a in arrays]
        self.views, self.in_specs, self.out_specs, self.out_shapes = [], [], [], []
        for a in arrays:
            cols = a.shape[-1]
            rows = math.prod(a.shape[:-1]) // self.n_steps
            assert rows * self.n_steps == math.prod(a.shape[:-1]) and rows % 16 == 0
            self.views.append(a.reshape(self.n_steps, rows, cols))
            spec = pl.BlockSpec((1, rows, cols), lambda *g: (step(*g), 0, 0))
            self.in_specs.append(spec)
            self.out_specs.append(spec)
            self.out_shapes.append(jax.ShapeDtypeStruct((self.n_steps, rows, cols), BF16))

    @staticmethod
    def run(src_refs, dst_refs):
        for src, dst in zip(src_refs, dst_refs):
            dst[...] = src[...].astype(BF16)

    def restore(self, outs):
        return [o.reshape(shape) for o, shape in zip(outs, self.shapes)]


def _gqa_kernel(tk, n_side, q_ref, kl_ref, vlt_ref, kc_ref, vct_ref, *refs):
    side_src, (o_ref, *side_dst) = refs[:n_side], refs[n_side:2 * n_side + 1]
    qs_ref, st_ref, acc_ref = refs[2 * n_side + 1:]
    _SideCasts.run(side_src, side_dst)
    group = A_HEADS // A_KV_HEADS
    tq = q_ref.shape[1]
    for g in range(group):
        qs_ref[g * tq:(g + 1) * tq, :] = q_ref[0, :, g * A_HEAD_DIM:(g + 1) * A_HEAD_DIM]
    out = _flash(qs_ref, kc_ref, vct_ref, kl_ref, vlt_ref, st_ref, acc_ref, tk)
    for g in range(group):
        o_ref[0, :, g * A_HEAD_DIM:(g + 1) * A_HEAD_DIM] = out[g * tq:(g + 1) * tq].astype(BF16)


def _gqa_attention(p_lat, vat_lat, p_ctx, vat_ctx, tq, tk, cast_arrays):
    b, s, _ = p_lat.shape
    n_ctx = p_ctx.shape[1]
    group_w = (A_HEADS // A_KV_HEADS) * A_HEAD_DIM
    grid = (b, A_KV_HEADS, s // tq)
    side = _SideCasts(cast_arrays, grid)
    oa, *cast = pl.pallas_call(
        functools.partial(_gqa_kernel, tk, len(cast_arrays)),
        grid=grid,
        in_specs=[pl.BlockSpec((1, tq, group_w), lambda bb, k, i: (bb, i, P_QA // group_w + k)),
                  pl.BlockSpec((1, s, LANES), lambda bb, k, i: (bb, 0, P_KA // LANES + k)),
                  pl.BlockSpec((1, 1, A_HEAD_DIM, s), lambda bb, k, i: (bb, k, 0, 0)),
                  pl.BlockSpec((1, n_ctx, LANES), lambda bb, k, i: (bb, 0, P_KA // LANES + k)),
                  pl.BlockSpec((1, 1, A_HEAD_DIM, n_ctx), lambda bb, k, i: (bb, k, 0, 0)),
                  *side.in_specs],
        out_specs=[pl.BlockSpec((1, tq, group_w), lambda bb, k, i: (bb, i, k)), *side.out_specs],
        out_shape=[jax.ShapeDtypeStruct((b, s, A_HEADS * A_HEAD_DIM), BF16), *side.out_shapes],
        scratch_shapes=_flash_scratch((A_HEADS // A_KV_HEADS) * tq, A_HEAD_DIM, A_HEAD_DIM, tk),
        compiler_params=_cparams("parallel", "parallel", "parallel"),
        name="gqa_attention",
    )(p_lat, p_lat, vat_lat, p_ctx, vat_ctx, *side.views)
    return oa, side.restore(cast)


def _mla_kernel(tk, n_side, qn_ref, qr_ref, kl_ref, vlt_ref, kc_ref, vct_ref, *refs):
    side_src, (o_ref, *side_dst) = refs[:n_side], refs[n_side:2 * n_side + 1]
    qs_ref, st_ref, acc_ref = refs[2 * n_side + 1:]
    _SideCasts.run(side_src, side_dst)
    hd = pl.program_id(1)
    qr = qr_ref[0]
    lane = lax.broadcasted_iota(jnp.int32, qr.shape, 1)
    qs_ref[:, :B_NOPE] = qn_ref[0]
    qs_ref[:, B_NOPE:] = jnp.where((lane >> 6) == (hd & 1), qr, jnp.zeros_like(qr))
    o_ref[0] = _flash(qs_ref, kc_ref, vct_ref, kl_ref, vlt_ref, st_ref, acc_ref, tk).astype(BF16)


def _mla_attention(p_lat, kb_lat, vbt_lat, kb_ctx, vbt_ctx, tq, tk, cast_arrays):
    b, s, _ = p_lat.shape
    n_ctx = kb_ctx.shape[1]
    grid = (b, B_HEADS, s // tq)
    side = _SideCasts(cast_arrays, grid)
    ob, *cast = pl.pallas_call(
        functools.partial(_mla_kernel, tk, len(cast_arrays)),
        grid=grid,
        in_specs=[pl.BlockSpec((1, tq, LANES), lambda bb, h, i: (bb, i, P_QBN // LANES + h)),
                  pl.BlockSpec((1, tq, LANES), lambda bb, h, i: (bb, i, P_QBR // LANES + h // 2)),
                  pl.BlockSpec((1, s, 256), lambda bb, h, i: (bb, 0, h)),
                  pl.BlockSpec((1, 1, B_V, s), lambda bb, h, i: (bb, h, 0, 0)),
                  pl.BlockSpec((1, n_ctx, 256), lambda bb, h, i: (bb, 0, h)),
                  pl.BlockSpec((1, 1, B_V, n_ctx), lambda bb, h, i: (bb, h, 0, 0)),
                  *side.in_specs],
        out_specs=[pl.BlockSpec((1, tq, B_V), lambda bb, h, i: (bb, i, h)), *side.out_specs],
        out_shape=[jax.ShapeDtypeStruct((b, s, B_HEADS * B_V), BF16), *side.out_shapes],
        scratch_shapes=_flash_scratch(tq, 2 * LANES, B_V, tk),
        compiler_params=_cparams("parallel", "parallel", "parallel"),
        name="mla_attention",
    )(p_lat, p_lat, kb_lat, vbt_lat, kb_ctx, vbt_ctx, *side.views)
    return ob, side.restore(cast)


def _merge_kernel(oa_ref, ob_ref, ga_ref, gb_ref, x_ref, g1_ref, sh_ref, sc_ref, gn_ref,
                  wa_ref, wb_ref, wo_ref, wr_ref, br_ref, xn_ref, h2_ref, rt_ref, rtt_ref, cnt_ref):
    ya = _dot(oa_ref[0], wa_ref[...])
    yb = _dot(ob_ref[0], wb_ref[...])
    mix = (ga_ref[0].astype(F32) * ya + gb_ref[0].astype(F32) * yb).astype(BF16)
    xn = x_ref[0] + g1_ref[0] * _dot(mix, wo_ref[...])
    xn_ref[0] = xn
    h2 = _rms(xn, gn_ref[...]) * (1.0 + sc_ref[0]) + sh_ref[0]
    h2_ref[0] = h2
    logits = _dot(h2.astype(BF16), wr_ref[...]) + br_ref[...]
    lt = logits.T
    gl = [lt[g:g + 1, :] for g in range(N_GROUPS)]
    gmax = functools.reduce(jnp.maximum, gl)
    gsum = functools.reduce(lambda a, b_: a + b_, [jnp.exp(v - gmax) for v in gl])
    g_val = 1.0 / gsum
    g_idx = jnp.full(gmax.shape, N_GROUPS - 1, jnp.int32)
    for g in range(N_GROUPS - 2, -1, -1):
        g_idx = jnp.where(gl[g] == gmax, g, g_idx)
    el = []
    for e in range(EXPERTS_PER_GROUP):
        v = lt[N_GROUPS + e:N_GROUPS + e + 1, :]
        for g in range(1, N_GROUPS):
            row = N_GROUPS + g * EXPERTS_PER_GROUP + e
            v = jnp.where(g_idx == g, lt[row:row + 1, :], v)
        el.append(v)
    emax = functools.reduce(jnp.maximum, el)
    i1 = jnp.full(emax.shape, EXPERTS_PER_GROUP - 1, jnp.int32)
    for e in range(EXPERTS_PER_GROUP - 2, -1, -1):
        i1 = jnp.where(el[e] == emax, e, i1)
    neg = jnp.full(emax.shape, -jnp.inf, F32)
    el2 = [jnp.where(i1 == e, neg, el[e]) for e in range(EXPERTS_PER_GROUP)]
    emax2 = functools.reduce(jnp.maximum, el2)
    i2 = jnp.full(emax.shape, EXPERTS_PER_GROUP - 1, jnp.int32)
    for e in range(EXPERTS_PER_GROUP - 2, -1, -1):
        i2 = jnp.where(el2[e] == emax2, e, i2)
    p2 = jnp.exp(emax2 - emax)
    w1 = g_val / (1.0 + p2)
    w2 = g_val * p2 / (1.0 + p2)
    e1 = g_idx * EXPERTS_PER_GROUP + i1
    e2 = g_idx * EXPERTS_PER_GROUP + i2
    zero = jnp.zeros_like(w1)
    rt = jnp.concatenate([e1.astype(F32), e2.astype(F32), w1, w2, zero, zero, zero, zero], axis=0)
    rt_ref[...] = rt
    rtt_ref[...] = jnp.concatenate([rt, jnp.zeros((LANES - SUBLANES, rt.shape[1]), F32)], axis=0).T
    lane = lax.broadcasted_iota(jnp.int32, (1, LANES), 1)
    counts = jnp.zeros((1, LANES), F32)
    for e in range(N_EXPERTS):
        hits = jnp.where(e1 == e, 1.0, 0.0) + jnp.where(e2 == e, 1.0, 0.0)
        counts = counts + jnp.where(lane == e, jnp.sum(hits, axis=1, keepdims=True), 0.0)
    cnt_ref[0] = jnp.broadcast_to(counts, (SUBLANES, LANES))


def _merge_route(oa, ob, p_lat, x, g1, sh2, sc2, g_ffn, w_a, w_b, w_o, w_r, b_r, tm):
    b, s, d = x.shape
    n_i = s // tm
    row = lambda bb, i: (bb, i, 0)
    per_b = lambda bb, i: (bb, 0, 0)
    fixed = lambda bb, i: (0, 0)
    return pl.pallas_call(
        _merge_kernel,
        grid=(b, n_i),
        in_specs=[pl.BlockSpec((1, tm, oa.shape[2]), row),
                  pl.BlockSpec((1, tm, ob.shape[2]), row),
                  pl.BlockSpec((1, tm, d), lambda bb, i: (bb, i, P_GA // d)),
                  pl.BlockSpec((1, tm, d), lambda bb, i: (bb, i, P_GB // d)),
                  pl.BlockSpec((1, tm, d), row),
                  pl.BlockSpec((1, 1, d), per_b),
                  pl.BlockSpec((1, 1, d), per_b),
                  pl.BlockSpec((1, 1, d), per_b),
                  pl.BlockSpec((1, d), fixed),
                  _resident(w_a.shape),
                  _resident(w_b.shape),
                  _resident(w_o.shape),
                  _resident(w_r.shape),
                  pl.BlockSpec((1, LANES), fixed)],
        out_specs=[pl.BlockSpec((1, tm, d), row),
                   pl.BlockSpec((1, tm, d), row),
                   pl.BlockSpec((SUBLANES, tm), lambda bb, i: (0, bb * n_i + i)),
                   pl.BlockSpec((tm, LANES), lambda bb, i: (bb * n_i + i, 0)),
                   pl.BlockSpec((1, SUBLANES, LANES), lambda bb, i: (bb * n_i + i, 0, 0))],
        out_shape=[jax.ShapeDtypeStruct((b, s, d), F32),
                   jax.ShapeDtypeStruct((b, s, d), F32),
                   jax.ShapeDtypeStruct((SUBLANES, b * s), F32),
                   jax.ShapeDtypeStruct((b * s, LANES), F32),
                   jax.ShapeDtypeStruct((b * n_i, SUBLANES, LANES), F32)],
        compiler_params=_cparams("parallel", "parallel"),
        name="merge_route",
    )(oa, ob, p_lat, p_lat, x, g1, sh2, sc2, g_ffn, w_a, w_b, w_o, w_r, b_r)


def _moe_kernel(tm, na_ref, te_ref, src_ref, dst_ref, h_hbm, wg_ref, wu_ref, wd_ref,
                y_hbm, xbuf, ybuf, gsem, ssem):
    t = pl.program_id(0)
    n_active = na_ref[0]
    phase = lax.rem(t, MOE_SLOTS)

    def gather_start(tile, slot):
        for r in range(tm):
            pltpu.make_async_copy(h_hbm.at[pl.ds(src_ref[tile * tm + r], 1)],
                                  xbuf.at[slot, pl.ds(r, 1)], gsem.at[slot]).start()

    def gather_wait(slot):
        pltpu.make_async_copy(h_hbm.at[pl.ds(0, tm)], xbuf.at[slot], gsem.at[slot]).wait()

    def scatter_start(tile, slot):
        for r in range(tm):
            pltpu.make_async_copy(ybuf.at[slot, pl.ds(r, 1)],
                                  y_hbm.at[pl.ds(dst_ref[(tile + MOE_LEAD_TILES) * tm + r], 1)],
                                  ssem.at[slot]).start()

    def scatter_wait(slot):
        pltpu.make_async_copy(ybuf.at[slot], y_hbm.at[pl.ds(0, tm)], ssem.at[slot]).wait()

    @pl.when(t == 0)
    def _():
        ybuf[1] = jnp.zeros(ybuf.shape[1:], F32)
        ybuf[2] = jnp.zeros(ybuf.shape[1:], F32)
        scatter_start(-2, 1)
        gather_start(0, 0)

    def step(cur):
        nxt, prev = (cur + 1) % MOE_SLOTS, (cur + 2) % MOE_SLOTS
        gather_wait(cur)
        gate = _dot(xbuf[cur].astype(BF16), wg_ref[0])
        gather_start(t + 1, nxt)
        up = _dot(xbuf[cur].astype(BF16), wu_ref[0])
        scatter_start(t - 1, prev)
        hid = (gate * jax.nn.sigmoid(gate) * up).astype(BF16)
        ybuf[cur] = _dot(hid, wd_ref[0])
        scatter_wait(nxt)

    def drain(cur):
        scatter_start(t, cur)
        scatter_wait((cur + 2) % MOE_SLOTS)
        scatter_wait(cur)
        gather_wait((cur + 1) % MOE_SLOTS)

    for cur in range(MOE_SLOTS):
        mine = phase == cur
        pl.when(mine & (t < n_active))(functools.partial(step, cur))
        pl.when(mine & (t == n_active - 1))(functools.partial(drain, cur))


def _moe(h2, n_active, tile_expert, src_tok, dst_row, w_gate, w_up, w_down, tm):
    n, d = h2.shape
    n_tiles = tile_expert.shape[0]
    w_in_map = lambda t, na, te, sr, ds: (te[t], 0, 0)
    grid_spec = pltpu.PrefetchScalarGridSpec(
        num_scalar_prefetch=4,
        grid=(n_tiles,),
        in_specs=[pl.BlockSpec(memory_space=pl.ANY),
                  pl.BlockSpec((1, d, D_EXPERT), w_in_map),
                  pl.BlockSpec((1, d, D_EXPERT), w_in_map),
                  pl.BlockSpec((1, D_EXPERT, d), w_in_map)],
        out_specs=pl.BlockSpec(memory_space=pl.ANY),
        scratch_shapes=[pltpu.VMEM((MOE_SLOTS, tm, d), F32),
                        pltpu.VMEM((MOE_SLOTS, tm, d), F32),
                        pltpu.SemaphoreType.DMA((MOE_SLOTS,)),
                        pltpu.SemaphoreType.DMA((MOE_SLOTS,))],
    )
    return pl.pallas_call(
        functools.partial(_moe_kernel, tm),
        grid_spec=grid_spec,
        out_shape=jax.ShapeDtypeStruct((2 * n + MOE_LEAD_TILES * tm, d), F32),
        compiler_params=_cparams("arbitrary"),
        name="moe_experts",
    )(n_active, tile_expert, src_tok, dst_row, h2, w_gate, w_up, w_down)


def _moe_plan(eid, counts, tm):
    n = eid.shape[1]
    pairs = 2 * n
    n_tiles = pairs // tm + N_EXPERTS
    _, order = lax.sort((eid.reshape(pairs), lax.iota(jnp.int32, pairs)), num_keys=1)
    padded = ((counts + tm - 1) // tm) * tm
    pad_end = jnp.cumsum(padded)
    pad_start = pad_end - padded
    raw_start = jnp.cumsum(counts) - counts
    tile_row0 = jnp.arange(n_tiles, dtype=jnp.int32) * tm
    tile_expert = jnp.minimum(
        jnp.sum(tile_row0[:, None] >= pad_end[None, :], axis=1), N_EXPERTS - 1).astype(jnp.int32)
    shift = (raw_start - pad_start)[tile_expert]
    limit = (pad_start + counts)[tile_expert]
    pos = jnp.arange(n_tiles * tm, dtype=jnp.int32).reshape(n_tiles, tm)
    valid = pos < limit[:, None]
    pair = jnp.where(valid, order[jnp.clip(pos + shift[:, None], 0, pairs - 1)], 0).reshape(-1)
    valid = valid.reshape(-1)
    dummy = pairs + jnp.arange(MOE_LEAD_TILES * tm, dtype=jnp.int32)
    pad_dst = (pairs + ((pos // tm) % MOE_LEAD_TILES) * tm + pos % tm).reshape(-1)
    src_tok = jnp.concatenate([pair % n, jnp.zeros((tm,), jnp.int32)]).astype(jnp.int32)
    dst_row = jnp.concatenate([dummy, jnp.where(valid, pair, pad_dst)]).astype(jnp.int32)
    n_active = (pad_end[-1:] // tm).astype(jnp.int32)
    return n_active, tile_expert, src_tok, dst_row


def _final_kernel(x_ref, y0_ref, y1_ref, rtt_ref, g2_ref, gn_ref, o_ref):
    wt = rtt_ref[...]
    moe = wt[:, 2:3] * y0_ref[...] + wt[:, 3:4] * y1_ref[...]
    o_ref[0] = _rms(x_ref[0] + g2_ref[0] * moe, gn_ref[...])


def _final(x_new, y2, route, g2, g_final, tm):
    b, s, d = x_new.shape
    row = lambda bb, i: (bb, i, 0)
    n_i = s // tm
    return pl.pallas_call(
        _final_kernel,
        grid=(b, n_i),
        in_specs=[pl.BlockSpec((1, tm, d), row),
                  pl.BlockSpec((tm, d), lambda bb, i: (bb * n_i + i, 0)),
                  pl.BlockSpec((tm, d), lambda bb, i: (b * n_i + bb * n_i + i, 0)),
                  pl.BlockSpec((tm, LANES), lambda bb, i: (bb * n_i + i, 0)),
                  pl.BlockSpec((1, 1, d), lambda bb, i: (bb, 0, 0)),
                  pl.BlockSpec((1, d), lambda bb, i: (0, 0))],
        out_specs=pl.BlockSpec((1, tm, d), row),
        out_shape=jax.ShapeDtypeStruct((b, s, d), F32),
        compiler_params=_cparams("parallel", "parallel"),
        name="final_norm",
    )(x_new, y2, y2, route, g2, g_final)


def _rope_tables(n_tokens, rot_dim):
    rows = n_tokens // GRID_W
    row = np.repeat(np.arange(rows), GRID_W).astype(np.float64)
    col = np.tile(np.arange(GRID_W), rows).astype(np.float64)
    n_freq = rot_dim // 4
    freqs = ROPE_THETA ** (-np.arange(n_freq, dtype=np.float64) / n_freq)
    ang = np.concatenate([row[:, None] * freqs, col[:, None] * freqs], axis=-1)
    cos, sin = np.cos(ang), np.sin(ang)
    reps = LANES // rot_dim
    cos_t = np.tile(np.concatenate([cos, cos], axis=-1), (1, reps))
    sin_t = np.tile(np.concatenate([-sin, sin], axis=-1), (1, reps))
    return jnp.asarray(cos_t, F32), jnp.asarray(sin_t, F32)


def kernel(x, c, ctx, c_ctx, w_mod, b_mod, norm_mix, norm_ffn, w_in, a_q_norm, a_k_norm, b_kv_norm, w_ukv, w_br_a, w_br_b, w_out, w_group, b_group, w_router, b_router, w_e_gate, w_e_up, w_e_down, norm_final):
    b, s, d = x.shape
    n_ctx = ctx.shape[1]
    assert w_mod.shape[0] == 1, "single-layer block"
    assert s % GRID_W == 0

    wt = jnp.transpose(w_in[0])
    qa_w = A_HEADS * A_HEAD_DIM
    kv_w = A_KV_HEADS * A_HEAD_DIM
    o_qb = qa_w + 2 * kv_w
    qb_w = B_HEADS * (B_NOPE + B_ROPE)
    o_ckv = o_qb + qb_w
    o_kr = o_ckv + B_KV_RANK
    o_gl = o_kr + B_ROPE
    w_qb = wt[o_qb:o_ckv].reshape(B_HEADS, B_NOPE + B_ROPE, d)
    w_p = jnp.concatenate(
        [wt[o_gl:], wt[:o_qb],
         w_qb[:, :B_NOPE].reshape(B_HEADS * B_NOPE, d),
         w_qb[:, B_NOPE:].reshape(B_HEADS * B_ROPE, d)], axis=0).astype(BF16)
    w_ck = jnp.concatenate([wt[o_ckv:o_kr], wt[o_kr:o_gl], wt[o_kr:o_gl]], axis=0).astype(BF16)
    w_kv = w_ukv[0].reshape(B_KV_RANK, B_HEADS, B_NOPE + B_V)
    w_k = w_kv[:, :, :B_NOPE].reshape(B_KV_RANK, B_HEADS * B_NOPE).astype(BF16)
    w_vt = w_kv[:, :, B_NOPE:].reshape(B_KV_RANK, B_HEADS * B_V).T.astype(BF16)
    w_r = jnp.concatenate(
        [w_group[0], jnp.transpose(w_router[0], (1, 0, 2)).reshape(d, N_EXPERTS),
         jnp.zeros((d, LANES - N_GROUPS - N_EXPERTS), F32)], axis=1).astype(BF16)
    b_r = jnp.concatenate([b_group[0], b_router[0].reshape(N_EXPERTS),
                           jnp.zeros((LANES - N_GROUPS - N_EXPERTS,), F32)])[None, :]

    cond = jnp.concatenate([c, c_ctx[None, :], jnp.zeros((SUBLANES - b - 1, d), F32)], axis=0)
    mod = _modulation(cond, w_mod[0], b_mod[0])
    mx = mod[:b].reshape(b, N_MOD, 1, d)
    sh1, sc1, g1, sh2, sc2, g2 = [mx[:, k] for k in range(N_MOD)]
    mc = jnp.broadcast_to(mod[b].reshape(N_MOD, 1, 1, d), (N_MOD, b, 1, d))
    csh1, csc1 = mc[0], mc[1]

    cos_a, sin_a = _rope_tables(s, A_HEAD_DIM)
    cos_b, sin_b = _rope_tables(s, B_ROPE)
    g_mix = norm_mix[0][None, :]
    g_kv = b_kv_norm[0][None, :]
    g_q = a_q_norm[0][None, :]
    g_k = a_k_norm[0][None, :]

    tm = min(TOKEN_TM, s)
    h, kb_lat, vbt_lat = _prologue(x, sh1, sc1, g_mix, w_ck, g_kv, w_k, w_vt, cos_b, sin_b, True, tm)
    p_lat, vat_lat = _projection(h, w_p, g_q, g_k, cos_a, sin_a, cos_b, sin_b, True, tm)

    tab_c = jnp.zeros((n_ctx, LANES), F32)
    hc, kb_ctx, vbt_ctx = _prologue(ctx, csh1, csc1, g_mix, w_ck, g_kv, w_k, w_vt, tab_c, tab_c,
                                    False, n_ctx)
    p_ctx, vat_ctx = _projection(hc, w_p, g_q, g_k, tab_c, tab_c, tab_c, tab_c, False, n_ctx)

    oa, (w_down,) = _gqa_attention(p_lat, vat_lat, p_ctx, vat_ctx, min(GQA_TQ, s), min(GQA_TK, s),
                                   [w_e_down[0]])
    ob, (w_gate, w_up) = _mla_attention(p_lat, kb_lat, vbt_lat, kb_ctx, vbt_ctx, min(MLA_TQ, s),
                                        min(MLA_TK, s), [w_e_gate[0], w_e_up[0]])

    x_new, h2, route, route_t, cnt = _merge_route(
        oa, ob, p_lat, x, g1, sh2, sc2, norm_ffn[0][None, :],
        w_br_a[0].astype(BF16), w_br_b[0].astype(BF16), w_out[0].astype(BF16), w_r, b_r,
        min(MERGE_TM, s))

    n = b * s
    counts = jnp.sum(cnt[:, 0, :N_EXPERTS], axis=0).astype(jnp.int32)
    plan = _moe_plan(route[0:2].astype(jnp.int32), counts, MOE_TM)
    y2 = _moe(h2.reshape(n, d), *plan, w_gate, w_up, w_down, MOE_TM)

    return _final(x_new, y2, route_t, g2, norm_final[None, :], tm)
```

```python
import functools
import math

import jax
import jax.numpy as jnp
import numpy as np
from jax import lax
from jax.experimental import pallas as pl
from jax.experimental.pallas import tpu as pltpu

GRID_W = 64
ROPE_THETA = 10000.0
EPS = 1e-6
A_HEADS = 8
A_KV_HEADS = 2
A_HEAD_DIM = 128
B_HEADS = 8
B_NOPE = 128
B_ROPE = 64
B_V = 128
B_KV_RANK = 512
N_GROUPS = 4
EXPERTS_PER_GROUP = 4
N_EXPERTS = N_GROUPS * EXPERTS_PER_GROUP
D_EXPERT = 1024
N_MOD = 6

LANES = 128
SUBLANES = 8
V7X_VMEM_LIMIT_BYTES = 56 * 1024 * 1024

BF16 = jnp.bfloat16
F32 = jnp.float32

P_GA = 0
P_GB = 2048
P_QA = 4096
P_KA = 5120
P_VA = 5376
P_QBN = 5632
P_QBR = 6656
P_COLS = 7168
PROJ_TN = 512

LOG2_E = math.log2(math.e)
A_SCORE_SCALE = LOG2_E / math.sqrt(A_HEAD_DIM)
B_SCORE_SCALE = LOG2_E / math.sqrt(B_NOPE + B_ROPE)

TOKEN_TM = 512
MERGE_TM = 256
GQA_TQ = 512
MLA_TQ = 2048
GQA_TK = 512
MLA_TK = 512
MOE_TM = 256
MOE_LEAD_TILES = 1
MOE_SLOTS = MOE_LEAD_TILES + 1


def _cparams(*sem):
    return pltpu.CompilerParams(dimension_semantics=sem, vmem_limit_bytes=V7X_VMEM_LIMIT_BYTES)


def _dot(a, b):
    return jnp.dot(a, b, preferred_element_type=F32)


def _dot_nt(a, b):
    return lax.dot_general(a, b, (((1,), (1,)), ((), ())), preferred_element_type=F32)


def _resident(shape):
    return pl.BlockSpec(shape, lambda *_: (0,) * len(shape), pipeline_mode=pl.Buffered(1))


def _rms(v, gain):
    return v * lax.rsqrt(jnp.mean(v * v, axis=-1, keepdims=True) + EPS) * gain


def _mod_kernel(c_ref, w_ref, b_ref, o_ref):
    c = c_ref[...]
    s = (c * jax.nn.sigmoid(c)).astype(BF16)
    o_ref[...] = _dot(s, w_ref[...].astype(BF16)) + b_ref[...]


def _modulation(cond, w_mod, b_mod):
    rows, d = cond.shape
    n = w_mod.shape[1]
    tn = 1024
    return pl.pallas_call(
        _mod_kernel,
        grid=(n // tn,),
        in_specs=[pl.BlockSpec((rows, d), lambda j: (0, 0)),
                  pl.BlockSpec((d, tn), lambda j: (0, j)),
                  pl.BlockSpec((1, tn), lambda j: (0, j))],
        out_specs=pl.BlockSpec((rows, tn), lambda j: (0, j)),
        out_shape=jax.ShapeDtypeStruct((rows, n), F32),
        compiler_params=_cparams("parallel"),
        name="modulation",
    )(cond, w_mod, b_mod.reshape(1, n))


def _swap_halves_64(v):
    lane = lax.broadcasted_iota(jnp.int32, v.shape, 1)
    return jnp.where((lane & 63) < 32, pltpu.roll(v, LANES - 32, 1), pltpu.roll(v, 32, 1))


def _pre_kernel(use_rope, x_ref, sh_ref, sc_ref, g_ref, wck_ref, gkv_ref, wk_ref, wvt_ref,
                cb_ref, sb_ref, h_ref, kb_ref, vbt_ref):
    xf = x_ref[0]
    h = _rms(xf, g_ref[...]) * (1.0 + sc_ref[0]) + sh_ref[0]
    hb = h.astype(BF16)
    h_ref[0] = hb
    p = _dot_nt(hb, wck_ref[...])
    cn = _rms(p[:, :B_KV_RANK], gkv_ref[...]).astype(BF16)
    kr2 = p[:, B_KV_RANK:]
    if use_rope:
        kr2 = kr2 * cb_ref[...] + _swap_halves_64(kr2) * sb_ref[...]
    kr2 = kr2.astype(BF16)
    kbn = _dot(cn, wk_ref[...]).astype(BF16)
    vbt_ref[0] = _dot_nt(wvt_ref[...], cn).astype(BF16).reshape(vbt_ref.shape[1:])
    for hd in range(B_HEADS):
        kb_ref[0, :, hd * 256:hd * 256 + B_NOPE] = kbn[:, hd * B_NOPE:(hd + 1) * B_NOPE]
        kb_ref[0, :, hd * 256 + B_NOPE:(hd + 1) * 256] = kr2


def _prologue(x, shift, scale, gain, w_ck, g_kv, w_k, w_vt, cos_b, sin_b, use_rope, tm):
    bt, st, d = x.shape
    grid = (bt, st // tm)
    row = lambda b, i: (b, i, 0)
    per_b = lambda b, i: (b, 0, 0)
    fixed = lambda b, i: (0, 0)
    return pl.pallas_call(
        functools.partial(_pre_kernel, use_rope),
        grid=grid,
        in_specs=[pl.BlockSpec((1, tm, d), row),
                  pl.BlockSpec((1, 1, d), per_b),
                  pl.BlockSpec((1, 1, d), per_b),
                  pl.BlockSpec((1, d), fixed),
                  _resident(w_ck.shape),
                  pl.BlockSpec((1, B_KV_RANK), fixed),
                  _resident(w_k.shape),
                  _resident(w_vt.shape),
                  pl.BlockSpec((tm, LANES), lambda b, i: (i, 0)),
                  pl.BlockSpec((tm, LANES), lambda b, i: (i, 0))],
        out_specs=[pl.BlockSpec((1, tm, d), row),
                   pl.BlockSpec((1, tm, B_HEADS * 256), row),
                   pl.BlockSpec((1, B_HEADS, B_V, tm), lambda b, i: (b, 0, 0, i))],
        out_shape=[jax.ShapeDtypeStruct((bt, st, d), BF16),
                   jax.ShapeDtypeStruct((bt, st, B_HEADS * 256), BF16),
                   jax.ShapeDtypeStruct((bt, B_HEADS, B_V, st), BF16)],
        compiler_params=_cparams("parallel", "parallel"),
        name="prologue_rope" if use_rope else "prologue_ctx",
    )(x, shift, scale, gain, w_ck, g_kv, w_k, w_vt, cos_b, sin_b)


def _proj_kernel(use_rope, h_ref, w_ref, gq_ref, gk_ref, ca_ref, sa_ref, cb_ref, sb_ref,
                 o_ref, vat_ref):
    h = h_ref[0]
    n_blk = PROJ_TN // LANES

    def rope_a(v):
        if not use_rope:
            return v
        return v * ca_ref[...] + pltpu.roll(v, A_HEAD_DIM // 2, 1) * sa_ref[...]

    def rope_b(v):
        if not use_rope:
            return v
        return v * cb_ref[...] + _swap_halves_64(v) * sb_ref[...]

    for j in range(P_COLS // PROJ_TN):
        c0 = j * PROJ_TN
        acc = _dot_nt(h, w_ref[c0:c0 + PROJ_TN, :])

        def blk(k):
            return acc[:, k * LANES:(k + 1) * LANES]

        def put(k, v):
            o_ref[0, :, c0 + k * LANES:c0 + (k + 1) * LANES] = v.astype(BF16)

        if c0 < P_QA:
            o_ref[0, :, c0:c0 + PROJ_TN] = jax.nn.sigmoid(acc).astype(BF16)
        elif c0 < P_KA:
            for k in range(n_blk):
                put(k, rope_a(_rms(blk(k), gq_ref[...])) * A_SCORE_SCALE)
        elif c0 < P_QBN:
            for k in range(A_KV_HEADS):
                put(k, rope_a(_rms(blk(k), gk_ref[...])))
            for k in range(A_KV_HEADS, n_blk):
                put(k, blk(k))
                vat_ref[0, k - A_KV_HEADS] = blk(k).T.astype(BF16)
        elif c0 < P_QBR:
            o_ref[0, :, c0:c0 + PROJ_TN] = (acc * B_SCORE_SCALE).astype(BF16)
        else:
            for k in range(n_blk):
                put(k, rope_b(blk(k)) * B_SCORE_SCALE)


def _projection(h, w_p, g_q, g_k, cos_a, sin_a, cos_b, sin_b, use_rope, tm):
    bt, st, d = h.shape
    fixed = lambda b, i: (0, 0)
    tab = lambda b, i: (i, 0)
    return pl.pallas_call(
        functools.partial(_proj_kernel, use_rope),
        grid=(bt, st // tm),
        in_specs=[pl.BlockSpec((1, tm, d), lambda b, i: (b, i, 0)),
                  _resident(w_p.shape),
                  pl.BlockSpec((1, LANES), fixed),
                  pl.BlockSpec((1, LANES), fixed),
                  pl.BlockSpec((tm, LANES), tab),
                  pl.BlockSpec((tm, LANES), tab),
                  pl.BlockSpec((tm, LANES), tab),
                  pl.BlockSpec((tm, LANES), tab)],
        out_specs=[pl.BlockSpec((1, tm, P_COLS), lambda b, i: (b, i, 0)),
                   pl.BlockSpec((1, A_KV_HEADS, A_HEAD_DIM, tm), lambda b, i: (b, 0, 0, i))],
        out_shape=[jax.ShapeDtypeStruct((bt, st, P_COLS), BF16),
                   jax.ShapeDtypeStruct((bt, A_KV_HEADS, A_HEAD_DIM, st), BF16)],
        compiler_params=_cparams("parallel", "parallel"),
        name="projection_rope" if use_rope else "projection_ctx",
    )(h, w_p, g_q, g_k, cos_a, sin_a, cos_b, sin_b)


def _flash(q_ref, kc_ref, vct_ref, kl_ref, vlt_ref, st_ref, acc_ref, tk):
    n_chunks = kl_ref.shape[1] // tk

    def latent_scores(c, slot):
        st_ref[slot] = _dot_nt(kl_ref[0, c * tk:(c + 1) * tk, :], q_ref[...])

    def update(c, slot, m, l):
        st = st_ref[slot]
        m_new = jnp.maximum(m, jnp.max(st, axis=0, keepdims=True))
        alpha = jnp.exp2(m - m_new)
        pt = jnp.exp2(st - m_new)
        acc_ref[...] = alpha * acc_ref[...] + _dot(vlt_ref[0, 0, :, c * tk:(c + 1) * tk],
                                                  pt.astype(BF16))
        return m_new, alpha * l + jnp.sum(pt, axis=0, keepdims=True)

    latent_scores(0, 0)
    st = _dot_nt(kc_ref[0], q_ref[...])
    m = jnp.max(st, axis=0, keepdims=True)
    pt = jnp.exp2(st - m)
    l = jnp.sum(pt, axis=0, keepdims=True)
    acc_ref[...] = _dot(vct_ref[0, 0], pt.astype(BF16))

    for c in range(n_chunks):
        if c + 1 < n_chunks:
            latent_scores(c + 1, (c + 1) & 1)
        m, l = update(c, c & 1, m, l)
    return (acc_ref[...] / l).T


def _flash_scratch(m_rows, dk, dv, tk):
    return [pltpu.VMEM((m_rows, dk), BF16),
            pltpu.VMEM((2, tk, m_rows), F32),
            pltpu.VMEM((dv, m_rows), F32)]


class _SideCasts:
    def __init__(self, arrays, grid):
        self.n_steps = math.prod(grid)
        strides = [math.prod(grid[k + 1:]) for k in range(len(grid))]
        step = lambda *g: sum(i * st for i, st in zip(g, strides))
        self.shapes = [a.shape for a in arrays]
        self.views, self.in_specs, self.out_specs, self.out_shapes = [], [], [], []
        for a in arrays:
            cols = a.shape[-1]
            rows = math.prod(a.shape[:-1]) // self.n_steps
            assert rows * self.n_steps == math.prod(a.shape[:-1]) and rows % 16 == 0
            self.views.append(a.reshape(self.n_steps, rows, cols))
            spec = pl.BlockSpec((1, rows, cols), lambda *g: (step(*g), 0, 0))
            self.in_specs.append(spec)
            self.out_specs.append(spec)
            self.out_shapes.append(jax.ShapeDtypeStruct((self.n_steps, rows, cols), BF16))

    @staticmethod
    def run(src_refs, dst_refs):
        for src, dst in zip(src_refs, dst_refs):
            dst[...] = src[...].astype(BF16)

    def restore(self, outs):
        return [o.reshape(shape) for o, shape in zip(outs, self.shapes)]


def _gqa_kernel(tk, n_side, q_ref, kl_ref, vlt_ref, kc_ref, vct_ref, *refs):
    side_src, (o_ref, *side_dst) = refs[:n_side], refs[n_side:2 * n_side + 1]
    qs_ref, st_ref, acc_ref = refs[2 * n_side + 1:]
    _SideCasts.run(side_src, side_dst)
    group = A_HEADS // A_KV_HEADS
    tq = q_ref.shape[1]
    for g in range(group):
        qs_ref[g * tq:(g + 1) * tq, :] = q_ref[0, :, g * A_HEAD_DIM:(g + 1) * A_HEAD_DIM]
    out = _flash(qs_ref, kc_ref, vct_ref, kl_ref, vlt_ref, st_ref, acc_ref, tk)
    for g in range(group):
        o_ref[0, :, g * A_HEAD_DIM:(g + 1) * A_HEAD_DIM] = out[g * tq:(g + 1) * tq].astype(BF16)


def _gqa_attention(p_lat, vat_lat, p_ctx, vat_ctx, tq, tk, cast_arrays):
    b, s, _ = p_lat.shape
    n_ctx = p_ctx.shape[1]
    group_w = (A_HEADS // A_KV_HEADS) * A_HEAD_DIM
    grid = (b, A_KV_HEADS, s // tq)
    side = _SideCasts(cast_arrays, grid)
    oa, *cast = pl.pallas_call(
        functools.partial(_gqa_kernel, tk, len(cast_arrays)),
        grid=grid,
        in_specs=[pl.BlockSpec((1, tq, group_w), lambda bb, k, i: (bb, i, P_QA // group_w + k)),
                  pl.BlockSpec((1, s, LANES), lambda bb, k, i: (bb, 0, P_KA // LANES + k)),
                  pl.BlockSpec((1, 1, A_HEAD_DIM, s), lambda bb, k, i: (bb, k, 0, 0)),
                  pl.BlockSpec((1, n_ctx, LANES), lambda bb, k, i: (bb, 0, P_KA // LANES + k)),
                  pl.BlockSpec((1, 1, A_HEAD_DIM, n_ctx), lambda bb, k, i: (bb, k, 0, 0)),
                  *side.in_specs],
        out_specs=[pl.BlockSpec((1, tq, group_w), lambda bb, k, i: (bb, i, k)), *side.out_specs],
        out_shape=[jax.ShapeDtypeStruct((b, s, A_HEADS * A_HEAD_DIM), BF16), *side.out_shapes],
        scratch_shapes=_flash_scratch((A_HEADS // A_KV_HEADS) * tq, A_HEAD_DIM, A_HEAD_DIM, tk),
        compiler_params=_cparams("parallel", "parallel", "parallel"),
        name="gqa_attention",
    )(p_lat, p_lat, vat_lat, p_ctx, vat_ctx, *side.views)
    return oa, side.restore(cast)


def _mla_kernel(tk, n_side, qn_ref, qr_ref, kl_ref, vlt_ref, kc_ref, vct_ref, *refs):
    side_src, (o_ref, *side_dst) = refs[:n_side], refs[n_side:2 * n_side + 1]
    qs_ref, st_ref, acc_ref = refs[2 * n_side + 1:]
    _SideCasts.run(side_src, side_dst)
    hd = pl.program_id(1)
    qr = qr_ref[0]
    lane = lax.broadcasted_iota(jnp.int32, qr.shape, 1)
    qs_ref[:, :B_NOPE] = qn_ref[0]
    qs_ref[:, B_NOPE:] = jnp.where((lane >> 6) == (hd & 1), qr, jnp.zeros_like(qr))
    o_ref[0] = _flash(qs_ref, kc_ref, vct_ref, kl_ref, vlt_ref, st_ref, acc_ref, tk).astype(BF16)


def _mla_attention(p_lat, kb_lat, vbt_lat, kb_ctx, vbt_ctx, tq, tk, cast_arrays):
    b, s, _ = p_lat.shape
    n_ctx = kb_ctx.shape[1]
    grid = (b, B_HEADS, s // tq)
    side = _SideCasts(cast_arrays, grid)
    ob, *cast = pl.pallas_call(
        functools.partial(_mla_kernel, tk, len(cast_arrays)),
        grid=grid,
        in_specs=[pl.BlockSpec((1, tq, LANES), lambda bb, h, i: (bb, i, P_QBN // LANES + h)),
                  pl.BlockSpec((1, tq, LANES), lambda bb, h, i: (bb, i, P_QBR // LANES + h // 2)),
                  pl.BlockSpec((1, s, 256), lambda bb, h, i: (bb, 0, h)),
                  pl.BlockSpec((1, 1, B_V, s), lambda bb, h, i: (bb, h, 0, 0)),
                  pl.BlockSpec((1, n_ctx, 256), lambda bb, h, i: (bb, 0, h)),
                  pl.BlockSpec((1, 1, B_V, n_ctx), lambda bb, h, i: (bb, h, 0, 0)),
                  *side.in_specs],
        out_specs=[pl.BlockSpec((1, tq, B_V), lambda bb, h, i: (bb, i, h)), *side.out_specs],
        out_shape=[jax.ShapeDtypeStruct((b, s, B_HEADS * B_V), BF16), *side.out_shapes],
        scratch_shapes=_flash_scratch(tq, 2 * LANES, B_V, tk),
        compiler_params=_cparams("parallel", "parallel", "parallel"),
        name="mla_attention",
    )(p_lat, p_lat, kb_lat, vbt_lat, kb_ctx, vbt_ctx, *side.views)
    return ob, side.restore(cast)


def _merge_kernel(oa_ref, ob_ref, ga_ref, gb_ref, x_ref, g1_ref, sh_ref, sc_ref, gn_ref,
                  wa_ref, wb_ref, wo_ref, wr_ref, br_ref, xn_ref, h2_ref, rt_ref, rtt_ref, cnt_ref):
    ya = _dot(oa_ref[0], wa_ref[...])
    yb = _dot(ob_ref[0], wb_ref[...])
    mix = (ga_ref[0].astype(F32) * ya + gb_ref[0].astype(F32) * yb).astype(BF16)
    xn = x_ref[0] + g1_ref[0] * _dot(mix, wo_ref[...])
    xn_ref[0] = xn
    h2 = _rms(xn, gn_ref[...]) * (1.0 + sc_ref[0]) + sh_ref[0]
    h2_ref[0] = h2
    logits = _dot(h2.astype(BF16), wr_ref[...]) + br_ref[...]
    lt = logits.T
    gl = [lt[g:g + 1, :] for g in range(N_GROUPS)]
    gmax = functools.reduce(jnp.maximum, gl)
    gsum = functools.reduce(lambda a, b_: a + b_, [jnp.exp(v - gmax) for v in gl])
    g_val = 1.0 / gsum
    g_idx = jnp.full(gmax.shape, N_GROUPS - 1, jnp.int32)
    for g in range(N_GROUPS - 2, -1, -1):
        g_idx = jnp.where(gl[g] == gmax, g, g_idx)
    el = []
    for e in range(EXPERTS_PER_GROUP):
        v = lt[N_GROUPS + e:N_GROUPS + e + 1, :]
        for g in range(1, N_GROUPS):
            row = N_GROUPS + g * EXPERTS_PER_GROUP + e
            v = jnp.where(g_idx == g, lt[row:row + 1, :], v)
        el.append(v)
    emax = functools.reduce(jnp.maximum, el)
    i1 = jnp.full(emax.shape, EXPERTS_PER_GROUP - 1, jnp.int32)
    for e in range(EXPERTS_PER_GROUP - 2, -1, -1):
        i1 = jnp.where(el[e] == emax, e, i1)
    neg = jnp.full(emax.shape, -jnp.inf, F32)
    el2 = [jnp.where(i1 == e, neg, el[e]) for e in range(EXPERTS_PER_GROUP)]
    emax2 = functools.reduce(jnp.maximum, el2)
    i2 = jnp.full(emax.shape, EXPERTS_PER_GROUP - 1, jnp.int32)
    for e in range(EXPERTS_PER_GROUP - 2, -1, -1):
        i2 = jnp.where(el2[e] == emax2, e, i2)
    p2 = jnp.exp(emax2 - emax)
    w1 = g_val / (1.0 + p2)
    w2 = g_val * p2 / (1.0 + p2)
    e1 = g_idx * EXPERTS_PER_GROUP + i1
    e2 = g_idx * EXPERTS_PER_GROUP + i2
    zero = jnp.zeros_like(w1)
    rt = jnp.concatenate([e1.astype(F32), e2.astype(F32), w1, w2, zero, zero, zero, zero], axis=0)
    rt_ref[...] = rt
    rtt_ref[...] = jnp.concatenate([rt, jnp.zeros((LANES - SUBLANES, rt.shape[1]), F32)], axis=0).T
    lane = lax.broadcasted_iota(jnp.int32, (1, LANES), 1)
    counts = jnp.zeros((1, LANES), F32)
    for e in range(N_EXPERTS):
        hits = jnp.where(e1 == e, 1.0, 0.0) + jnp.where(e2 == e, 1.0, 0.0)
        counts = counts + jnp.where(lane == e, jnp.sum(hits, axis=1, keepdims=True), 0.0)
    cnt_ref[0] = jnp.broadcast_to(counts, (SUBLANES, LANES))


def _merge_route(oa, ob, p_lat, x, g1, sh2, sc2, g_ffn, w_a, w_b, w_o, w_r, b_r, tm):
    b, s, d = x.shape
    n_i = s // tm
    row = lambda bb, i: (bb, i, 0)
    per_b = lambda bb, i: (bb, 0, 0)
    fixed = lambda bb, i: (0, 0)
    return pl.pallas_call(
        _merge_kernel,
        grid=(b, n_i),
        in_specs=[pl.BlockSpec((1, tm, oa.shape[2]), row),
                  pl.BlockSpec((1, tm, ob.shape[2]), row),
                  pl.BlockSpec((1, tm, d), lambda bb, i: (bb, i, P_GA // d)),
                  pl.BlockSpec((1, tm, d), lambda bb, i: (bb, i, P_GB // d)),
                  pl.BlockSpec((1, tm, d), row),
                  pl.BlockSpec((1, 1, d), per_b),
                  pl.BlockSpec((1, 1, d), per_b),
                  pl.BlockSpec((1, 1, d), per_b),
                  pl.BlockSpec((1, d), fixed),
                  _resident(w_a.shape),
                  _resident(w_b.shape),
                  _resident(w_o.shape),
                  _resident(w_r.shape),
                  pl.BlockSpec((1, LANES), fixed)],
        out_specs=[pl.BlockSpec((1, tm, d), row),
                   pl.BlockSpec((1, tm, d), row),
                   pl.BlockSpec((SUBLANES, tm), lambda bb, i: (0, bb * n_i + i)),
                   pl.BlockSpec((tm, LANES), lambda bb, i: (bb * n_i + i, 0)),
                   pl.BlockSpec((1, SUBLANES, LANES), lambda bb, i: (bb * n_i + i, 0, 0))],
        out_shape=[jax.ShapeDtypeStruct((b, s, d), F32),
                   jax.ShapeDtypeStruct((b, s, d), F32),
                   jax.ShapeDtypeStruct((SUBLANES, b * s), F32),
                   jax.ShapeDtypeStruct((b * s, LANES), F32),
                   jax.ShapeDtypeStruct((b * n_i, SUBLANES, LANES), F32)],
        compiler_params=_cparams("parallel", "parallel"),
        name="merge_route",
    )(oa, ob, p_lat, p_lat, x, g1, sh2, sc2, g_ffn, w_a, w_b, w_o, w_r, b_r)


def _moe_kernel(tm, na_ref, te_ref, src_ref, dst_ref, h_hbm, wg_ref, wu_ref, wd_ref,
                y_hbm, xbuf, ybuf, gsem, ssem):
    t = pl.program_id(0)
    n_active = na_ref[0]
    phase = lax.rem(t, MOE_SLOTS)

    def gather_start(tile, slot):
        for r in range(tm):
            pltpu.make_async_copy(h_hbm.at[pl.ds(src_ref[tile * tm + r], 1)],
                                  xbuf.at[slot, pl.ds(r, 1)], gsem.at[slot]).start()

    def gather_wait(slot):
        pltpu.make_async_copy(h_hbm.at[pl.ds(0, tm)], xbuf.at[slot], gsem.at[slot]).wait()

    def scatter_start(tile, slot):
        for r in range(tm):
            pltpu.make_async_copy(ybuf.at[slot, pl.ds(r, 1)],
                                  y_hbm.at[pl.ds(dst_ref[(tile + MOE_LEAD_TILES) * tm + r], 1)],
                                  ssem.at[slot]).start()

    def scatter_wait(slot):
        pltpu.make_async_copy(ybuf.at[slot], y_hbm.at[pl.ds(0, tm)], ssem.at[slot]).wait()

    @pl.when(t == 0)
    def _():
        for k in range(1, MOE_LEAD_TILES + 1):
            ybuf[(-k) % MOE_SLOTS] = jnp.zeros(ybuf.shape[1:], ybuf.dtype)
        for k in range(2, MOE_LEAD_TILES + 1):
            scatter_start(-k, (-k) % MOE_SLOTS)
        gather_start(0, 0)

    def step(cur):
        nxt = (cur + 1) % MOE_SLOTS
        gather_wait(cur)
        gate = _dot(xbuf[cur].astype(BF16), wg_ref[0])
        gather_start(t + 1, nxt)
        up = _dot(xbuf[cur].astype(BF16), wu_ref[0])
        scatter_start(t - 1, (cur - 1) % MOE_SLOTS)
        hid = (gate * jax.nn.sigmoid(gate) * up).astype(BF16)
        ybuf[cur] = _dot(hid, wd_ref[0])
        scatter_wait((cur - MOE_LEAD_TILES) % MOE_SLOTS)

    def drain(cur):
        scatter_start(t, cur)
        for k in range(MOE_LEAD_TILES - 1, -1, -1):
            scatter_wait((cur - k) % MOE_SLOTS)
        gather_wait((cur + 1) % MOE_SLOTS)

    for cur in range(MOE_SLOTS):
        mine = phase == cur
        pl.when(mine & (t < n_active))(functools.partial(step, cur))
        pl.when(mine & (t == n_active - 1))(functools.partial(drain, cur))


def _moe(h2, n_active, tile_expert, src_tok, dst_row, w_gate, w_up, w_down, tm):
    n, d = h2.shape
    n_tiles = tile_expert.shape[0]
    w_in_map = lambda t, na, te, sr, ds: (te[t], 0, 0)
    grid_spec = pltpu.PrefetchScalarGridSpec(
        num_scalar_prefetch=4,
        grid=(n_tiles,),
        in_specs=[pl.BlockSpec(memory_space=pl.ANY),
                  pl.BlockSpec((1, d, D_EXPERT), w_in_map),
                  pl.BlockSpec((1, d, D_EXPERT), w_in_map),
                  pl.BlockSpec((1, D_EXPERT, d), w_in_map)],
        out_specs=pl.BlockSpec(memory_space=pl.ANY),
        scratch_shapes=[pltpu.VMEM((MOE_SLOTS, tm, d), F32),
                        pltpu.VMEM((MOE_SLOTS, tm, d), F32),
                        pltpu.SemaphoreType.DMA((MOE_SLOTS,)),
                        pltpu.SemaphoreType.DMA((MOE_SLOTS,))],
    )
    return pl.pallas_call(
        functools.partial(_moe_kernel, tm),
        grid_spec=grid_spec,
        out_shape=jax.ShapeDtypeStruct((2 * n + MOE_LEAD_TILES * tm, d), F32),
        compiler_params=_cparams("arbitrary"),
        name="moe_experts",
    )(n_active, tile_expert, src_tok, dst_row, h2, w_gate, w_up, w_down)


def _moe_plan(eid, counts, tm):
    n = eid.shape[1]
    pairs = 2 * n
    n_tiles = pairs // tm + N_EXPERTS
    _, order = lax.sort((eid.reshape(pairs), lax.iota(jnp.int32, pairs)), num_keys=1)
    padded = ((counts + tm - 1) // tm) * tm
    pad_end = jnp.cumsum(padded)
    pad_start = pad_end - padded
    raw_start = jnp.cumsum(counts) - counts
    tile_row0 = jnp.arange(n_tiles, dtype=jnp.int32) * tm
    tile_expert = jnp.minimum(
        jnp.sum(tile_row0[:, None] >= pad_end[None, :], axis=1), N_EXPERTS - 1).astype(jnp.int32)
    shift = (raw_start - pad_start)[tile_expert]
    limit = (pad_start + counts)[tile_expert]
    pos = jnp.arange(n_tiles * tm, dtype=jnp.int32).reshape(n_tiles, tm)
    valid = pos < limit[:, None]
    pair = jnp.where(valid, order[jnp.clip(pos + shift[:, None], 0, pairs - 1)], 0).reshape(-1)
    valid = valid.reshape(-1)
    dummy = pairs + jnp.arange(MOE_LEAD_TILES * tm, dtype=jnp.int32)
    pad_dst = (pairs + ((pos // tm) % MOE_LEAD_TILES) * tm + pos % tm).reshape(-1)
    src_tok = jnp.concatenate([pair % n, jnp.zeros((tm,), jnp.int32)]).astype(jnp.int32)
    dst_row = jnp.concatenate([dummy, jnp.where(valid, pair, pad_dst)]).astype(jnp.int32)
    n_active = (pad_end[-1:] // tm).astype(jnp.int32)
    return n_active, tile_expert, src_tok, dst_row


def _final_kernel(x_ref, y0_ref, y1_ref, rtt_ref, g2_ref, gn_ref, o_ref):
    wt = rtt_ref[...]
    moe = wt[:, 2:3] * y0_ref[...] + wt[:, 3:4] * y1_ref[...]
    o_ref[0] = _rms(x_ref[0] + g2_ref[0] * moe, gn_ref[...])


def _final(x_new, y2, route, g2, g_final, tm):
    b, s, d = x_new.shape
    row = lambda bb, i: (bb, i, 0)
    n_i = s // tm
    return pl.pallas_call(
        _final_kernel,
        grid=(b, n_i),
        in_specs=[pl.BlockSpec((1, tm, d), row),
                  pl.BlockSpec((tm, d), lambda bb, i: (bb * n_i + i, 0)),
                  pl.BlockSpec((tm, d), lambda bb, i: (b * n_i + bb * n_i + i, 0)),
                  pl.BlockSpec((tm, LANES), lambda bb, i: (bb * n_i + i, 0)),
                  pl.BlockSpec((1, 1, d), lambda bb, i: (bb, 0, 0)),
                  pl.BlockSpec((1, d), lambda bb, i: (0, 0))],
        out_specs=pl.BlockSpec((1, tm, d), row),
        out_shape=jax.ShapeDtypeStruct((b, s, d), F32),
        compiler_params=_cparams("parallel", "parallel"),
        name="final_norm",
    )(x_new, y2, y2, route, g2, g_final)


def _rope_tables(n_tokens, rot_dim):
    rows = n_tokens // GRID_W
    row = np.repeat(np.arange(rows), GRID_W).astype(np.float64)
    col = np.tile(np.arange(GRID_W), rows).astype(np.float64)
    n_freq = rot_dim // 4
    freqs = ROPE_THETA ** (-np.arange(n_freq, dtype=np.float64) / n_freq)
    ang = np.concatenate([row[:, None] * freqs, col[:, None] * freqs], axis=-1)
    cos, sin = np.cos(ang), np.sin(ang)
    reps = LANES // rot_dim
    cos_t = np.tile(np.concatenate([cos, cos], axis=-1), (1, reps))
    sin_t = np.tile(np.concatenate([-sin, sin], axis=-1), (1, reps))
    return jnp.asarray(cos_t, F32), jnp.asarray(sin_t, F32)


def kernel(x, c, ctx, c_ctx, w_mod, b_mod, norm_mix, norm_ffn, w_in, a_q_norm, a_k_norm, b_kv_norm, w_ukv, w_br_a, w_br_b, w_out, w_group, b_group, w_router, b_router, w_e_gate, w_e_up, w_e_down, norm_final):
    b, s, d = x.shape
    n_ctx = ctx.shape[1]
    assert w_mod.shape[0] == 1, "single-layer block"
    assert s % GRID_W == 0

    wt = jnp.transpose(w_in[0])
    qa_w = A_HEADS * A_HEAD_DIM
    kv_w = A_KV_HEADS * A_HEAD_DIM
    o_qb = qa_w + 2 * kv_w
    qb_w = B_HEADS * (B_NOPE + B_ROPE)
    o_ckv = o_qb + qb_w
    o_kr = o_ckv + B_KV_RANK
    o_gl = o_kr + B_ROPE
    w_qb = wt[o_qb:o_ckv].reshape(B_HEADS, B_NOPE + B_ROPE, d)
    w_p = jnp.concatenate(
        [wt[o_gl:], wt[:o_qb],
         w_qb[:, :B_NOPE].reshape(B_HEADS * B_NOPE, d),
         w_qb[:, B_NOPE:].reshape(B_HEADS * B_ROPE, d)], axis=0).astype(BF16)
    w_ck = jnp.concatenate([wt[o_ckv:o_kr], wt[o_kr:o_gl], wt[o_kr:o_gl]], axis=0).astype(BF16)
    w_kv = w_ukv[0].reshape(B_KV_RANK, B_HEADS, B_NOPE + B_V)
    w_k = w_kv[:, :, :B_NOPE].reshape(B_KV_RANK, B_HEADS * B_NOPE).astype(BF16)
    w_vt = w_kv[:, :, B_NOPE:].reshape(B_KV_RANK, B_HEADS * B_V).T.astype(BF16)
    w_r = jnp.concatenate(
        [w_group[0], jnp.transpose(w_router[0], (1, 0, 2)).reshape(d, N_EXPERTS),
         jnp.zeros((d, LANES - N_GROUPS - N_EXPERTS), F32)], axis=1).astype(BF16)
    b_r = jnp.concatenate([b_group[0], b_router[0].reshape(N_EXPERTS),
                           jnp.zeros((LANES - N_GROUPS - N_EXPERTS,), F32)])[None, :]

    cond = jnp.concatenate([c, c_ctx[None, :], jnp.zeros((SUBLANES - b - 1, d), F32)], axis=0)
    mod = _modulation(cond, w_mod[0], b_mod[0])
    mx = mod[:b].reshape(b, N_MOD, 1, d)
    sh1, sc1, g1, sh2, sc2, g2 = [mx[:, k] for k in range(N_MOD)]
    mc = jnp.broadcast_to(mod[b].reshape(N_MOD, 1, 1, d), (N_MOD, b, 1, d))
    csh1, csc1 = mc[0], mc[1]

    cos_a, sin_a = _rope_tables(s, A_HEAD_DIM)
    cos_b, sin_b = _rope_tables(s, B_ROPE)
    g_mix = norm_mix[0][None, :]
    g_kv = b_kv_norm[0][None, :]
    g_q = a_q_norm[0][None, :]
    g_k = a_k_norm[0][None, :]

    tm = min(TOKEN_TM, s)
    h, kb_lat, vbt_lat = _prologue(x, sh1, sc1, g_mix, w_ck, g_kv, w_k, w_vt, cos_b, sin_b, True, tm)
    p_lat, vat_lat = _projection(h, w_p, g_q, g_k, cos_a, sin_a, cos_b, sin_b, True, tm)

    tab_c = jnp.zeros((n_ctx, LANES), F32)
    hc, kb_ctx, vbt_ctx = _prologue(ctx, csh1, csc1, g_mix, w_ck, g_kv, w_k, w_vt, tab_c, tab_c,
                                    False, n_ctx)
    p_ctx, vat_ctx = _projection(hc, w_p, g_q, g_k, tab_c, tab_c, tab_c, tab_c, False, n_ctx)

    oa, (w_down,) = _gqa_attention(p_lat, vat_lat, p_ctx, vat_ctx, min(GQA_TQ, s), min(GQA_TK, s),
                                   [w_e_down[0]])
    ob, (w_gate, w_up) = _mla_attention(p_lat, kb_lat, vbt_lat, kb_ctx, vbt_ctx, min(MLA_TQ, s),
                                        min(MLA_TK, s), [w_e_gate[0], w_e_up[0]])

    x_new, h2, route, route_t, cnt = _merge_route(
        oa, ob, p_lat, x, g1, sh2, sc2, norm_ffn[0][None, :],
        w_br_a[0].astype(BF16), w_br_b[0].astype(BF16), w_out[0].astype(BF16), w_r, b_r,
        min(MERGE_TM, s))

    n = b * s
    counts = jnp.sum(cnt[:, 0, :N_EXPERTS], axis=0).astype(jnp.int32)
    plan = _moe_plan(route[0:2].astype(jnp.int32), counts, MOE_TM)
    y2 = _moe(h2.reshape(n, d), *plan, w_gate, w_up, w_down, MOE_TM)

    return _final(x_new, y2, route_t, g2, norm_final[None, :], tm)
```

```python
import functools
import math

import jax
import jax.numpy as jnp
import numpy as np
from jax import lax
from jax.experimental import pallas as pl
from jax.experimental.pallas import tpu as pltpu

GRID_W = 64
ROPE_THETA = 10000.0
EPS = 1e-6
A_HEADS = 8
A_KV_HEADS = 2
A_HEAD_DIM = 128
B_HEADS = 8
B_NOPE = 128
B_ROPE = 64
B_V = 128
B_KV_RANK = 512
N_GROUPS = 4
EXPERTS_PER_GROUP = 4
N_EXPERTS = N_GROUPS * EXPERTS_PER_GROUP
D_EXPERT = 1024
N_MOD = 6

LANES = 128
SUBLANES = 8
V7X_VMEM_LIMIT_BYTES = 56 * 1024 * 1024

BF16 = jnp.bfloat16
F32 = jnp.float32

P_GA = 0
P_GB = 2048
P_QA = 4096
P_KA = 5120
P_VA = 5376
P_QBN = 5632
P_QBR = 6656
P_COLS = 7168
PROJ_TN = 512

LOG2_E = math.log2(math.e)
A_SCORE_SCALE = LOG2_E / math.sqrt(A_HEAD_DIM)
B_SCORE_SCALE = LOG2_E / math.sqrt(B_NOPE + B_ROPE)

B_KEY_W = B_NOPE + 2 * B_ROPE
W_REGROUP_ROWS = 32
MOD_TN = 1024

TOKEN_TM = 512
MERGE_TM = 256
GQA_TQ = 512
MLA_TQ = 2048
GQA_TK = 512
MLA_TK = 512
MOE_TM = 256
MOE_LEAD_TILES = 1
MOE_SLOTS = MOE_LEAD_TILES + 1


def _cparams(*sem):
    return pltpu.CompilerParams(dimension_semantics=sem, vmem_limit_bytes=V7X_VMEM_LIMIT_BYTES)


def _dot(a, b):
    return jnp.dot(a, b, preferred_element_type=F32)


def _dot_nt(a, b):
    return lax.dot_general(a, b, (((1,), (1,)), ((), ())), preferred_element_type=F32)


def _resident(shape):
    return pl.BlockSpec(shape, lambda *_: (0,) * len(shape), pipeline_mode=pl.Buffered(1))


def _rms(v, gain):
    return v * lax.rsqrt(jnp.mean(v * v, axis=-1, keepdims=True) + EPS) * gain


def _mod_kernel(c_ref, w_ref, b_ref, o_ref):
    c = c_ref[...]
    s = (c * jax.nn.sigmoid(c)).astype(BF16)
    o_ref[...] = _dot(s, w_ref[...].astype(BF16)) + b_ref[...]


def _modulation(cond, w_mod, b_mod):
    rows, d = cond.shape
    n = w_mod.shape[1]
    tn = MOD_TN
    return pl.pallas_call(
        _mod_kernel,
        grid=(n // tn,),
        in_specs=[pl.BlockSpec((rows, d), lambda j: (0, 0)),
                  pl.BlockSpec((d, tn), lambda j: (0, j)),
                  pl.BlockSpec((1, tn), lambda j: (0, j))],
        out_specs=pl.BlockSpec((rows, tn), lambda j: (0, j)),
        out_shape=jax.ShapeDtypeStruct((rows, n), F32),
        compiler_params=_cparams("parallel"),
        name="modulation",
    )(cond, w_mod, b_mod.reshape(1, n))


def _swap_halves_64(v):
    lane = lax.broadcasted_iota(jnp.int32, v.shape, 1)
    return jnp.where((lane & 63) < 32, pltpu.roll(v, LANES - 32, 1), pltpu.roll(v, 32, 1))


def _pre_kernel(use_rope, n_side, tbl_ref, x_ref, sh_ref, sc_ref, g_ref, wck_ref, gkv_ref, wk_ref,
                wvt_ref, cb_ref, sb_ref, *refs):
    del tbl_ref
    side_src, (h_ref, kb_ref, vbt_ref), side_dst = refs[:n_side], refs[n_side:n_side + 3], refs[n_side + 3:]
    for k, src in enumerate(side_src):
        side_dst[0][k] = src[0].astype(BF16)
    xf = x_ref[0]
    h = _rms(xf, g_ref[...]) * (1.0 + sc_ref[0]) + sh_ref[0]
    hb = h.astype(BF16)
    h_ref[0] = hb
    p = _dot_nt(hb, wck_ref[...])
    cn = _rms(p[:, :B_KV_RANK], gkv_ref[...]).astype(BF16)
    kr2 = p[:, B_KV_RANK:]
    if use_rope:
        kr2 = kr2 * cb_ref[...] + _swap_halves_64(kr2) * sb_ref[...]
    kr2 = kr2.astype(BF16)
    kbn = _dot(cn, wk_ref[...]).astype(BF16)
    vbt_ref[0] = _dot_nt(wvt_ref[...], cn).astype(BF16).reshape(vbt_ref.shape[1:])
    for hd in range(B_HEADS):
        kb_ref[0, :, hd * B_KEY_W:hd * B_KEY_W + B_NOPE] = kbn[:, hd * B_NOPE:(hd + 1) * B_NOPE]
        kb_ref[0, :, hd * B_KEY_W + B_NOPE:(hd + 1) * B_KEY_W] = kr2


def _prologue(x, shift, scale, gain, w_ck, g_kv, w_k, w_vt, cos_b, sin_b, use_rope, tm,
              regroup=None):
    bt, st, d = x.shape
    n_i = st // tm
    row = lambda b, i, tbl: (b, i, 0)
    per_b = lambda b, i, tbl: (b, 0, 0)
    fixed = lambda b, i, tbl: (0, 0)
    tab = lambda b, i, tbl: (i, 0)
    side_in, side_out, side_shapes, side_args = [], [], [], []
    table = jnp.zeros((1,), jnp.int32)
    if regroup is not None:
        blocks, table = regroup
        per_step = table.shape[0] // (bt * n_i)
        assert per_step * bt * n_i == table.shape[0]
        for k in range(per_step):
            side_in.append(pl.BlockSpec(
                (1,) + blocks.shape[1:],
                lambda b, i, tbl, k=k: (tbl[(b * n_i + i) * per_step + k], 0, 0)))
            side_args.append(blocks)
        side_out.append(pl.BlockSpec((per_step,) + blocks.shape[1:],
                                     lambda b, i, tbl: (b * n_i + i, 0, 0)))
        side_shapes.append(jax.ShapeDtypeStruct((table.shape[0],) + blocks.shape[1:], BF16))
    grid_spec = pltpu.PrefetchScalarGridSpec(
        num_scalar_prefetch=1,
        grid=(bt, n_i),
        in_specs=[pl.BlockSpec((1, tm, d), row),
                  pl.BlockSpec((1, 1, d), per_b),
                  pl.BlockSpec((1, 1, d), per_b),
                  pl.BlockSpec((1, d), fixed),
                  _resident(w_ck.shape),
                  pl.BlockSpec((1, B_KV_RANK), fixed),
                  _resident(w_k.shape),
                  _resident(w_vt.shape),
                  pl.BlockSpec((tm, LANES), tab),
                  pl.BlockSpec((tm, LANES), tab),
                  *side_in],
        out_specs=[pl.BlockSpec((1, tm, d), row),
                   pl.BlockSpec((1, tm, B_HEADS * B_KEY_W), row),
                   pl.BlockSpec((1, B_HEADS, B_V, tm), lambda b, i, tbl: (b, 0, 0, i)),
                   *side_out],
    )
    h, kb, vbt, *side = pl.pallas_call(
        functools.partial(_pre_kernel, use_rope, len(side_in)),
        grid_spec=grid_spec,
        out_shape=[jax.ShapeDtypeStruct((bt, st, d), BF16),
                   jax.ShapeDtypeStruct((bt, st, B_HEADS * B_KEY_W), BF16),
                   jax.ShapeDtypeStruct((bt, B_HEADS, B_V, st), BF16),
                   *side_shapes],
        compiler_params=_cparams("parallel", "parallel"),
        name="prologue_rope" if use_rope else "prologue_ctx",
    )(table, x, shift, scale, gain, w_ck, g_kv, w_k, w_vt, cos_b, sin_b, *side_args)
    return h, kb, vbt, side


def _proj_kernel(use_rope, h_ref, w_ref, gq_ref, gk_ref, ca_ref, sa_ref, cb_ref, sb_ref,
                 o_ref, vat_ref):
    h = h_ref[0]
    n_blk = PROJ_TN // LANES

    def rope_a(v):
        if not use_rope:
            return v
        return v * ca_ref[...] + pltpu.roll(v, A_HEAD_DIM // 2, 1) * sa_ref[...]

    def rope_b(v):
        if not use_rope:
            return v
        return v * cb_ref[...] + _swap_halves_64(v) * sb_ref[...]

    for j in range(P_COLS // PROJ_TN):
        c0 = j * PROJ_TN
        acc = _dot_nt(h, w_ref[c0:c0 + PROJ_TN, :])

        def blk(k):
            return acc[:, k * LANES:(k + 1) * LANES]

        def put(k, v):
            o_ref[0, :, c0 + k * LANES:c0 + (k + 1) * LANES] = v.astype(BF16)

        if c0 < P_QA:
            o_ref[0, :, c0:c0 + PROJ_TN] = jax.nn.sigmoid(acc).astype(BF16)
        elif c0 < P_KA:
            for k in range(n_blk):
                put(k, rope_a(_rms(blk(k), gq_ref[...])) * A_SCORE_SCALE)
        elif c0 < P_QBN:
            for k in range(A_KV_HEADS):
                put(k, rope_a(_rms(blk(k), gk_ref[...])))
            for k in range(A_KV_HEADS, n_blk):
                put(k, blk(k))
                vat_ref[0, k - A_KV_HEADS] = blk(k).T.astype(BF16)
        elif c0 < P_QBR:
            o_ref[0, :, c0:c0 + PROJ_TN] = (acc * B_SCORE_SCALE).astype(BF16)
        else:
            for k in range(n_blk):
                put(k, rope_b(blk(k)) * B_SCORE_SCALE)


def _projection(h, w_p, g_q, g_k, cos_a, sin_a, cos_b, sin_b, use_rope, tm):
    bt, st, d = h.shape
    fixed = lambda b, i: (0, 0)
    tab = lambda b, i: (i, 0)
    return pl.pallas_call(
        functools.partial(_proj_kernel, use_rope),
        grid=(bt, st // tm),
        in_specs=[pl.BlockSpec((1, tm, d), lambda b, i: (b, i, 0)),
                  _resident(w_p.shape),
                  pl.BlockSpec((1, LANES), fixed),
                  pl.BlockSpec((1, LANES), fixed),
                  pl.BlockSpec((tm, LANES), tab),
                  pl.BlockSpec((tm, LANES), tab),
                  pl.BlockSpec((tm, LANES), tab),
                  pl.BlockSpec((tm, LANES), tab)],
        out_specs=[pl.BlockSpec((1, tm, P_COLS), lambda b, i: (b, i, 0)),
                   pl.BlockSpec((1, A_KV_HEADS, A_HEAD_DIM, tm), lambda b, i: (b, 0, 0, i))],
        out_shape=[jax.ShapeDtypeStruct((bt, st, P_COLS), BF16),
                   jax.ShapeDtypeStruct((bt, A_KV_HEADS, A_HEAD_DIM, st), BF16)],
        compiler_params=_cparams("parallel", "parallel"),
        name="projection_rope" if use_rope else "projection_ctx",
    )(h, w_p, g_q, g_k, cos_a, sin_a, cos_b, sin_b)


def _flash(q_ref, kc_ref, vct_ref, kl_ref, vlt_ref, st_ref, acc_ref, tk):
    n_chunks = kl_ref.shape[1] // tk

    def latent_scores(c, slot):
        st_ref[slot] = _dot_nt(kl_ref[0, c * tk:(c + 1) * tk, :], q_ref[...])

    def update(c, slot, m, l):
        st = st_ref[slot]
        m_new = jnp.maximum(m, jnp.max(st, axis=0, keepdims=True))
        alpha = jnp.exp2(m - m_new)
        pt = jnp.exp2(st - m_new)
        acc_ref[...] = alpha * acc_ref[...] + _dot(vlt_ref[0, 0, :, c * tk:(c + 1) * tk],
                                                  pt.astype(BF16))
        return m_new, alpha * l + jnp.sum(pt, axis=0, keepdims=True)

    latent_scores(0, 0)
    st = _dot_nt(kc_ref[0], q_ref[...])
    m = jnp.max(st, axis=0, keepdims=True)
    pt = jnp.exp2(st - m)
    l = jnp.sum(pt, axis=0, keepdims=True)
    acc_ref[...] = _dot(vct_ref[0, 0], pt.astype(BF16))

    for c in range(n_chunks):
        if c + 1 < n_chunks:
            latent_scores(c + 1, (c + 1) & 1)
        m, l = update(c, c & 1, m, l)
    return (acc_ref[...] / l).T


def _flash_scratch(m_rows, dk, dv, tk):
    return [pltpu.VMEM((m_rows, dk), BF16),
            pltpu.VMEM((2, tk, m_rows), F32),
            pltpu.VMEM((dv, m_rows), F32)]


class _SideCasts:
    def __init__(self, arrays, grid):
        self.n_steps = math.prod(grid)
        strides = [math.prod(grid[k + 1:]) for k in range(len(grid))]
        step = lambda *g: sum(i * st for i, st in zip(g, strides))
        self.shapes = [a.shape for a in arrays]
        self.views, self.in_specs, self.out_specs, self.out_shapes = [], [], [], []
        for a in arrays:
            cols = a.shape[-1]
            rows = math.prod(a.shape[:-1]) // self.n_steps
            assert rows * self.n_steps == math.prod(a.shape[:-1]) and rows % 16 == 0
            self.views.append(a.reshape(self.n_steps, rows, cols))
            spec = pl.BlockSpec((1, rows, cols), lambda *g: (step(*g), 0, 0))
            self.in_specs.append(spec)
            self.out_specs.append(spec)
            self.out_shapes.append(jax.ShapeDtypeStruct((self.n_steps, rows, cols), BF16))

    @staticmethod
    def run(src_refs, dst_refs):
        for src, dst in zip(src_refs, dst_refs):
            dst[...] = src[...].astype(BF16)

    def restore(self, outs):
        return [o.reshape(shape) for o, shape in zip(outs, self.shapes)]


def _gqa_kernel(tk, n_side, q_ref, kl_ref, vlt_ref, kc_ref, vct_ref, *refs):
    side_src, (o_ref, *side_dst) = refs[:n_side], refs[n_side:2 * n_side + 1]
    qs_ref, st_ref, acc_ref = refs[2 * n_side + 1:]
    _SideCasts.run(side_src, side_dst)
    group = A_HEADS // A_KV_HEADS
    tq = q_ref.shape[1]
    for g in range(group):
        qs_ref[g * tq:(g + 1) * tq, :] = q_ref[0, :, g * A_HEAD_DIM:(g + 1) * A_HEAD_DIM]
    out = _flash(qs_ref, kc_ref, vct_ref, kl_ref, vlt_ref, st_ref, acc_ref, tk)
    for g in range(group):
        o_ref[0, :, g * A_HEAD_DIM:(g + 1) * A_HEAD_DIM] = out[g * tq:(g + 1) * tq].astype(BF16)


def _gqa_attention(p_lat, vat_lat, p_ctx, vat_ctx, tq, tk, cast_arrays):
    b, s, _ = p_lat.shape
    n_ctx = p_ctx.shape[1]
    group_w = (A_HEADS // A_KV_HEADS) * A_HEAD_DIM
    grid = (b, A_KV_HEADS, s // tq)
    side = _SideCasts(cast_arrays, grid)
    oa, *cast = pl.pallas_call(
        functools.partial(_gqa_kernel, tk, len(cast_arrays)),
        grid=grid,
        in_specs=[pl.BlockSpec((1, tq, group_w), lambda bb, k, i: (bb, i, P_QA // group_w + k)),
                  pl.BlockSpec((1, s, LANES), lambda bb, k, i: (bb, 0, P_KA // LANES + k)),
                  pl.BlockSpec((1, 1, A_HEAD_DIM, s), lambda bb, k, i: (bb, k, 0, 0)),
                  pl.BlockSpec((1, n_ctx, LANES), lambda bb, k, i: (bb, 0, P_KA // LANES + k)),
                  pl.BlockSpec((1, 1, A_HEAD_DIM, n_ctx), lambda bb, k, i: (bb, k, 0, 0)),
                  *side.in_specs],
        out_specs=[pl.BlockSpec((1, tq, group_w), lambda bb, k, i: (bb, i, k)), *side.out_specs],
        out_shape=[jax.ShapeDtypeStruct((b, s, A_HEADS * A_HEAD_DIM), BF16), *side.out_shapes],
        scratch_shapes=_flash_scratch((A_HEADS // A_KV_HEADS) * tq, A_HEAD_DIM, A_HEAD_DIM, tk),
        compiler_params=_cparams("parallel", "parallel", "parallel"),
        name="gqa_attention",
    )(p_lat, p_lat, vat_lat, p_ctx, vat_ctx, *side.views)
    return oa, side.restore(cast)


def _mla_kernel(tk, n_side, qn_ref, qr_ref, kl_ref, vlt_ref, kc_ref, vct_ref, *refs):
    side_src, (o_ref, *side_dst) = refs[:n_side], refs[n_side:2 * n_side + 1]
    qs_ref, st_ref, acc_ref = refs[2 * n_side + 1:]
    _SideCasts.run(side_src, side_dst)
    hd = pl.program_id(1)
    qr = qr_ref[0]
    lane = lax.broadcasted_iota(jnp.int32, qr.shape, 1)
    qs_ref[:, :B_NOPE] = qn_ref[0]
    qs_ref[:, B_NOPE:] = jnp.where((lane >> 6) == (hd & 1), qr, jnp.zeros_like(qr))
    o_ref[0] = _flash(qs_ref, kc_ref, vct_ref, kl_ref, vlt_ref, st_ref, acc_ref, tk).astype(BF16)


def _mla_attention(p_lat, kb_lat, vbt_lat, kb_ctx, vbt_ctx, tq, tk, cast_arrays):
    b, s, _ = p_lat.shape
    n_ctx = kb_ctx.shape[1]
    grid = (b, B_HEADS, s // tq)
    side = _SideCasts(cast_arrays, grid)
    ob, *cast = pl.pallas_call(
        functools.partial(_mla_kernel, tk, len(cast_arrays)),
        grid=grid,
        in_specs=[pl.BlockSpec((1, tq, LANES), lambda bb, h, i: (bb, i, P_QBN // LANES + h)),
                  pl.BlockSpec((1, tq, LANES), lambda bb, h, i: (bb, i, P_QBR // LANES + h // 2)),
                  pl.BlockSpec((1, s, B_KEY_W), lambda bb, h, i: (bb, 0, h)),
                  pl.BlockSpec((1, 1, B_V, s), lambda bb, h, i: (bb, h, 0, 0)),
                  pl.BlockSpec((1, n_ctx, B_KEY_W), lambda bb, h, i: (bb, 0, h)),
                  pl.BlockSpec((1, 1, B_V, n_ctx), lambda bb, h, i: (bb, h, 0, 0)),
                  *side.in_specs],
        out_specs=[pl.BlockSpec((1, tq, B_V), lambda bb, h, i: (bb, i, h)), *side.out_specs],
        out_shape=[jax.ShapeDtypeStruct((b, s, B_HEADS * B_V), BF16), *side.out_shapes],
        scratch_shapes=_flash_scratch(tq, B_KEY_W, B_V, tk),
        compiler_params=_cparams("parallel", "parallel", "parallel"),
        name="mla_attention",
    )(p_lat, p_lat, kb_lat, vbt_lat, kb_ctx, vbt_ctx, *side.views)
    return ob, side.restore(cast)


def _merge_kernel(oa_ref, ob_ref, ga_ref, gb_ref, x_ref, g1_ref, sh_ref, sc_ref, gn_ref,
                  wa_ref, wb_ref, wo_ref, wr_ref, br_ref, xn_ref, h2_ref, rt_ref, rtt_ref, cnt_ref):
    ya = _dot(oa_ref[0], wa_ref[...])
    yb = _dot(ob_ref[0], wb_ref[...])
    mix = (ga_ref[0].astype(F32) * ya + gb_ref[0].astype(F32) * yb).astype(BF16)
    xn = x_ref[0] + g1_ref[0] * _dot(mix, wo_ref[...])
    xn_ref[0] = xn
    h2 = _rms(xn, gn_ref[...]) * (1.0 + sc_ref[0]) + sh_ref[0]
    h2_ref[0] = h2
    logits = _dot(h2.astype(BF16), wr_ref[...]) + br_ref[...]
    lt = logits.T
    gl = [lt[g:g + 1, :] for g in range(N_GROUPS)]
    gmax = functools.reduce(jnp.maximum, gl)
    gsum = functools.reduce(lambda a, b_: a + b_, [jnp.exp(v - gmax) for v in gl])
    g_val = 1.0 / gsum
    g_idx = jnp.full(gmax.shape, N_GROUPS - 1, jnp.int32)
    for g in range(N_GROUPS - 2, -1, -1):
        g_idx = jnp.where(gl[g] == gmax, g, g_idx)
    el = []
    for e in range(EXPERTS_PER_GROUP):
        v = lt[N_GROUPS + e:N_GROUPS + e + 1, :]
        for g in range(1, N_GROUPS):
            row = N_GROUPS + g * EXPERTS_PER_GROUP + e
            v = jnp.where(g_idx == g, lt[row:row + 1, :], v)
        el.append(v)
    emax = functools.reduce(jnp.maximum, el)
    i1 = jnp.full(emax.shape, EXPERTS_PER_GROUP - 1, jnp.int32)
    for e in range(EXPERTS_PER_GROUP - 2, -1, -1):
        i1 = jnp.where(el[e] == emax, e, i1)
    neg = jnp.full(emax.shape, -jnp.inf, F32)
    el2 = [jnp.where(i1 == e, neg, el[e]) for e in range(EXPERTS_PER_GROUP)]
    emax2 = functools.reduce(jnp.maximum, el2)
    i2 = jnp.full(emax.shape, EXPERTS_PER_GROUP - 1, jnp.int32)
    for e in range(EXPERTS_PER_GROUP - 2, -1, -1):
        i2 = jnp.where(el2[e] == emax2, e, i2)
    p2 = jnp.exp(emax2 - emax)
    w1 = g_val / (1.0 + p2)
    w2 = g_val * p2 / (1.0 + p2)
    e1 = g_idx * EXPERTS_PER_GROUP + i1
    e2 = g_idx * EXPERTS_PER_GROUP + i2
    zero = jnp.zeros_like(w1)
    rt = jnp.concatenate([e1.astype(F32), e2.astype(F32), w1, w2, zero, zero, zero, zero], axis=0)
    rt_ref[...] = rt
    rtt_ref[...] = jnp.concatenate([rt, jnp.zeros((LANES - SUBLANES, rt.shape[1]), F32)], axis=0).T
    lane = lax.broadcasted_iota(jnp.int32, (1, LANES), 1)
    counts = jnp.zeros((1, LANES), F32)
    for e in range(N_EXPERTS):
        hits = jnp.where(e1 == e, 1.0, 0.0) + jnp.where(e2 == e, 1.0, 0.0)
        counts = counts + jnp.where(lane == e, jnp.sum(hits, axis=1, keepdims=True), 0.0)
    cnt_ref[0] = jnp.broadcast_to(counts, (SUBLANES, LANES))


def _merge_route(oa, ob, p_lat, x, g1, sh2, sc2, g_ffn, w_a, w_b, w_o, w_r, b_r, tm):
    b, s, d = x.shape
    n_i = s // tm
    row = lambda bb, i: (bb, i, 0)
    per_b = lambda bb, i: (bb, 0, 0)
    fixed = lambda bb, i: (0, 0)
    return pl.pallas_call(
        _merge_kernel,
        grid=(b, n_i),
        in_specs=[pl.BlockSpec((1, tm, oa.shape[2]), row),
                  pl.BlockSpec((1, tm, ob.shape[2]), row),
                  pl.BlockSpec((1, tm, d), lambda bb, i: (bb, i, P_GA // d)),
                  pl.BlockSpec((1, tm, d), lambda bb, i: (bb, i, P_GB // d)),
                  pl.BlockSpec((1, tm, d), row),
                  pl.BlockSpec((1, 1, d), per_b),
                  pl.BlockSpec((1, 1, d), per_b),
                  pl.BlockSpec((1, 1, d), per_b),
                  pl.BlockSpec((1, d), fixed),
                  _resident(w_a.shape),
                  _resident(w_b.shape),
                  _resident(w_o.shape),
                  _resident(w_r.shape),
                  pl.BlockSpec((1, LANES), fixed)],
        out_specs=[pl.BlockSpec((1, tm, d), row),
                   pl.BlockSpec((1, tm, d), row),
                   pl.BlockSpec((SUBLANES, tm), lambda bb, i: (0, bb * n_i + i)),
                   pl.BlockSpec((tm, LANES), lambda bb, i: (bb * n_i + i, 0)),
                   pl.BlockSpec((1, SUBLANES, LANES), lambda bb, i: (bb * n_i + i, 0, 0))],
        out_shape=[jax.ShapeDtypeStruct((b, s, d), F32),
                   jax.ShapeDtypeStruct((b, s, d), F32),
                   jax.ShapeDtypeStruct((SUBLANES, b * s), F32),
                   jax.ShapeDtypeStruct((b * s, LANES), F32),
                   jax.ShapeDtypeStruct((b * n_i, SUBLANES, LANES), F32)],
        compiler_params=_cparams("parallel", "parallel"),
        name="merge_route",
    )(oa, ob, p_lat, p_lat, x, g1, sh2, sc2, g_ffn, w_a, w_b, w_o, w_r, b_r)


def _moe_kernel(tm, na_ref, te_ref, src_ref, dst_ref, h_hbm, wg_ref, wu_ref, wd_ref,
                y_hbm, xbuf, ybuf, gsem, ssem):
    t = pl.program_id(0)
    n_active = na_ref[0]
    phase = lax.rem(t, MOE_SLOTS)

    def gather_start(tile, slot):
        for r in range(tm):
            pltpu.make_async_copy(h_hbm.at[pl.ds(src_ref[tile * tm + r], 1)],
                                  xbuf.at[slot, pl.ds(r, 1)], gsem.at[slot]).start()

    def gather_wait(slot):
        pltpu.make_async_copy(h_hbm.at[pl.ds(0, tm)], xbuf.at[slot], gsem.at[slot]).wait()

    def scatter_start(tile, slot):
        for r in range(tm):
            pltpu.make_async_copy(ybuf.at[slot, pl.ds(r, 1)],
                                  y_hbm.at[pl.ds(dst_ref[(tile + MOE_LEAD_TILES) * tm + r], 1)],
                                  ssem.at[slot]).start()

    def scatter_wait(slot):
        pltpu.make_async_copy(ybuf.at[slot], y_hbm.at[pl.ds(0, tm)], ssem.at[slot]).wait()

    @pl.when(t == 0)
    def _():
        for k in range(1, MOE_LEAD_TILES + 1):
            ybuf[(-k) % MOE_SLOTS] = jnp.zeros(ybuf.shape[1:], ybuf.dtype)
        for k in range(2, MOE_LEAD_TILES + 1):
            scatter_start(-k, (-k) % MOE_SLOTS)
        gather_start(0, 0)

    def step(cur):
        nxt = (cur + 1) % MOE_SLOTS
        gather_wait(cur)
        gate = _dot(xbuf[cur].astype(BF16), wg_ref[0])
        gather_start(t + 1, nxt)
        up = _dot(xbuf[cur].astype(BF16), wu_ref[0])
        scatter_start(t - 1, (cur - 1) % MOE_SLOTS)
        hid = (gate * jax.nn.sigmoid(gate) * up).astype(BF16)
        ybuf[cur] = _dot(hid, wd_ref[0])
        scatter_wait((cur - MOE_LEAD_TILES) % MOE_SLOTS)

    def drain(cur):
        scatter_start(t, cur)
        for k in range(MOE_LEAD_TILES - 1, -1, -1):
            scatter_wait((cur - k) % MOE_SLOTS)
        gather_wait((cur + 1) % MOE_SLOTS)

    for cur in range(MOE_SLOTS):
        mine = phase == cur
        pl.when(mine & (t < n_active))(functools.partial(step, cur))
        pl.when(mine & (t == n_active - 1))(functools.partial(drain, cur))


def _moe(h2, n_active, tile_expert, src_tok, dst_row, w_gate, w_up, w_down, tm):
    n, d = h2.shape
    n_tiles = tile_expert.shape[0]
    w_in_map = lambda t, na, te, sr, ds: (te[t], 0, 0)
    grid_spec = pltpu.PrefetchScalarGridSpec(
        num_scalar_prefetch=4,
        grid=(n_tiles,),
        in_specs=[pl.BlockSpec(memory_space=pl.ANY),
                  pl.BlockSpec((1, d, D_EXPERT), w_in_map),
                  pl.BlockSpec((1, d, D_EXPERT), w_in_map),
                  pl.BlockSpec((1, D_EXPERT, d), w_in_map)],
        out_specs=pl.BlockSpec(memory_space=pl.ANY),
        scratch_shapes=[pltpu.VMEM((MOE_SLOTS, tm, d), F32),
                        pltpu.VMEM((MOE_SLOTS, tm, d), F32),
                        pltpu.SemaphoreType.DMA((MOE_SLOTS,)),
                        pltpu.SemaphoreType.DMA((MOE_SLOTS,))],
    )
    return pl.pallas_call(
        functools.partial(_moe_kernel, tm),
        grid_spec=grid_spec,
        out_shape=jax.ShapeDtypeStruct((2 * n + MOE_LEAD_TILES * tm, d), F32),
        compiler_params=_cparams("arbitrary"),
        name="moe_experts",
    )(n_active, tile_expert, src_tok, dst_row, h2, w_gate, w_up, w_down)


def _moe_plan(eid, counts, tm):
    n = eid.shape[1]
    pairs = 2 * n
    n_tiles = pairs // tm + N_EXPERTS
    _, order = lax.sort((eid.reshape(pairs), lax.iota(jnp.int32, pairs)), num_keys=1)
    padded = ((counts + tm - 1) // tm) * tm
    pad_end = jnp.cumsum(padded)
    pad_start = pad_end - padded
    raw_start = jnp.cumsum(counts) - counts
    tile_row0 = jnp.arange(n_tiles, dtype=jnp.int32) * tm
    tile_expert = jnp.minimum(
        jnp.sum(tile_row0[:, None] >= pad_end[None, :], axis=1), N_EXPERTS - 1).astype(jnp.int32)
    shift = (raw_start - pad_start)[tile_expert]
    limit = (pad_start + counts)[tile_expert]
    pos = jnp.arange(n_tiles * tm, dtype=jnp.int32).reshape(n_tiles, tm)
    valid = pos < limit[:, None]
    pair = jnp.where(valid, order[jnp.clip(pos + shift[:, None], 0, pairs - 1)], 0).reshape(-1)
    valid = valid.reshape(-1)
    dummy = pairs + jnp.arange(MOE_LEAD_TILES * tm, dtype=jnp.int32)
    pad_dst = (pairs + ((pos // tm) % MOE_LEAD_TILES) * tm + pos % tm).reshape(-1)
    src_tok = jnp.concatenate([pair % n, jnp.zeros((tm,), jnp.int32)]).astype(jnp.int32)
    dst_row = jnp.concatenate([dummy, jnp.where(valid, pair, pad_dst)]).astype(jnp.int32)
    n_active = (pad_end[-1:] // tm).astype(jnp.int32)
    return n_active, tile_expert, src_tok, dst_row


def _final_kernel(x_ref, y0_ref, y1_ref, rtt_ref, g2_ref, gn_ref, o_ref):
    wt = rtt_ref[...]
    moe = wt[:, 2:3] * y0_ref[...] + wt[:, 3:4] * y1_ref[...]
    o_ref[0] = _rms(x_ref[0] + g2_ref[0] * moe, gn_ref[...])


def _final(x_new, y2, route, g2, g_final, tm):
    b, s, d = x_new.shape
    row = lambda bb, i: (bb, i, 0)
    n_i = s // tm
    return pl.pallas_call(
        _final_kernel,
        grid=(b, n_i),
        in_specs=[pl.BlockSpec((1, tm, d), row),
                  pl.BlockSpec((tm, d), lambda bb, i: (bb * n_i + i, 0)),
                  pl.BlockSpec((tm, d), lambda bb, i: (b * n_i + bb * n_i + i, 0)),
                  pl.BlockSpec((tm, LANES), lambda bb, i: (bb * n_i + i, 0)),
                  pl.BlockSpec((1, 1, d), lambda bb, i: (bb, 0, 0)),
                  pl.BlockSpec((1, d), lambda bb, i: (0, 0))],
        out_specs=pl.BlockSpec((1, tm, d), row),
        out_shape=jax.ShapeDtypeStruct((b, s, d), F32),
        compiler_params=_cparams("parallel", "parallel"),
        name="final_norm",
    )(x_new, y2, y2, route, g2, g_final)


def _rope_tables(n_tokens, rot_dim):
    rows = n_tokens // GRID_W
    row = np.repeat(np.arange(rows), GRID_W).astype(np.float64)
    col = np.tile(np.arange(GRID_W), rows).astype(np.float64)
    n_freq = rot_dim // 4
    freqs = ROPE_THETA ** (-np.arange(n_freq, dtype=np.float64) / n_freq)
    ang = np.concatenate([row[:, None] * freqs, col[:, None] * freqs], axis=-1)
    cos, sin = np.cos(ang), np.sin(ang)
    reps = LANES // rot_dim
    cos_t = np.tile(np.concatenate([cos, cos], axis=-1), (1, reps))
    sin_t = np.tile(np.concatenate([-sin, sin], axis=-1), (1, reps))
    return jnp.asarray(cos_t, F32), jnp.asarray(sin_t, F32)


def kernel(x, c, ctx, c_ctx, w_mod, b_mod, norm_mix, norm_ffn, w_in, a_q_norm, a_k_norm, b_kv_norm, w_ukv, w_br_a, w_br_b, w_out, w_group, b_group, w_router, b_router, w_e_gate, w_e_up, w_e_down, norm_final):
    b, s, d = x.shape
    n_ctx = ctx.shape[1]
    assert w_mod.shape[0] == 1, "single-layer block"
    assert s % GRID_W == 0

    wt = jnp.transpose(w_in[0])
    qa_w = A_HEADS * A_HEAD_DIM
    kv_w = A_KV_HEADS * A_HEAD_DIM
    o_qb = qa_w + 2 * kv_w
    qb_w = B_HEADS * (B_NOPE + B_ROPE)
    o_ckv = o_qb + qb_w
    o_kr = o_ckv + B_KV_RANK
    o_gl = o_kr + B_ROPE
    qb_rows = o_qb + (B_NOPE + B_ROPE) * np.arange(B_HEADS)[:, None]
    p_rows = np.concatenate(
        [np.arange(o_gl, wt.shape[0]), np.arange(o_qb),
         (qb_rows + np.arange(B_NOPE)).reshape(-1),
         (qb_rows + B_NOPE + np.arange(B_ROPE)).reshape(-1)])
    p_blocks = p_rows.reshape(-1, W_REGROUP_ROWS)
    assert (np.diff(p_blocks, axis=1) == 1).all() and (p_blocks[:, 0] % W_REGROUP_ROWS == 0).all()
    regroup = (wt.reshape(-1, W_REGROUP_ROWS, d),
               jnp.asarray(p_blocks[:, 0] // W_REGROUP_ROWS, jnp.int32))
    w_ck = jnp.concatenate([wt[o_ckv:o_kr], wt[o_kr:o_gl], wt[o_kr:o_gl]], axis=0).astype(BF16)
    w_kv = w_ukv[0].reshape(B_KV_RANK, B_HEADS, B_NOPE + B_V)
    w_k = w_kv[:, :, :B_NOPE].reshape(B_KV_RANK, B_HEADS * B_NOPE).astype(BF16)
    w_vt = w_kv[:, :, B_NOPE:].reshape(B_KV_RANK, B_HEADS * B_V).T.astype(BF16)
    w_r = jnp.concatenate(
        [w_group[0], jnp.transpose(w_router[0], (1, 0, 2)).reshape(d, N_EXPERTS),
         jnp.zeros((d, LANES - N_GROUPS - N_EXPERTS), F32)], axis=1).astype(BF16)
    b_r = jnp.concatenate([b_group[0], b_router[0].reshape(N_EXPERTS),
                           jnp.zeros((LANES - N_GROUPS - N_EXPERTS,), F32)])[None, :]

    cond = jnp.concatenate([c, c_ctx[None, :], jnp.zeros((SUBLANES - b - 1, d), F32)], axis=0)
    mod = _modulation(cond, w_mod[0], b_mod[0])
    mx = mod[:b].reshape(b, N_MOD, 1, d)
    sh1, sc1, g1, sh2, sc2, g2 = [mx[:, k] for k in range(N_MOD)]
    mc = jnp.broadcast_to(mod[b].reshape(N_MOD, 1, 1, d), (N_MOD, b, 1, d))
    csh1, csc1 = mc[0], mc[1]

    cos_a, sin_a = _rope_tables(s, A_HEAD_DIM)
    cos_b, sin_b = _rope_tables(s, B_ROPE)
    g_mix = norm_mix[0][None, :]
    g_kv = b_kv_norm[0][None, :]
    g_q = a_q_norm[0][None, :]
    g_k = a_k_norm[0][None, :]

    tm = min(TOKEN_TM, s)
    h, kb_lat, vbt_lat, (w_p,) = _prologue(x, sh1, sc1, g_mix, w_ck, g_kv, w_k, w_vt, cos_b, sin_b,
                                           True, tm, regroup)
    w_p = w_p.reshape(P_COLS, d)
    p_lat, vat_lat = _projection(h, w_p, g_q, g_k, cos_a, sin_a, cos_b, sin_b, True, tm)

    tab_c = jnp.zeros((n_ctx, LANES), F32)
    hc, kb_ctx, vbt_ctx, _ = _prologue(ctx, csh1, csc1, g_mix, w_ck, g_kv, w_k, w_vt, tab_c, tab_c,
                                       False, n_ctx)
    p_ctx, vat_ctx = _projection(hc, w_p, g_q, g_k, tab_c, tab_c, tab_c, tab_c, False, n_ctx)

    oa, (w_down,) = _gqa_attention(p_lat, vat_lat, p_ctx, vat_ctx, min(GQA_TQ, s), min(GQA_TK, s),
                                   [w_e_down[0]])
    ob, (w_gate, w_up) = _mla_attention(p_lat, kb_lat, vbt_lat, kb_ctx, vbt_ctx, min(MLA_TQ, s),
                                        min(MLA_TK, s), [w_e_gate[0], w_e_up[0]])

    x_new, h2, route, route_t, cnt = _merge_route(
        oa, ob, p_lat, x, g1, sh2, sc2, norm_ffn[0][None, :],
        w_br_a[0].astype(BF16), w_br_b[0].astype(BF16), w_out[0].astype(BF16), w_r, b_r,
        min(MERGE_TM, s))

    n = b * s
    counts = jnp.sum(cnt[:, 0, :N_EXPERTS], axis=0).astype(jnp.int32)
    plan = _moe_plan(route[0:2].astype(jnp.int32), counts, MOE_TM)
    y2 = _moe(h2.reshape(n, d), *plan, w_gate, w_up, w_down, MOE_TM)

    return _final(x_new, y2, route_t, g2, norm_final[None, :], tm)
```

```python
import functools
import math

import jax
import jax.numpy as jnp
import numpy as np
from jax import lax
from jax.experimental import pallas as pl
from jax.experimental.pallas import tpu as pltpu

GRID_W = 64
ROPE_THETA = 10000.0
EPS = 1e-6
A_HEADS = 8
A_KV_HEADS = 2
A_HEAD_DIM = 128
B_HEADS = 8
B_NOPE = 128
B_ROPE = 64
B_V = 128
B_KV_RANK = 512
N_GROUPS = 4
EXPERTS_PER_GROUP = 4
N_EXPERTS = N_GROUPS * EXPERTS_PER_GROUP
D_EXPERT = 1024
N_MOD = 6

LANES = 128
SUBLANES = 8
V7X_VMEM_LIMIT_BYTES = 56 * 1024 * 1024

BF16 = jnp.bfloat16
F32 = jnp.float32

P_GA = 0
P_GB = 2048
P_QA = 4096
P_KA = 5120
P_VA = 5376
P_QBN = 5632
P_QBR = 6656
P_COLS = 7168
PROJ_TN = 512

LOG2_E = math.log2(math.e)
A_SCORE_SCALE = LOG2_E / math.sqrt(A_HEAD_DIM)
B_SCORE_SCALE = LOG2_E / math.sqrt(B_NOPE + B_ROPE)

B_KEY_W = B_NOPE + 2 * B_ROPE
W_REGROUP_ROWS = 32
MOD_TN = 1024

TOKEN_TM = 512
MERGE_TM = 256
GQA_TQ = 512
MLA_TQ = 2048
GQA_TK = 512
MLA_TK = 512
MOE_TM = 256
MOE_LEAD_TILES = 1
MOE_SLOTS = MOE_LEAD_TILES + 1


def _cparams(*sem):
    return pltpu.CompilerParams(dimension_semantics=sem, vmem_limit_bytes=V7X_VMEM_LIMIT_BYTES)


def _dot(a, b):
    return jnp.dot(a, b, preferred_element_type=F32)


def _dot_nt(a, b):
    return lax.dot_general(a, b, (((1,), (1,)), ((), ())), preferred_element_type=F32)


def _resident(shape):
    return pl.BlockSpec(shape, lambda *_: (0,) * len(shape), pipeline_mode=pl.Buffered(1))


def _rms(v, gain):
    return v * lax.rsqrt(jnp.mean(v * v, axis=-1, keepdims=True) + EPS) * gain


def _mod_kernel(c_ref, w_ref, b_ref, o_ref):
    c = c_ref[...]
    s = (c * jax.nn.sigmoid(c)).astype(BF16)
    o_ref[...] = _dot(s, w_ref[...].astype(BF16)) + b_ref[...]


def _modulation(cond, w_mod, b_mod):
    rows, d = cond.shape
    n = w_mod.shape[1]
    tn = MOD_TN
    return pl.pallas_call(
        _mod_kernel,
        grid=(n // tn,),
        in_specs=[pl.BlockSpec((rows, d), lambda j: (0, 0)),
                  pl.BlockSpec((d, tn), lambda j: (0, j)),
                  pl.BlockSpec((1, tn), lambda j: (0, j))],
        out_specs=pl.BlockSpec((rows, tn), lambda j: (0, j)),
        out_shape=jax.ShapeDtypeStruct((rows, n), F32),
        compiler_params=_cparams("parallel"),
        name="modulation",
    )(cond, w_mod, b_mod.reshape(1, n))


def _swap_halves_64(v):
    lane = lax.broadcasted_iota(jnp.int32, v.shape, 1)
    return jnp.where((lane & 63) < 32, pltpu.roll(v, LANES - 32, 1), pltpu.roll(v, 32, 1))


def _pre_kernel(use_rope, n_side, tbl_ref, x_ref, sh_ref, sc_ref, g_ref, wck_ref, gkv_ref, wk_ref,
                wvt_ref, cb_ref, sb_ref, *refs):
    del tbl_ref
    side_src, (h_ref, kb_ref, vbt_ref), side_dst = refs[:n_side], refs[n_side:n_side + 3], refs[n_side + 3:]
    for k, src in enumerate(side_src):
        side_dst[0][k] = src[0].astype(BF16)
    xf = x_ref[0]
    h = _rms(xf, g_ref[...]) * (1.0 + sc_ref[0]) + sh_ref[0]
    hb = h.astype(BF16)
    h_ref[0] = hb
    p = _dot_nt(hb, wck_ref[...])
    cn = _rms(p[:, :B_KV_RANK], gkv_ref[...]).astype(BF16)
    kr2 = p[:, B_KV_RANK:]
    if use_rope:
        kr2 = kr2 * cb_ref[...] + _swap_halves_64(kr2) * sb_ref[...]
    kr2 = kr2.astype(BF16)
    kbn = _dot(cn, wk_ref[...]).astype(BF16)
    vbt_ref[0] = _dot_nt(wvt_ref[...], cn).astype(BF16).reshape(vbt_ref.shape[1:])
    for hd in range(B_HEADS):
        kb_ref[0, :, hd * B_KEY_W:hd * B_KEY_W + B_NOPE] = kbn[:, hd * B_NOPE:(hd + 1) * B_NOPE]
        kb_ref[0, :, hd * B_KEY_W + B_NOPE:(hd + 1) * B_KEY_W] = kr2


def _prologue(x, shift, scale, gain, w_ck, g_kv, w_k, w_vt, cos_b, sin_b, use_rope, tm,
              regroup=None):
    bt, st, d = x.shape
    n_i = st // tm
    row = lambda b, i, tbl: (b, i, 0)
    per_b = lambda b, i, tbl: (b, 0, 0)
    fixed = lambda b, i, tbl: (0, 0)
    tab = lambda b, i, tbl: (i, 0)
    side_in, side_out, side_shapes, side_args = [], [], [], []
    table = jnp.zeros((1,), jnp.int32)
    if regroup is not None:
        blocks, table = regroup
        per_step = table.shape[0] // (bt * n_i)
        assert per_step * bt * n_i == table.shape[0]
        for k in range(per_step):
            side_in.append(pl.BlockSpec(
                (1,) + blocks.shape[1:],
                lambda b, i, tbl, k=k: (tbl[(b * n_i + i) * per_step + k], 0, 0)))
            side_args.append(blocks)
        side_out.append(pl.BlockSpec((per_step,) + blocks.shape[1:],
                                     lambda b, i, tbl: (b * n_i + i, 0, 0)))
        side_shapes.append(jax.ShapeDtypeStruct((table.shape[0],) + blocks.shape[1:], BF16))
    grid_spec = pltpu.PrefetchScalarGridSpec(
        num_scalar_prefetch=1,
        grid=(bt, n_i),
        in_specs=[pl.BlockSpec((1, tm, d), row),
                  pl.BlockSpec((1, 1, d), per_b),
                  pl.BlockSpec((1, 1, d), per_b),
                  pl.BlockSpec((1, d), fixed),
                  _resident(w_ck.shape),
                  pl.BlockSpec((1, B_KV_RANK), fixed),
                  _resident(w_k.shape),
                  _resident(w_vt.shape),
                  pl.BlockSpec((tm, LANES), tab),
                  pl.BlockSpec((tm, LANES), tab),
                  *side_in],
        out_specs=[pl.BlockSpec((1, tm, d), row),
                   pl.BlockSpec((1, tm, B_HEADS * B_KEY_W), row),
                   pl.BlockSpec((1, B_HEADS, B_V, tm), lambda b, i, tbl: (b, 0, 0, i)),
                   *side_out],
    )
    h, kb, vbt, *side = pl.pallas_call(
        functools.partial(_pre_kernel, use_rope, len(side_in)),
        grid_spec=grid_spec,
        out_shape=[jax.ShapeDtypeStruct((bt, st, d), BF16),
                   jax.ShapeDtypeStruct((bt, st, B_HEADS * B_KEY_W), BF16),
                   jax.ShapeDtypeStruct((bt, B_HEADS, B_V, st), BF16),
                   *side_shapes],
        compiler_params=_cparams("parallel", "parallel"),
        name="prologue_rope" if use_rope else "prologue_ctx",
    )(table, x, shift, scale, gain, w_ck, g_kv, w_k, w_vt, cos_b, sin_b, *side_args)
    return h, kb, vbt, side


def _proj_kernel(use_rope, h_ref, w_ref, gq_ref, gk_ref, ca_ref, sa_ref, cb_ref, sb_ref,
                 o_ref, vat_ref):
    h = h_ref[0]
    n_blk = PROJ_TN // LANES

    def rope_a(v):
        if not use_rope:
            return v
        return v * ca_ref[...] + pltpu.roll(v, A_HEAD_DIM // 2, 1) * sa_ref[...]

    def rope_b(v):
        if not use_rope:
            return v
        return v * cb_ref[...] + _swap_halves_64(v) * sb_ref[...]

    for j in range(P_COLS // PROJ_TN):
        c0 = j * PROJ_TN
        acc = _dot_nt(h, w_ref[c0:c0 + PROJ_TN, :])

        def blk(k):
            return acc[:, k * LANES:(k + 1) * LANES]

        def put(k, v):
            o_ref[0, :, c0 + k * LANES:c0 + (k + 1) * LANES] = v.astype(BF16)

        if c0 < P_QA:
            o_ref[0, :, c0:c0 + PROJ_TN] = jax.nn.sigmoid(acc).astype(BF16)
        elif c0 < P_KA:
            for k in range(n_blk):
                put(k, rope_a(_rms(blk(k), gq_ref[...])) * A_SCORE_SCALE)
        elif c0 < P_QBN:
            for k in range(A_KV_HEADS):
                put(k, rope_a(_rms(blk(k), gk_ref[...])))
            for k in range(A_KV_HEADS, n_blk):
                put(k, blk(k))
                vat_ref[0, k - A_KV_HEADS] = blk(k).T.astype(BF16)
        elif c0 < P_QBR:
            o_ref[0, :, c0:c0 + PROJ_TN] = (acc * B_SCORE_SCALE).astype(BF16)
        else:
            for k in range(n_blk):
                put(k, rope_b(blk(k)) * B_SCORE_SCALE)


def _projection(h, w_p, g_q, g_k, cos_a, sin_a, cos_b, sin_b, use_rope, tm):
    bt, st, d = h.shape
    fixed = lambda b, i: (0, 0)
    tab = lambda b, i: (i, 0)
    return pl.pallas_call(
        functools.partial(_proj_kernel, use_rope),
        grid=(bt, st // tm),
        in_specs=[pl.BlockSpec((1, tm, d), lambda b, i: (b, i, 0)),
                  _resident(w_p.shape),
                  pl.BlockSpec((1, LANES), fixed),
                  pl.BlockSpec((1, LANES), fixed),
                  pl.BlockSpec((tm, LANES), tab),
                  pl.BlockSpec((tm, LANES), tab),
                  pl.BlockSpec((tm, LANES), tab),
                  pl.BlockSpec((tm, LANES), tab)],
        out_specs=[pl.BlockSpec((1, tm, P_COLS), lambda b, i: (b, i, 0)),
                   pl.BlockSpec((1, A_KV_HEADS, A_HEAD_DIM, tm), lambda b, i: (b, 0, 0, i))],
        out_shape=[jax.ShapeDtypeStruct((bt, st, P_COLS), BF16),
                   jax.ShapeDtypeStruct((bt, A_KV_HEADS, A_HEAD_DIM, st), BF16)],
        compiler_params=_cparams("parallel", "parallel"),
        name="projection_rope" if use_rope else "projection_ctx",
    )(h, w_p, g_q, g_k, cos_a, sin_a, cos_b, sin_b)


def _flash(q_ref, kc_ref, vct_ref, kl_ref, vlt_ref, st_ref, acc_ref, tk):
    n_chunks = kl_ref.shape[1] // tk

    def latent_scores(c, slot):
        st_ref[slot] = _dot_nt(kl_ref[0, c * tk:(c + 1) * tk, :], q_ref[...])

    def update(c, slot, m, l):
        st = st_ref[slot]
        m_new = jnp.maximum(m, jnp.max(st, axis=0, keepdims=True))
        alpha = jnp.exp2(m - m_new)
        pt = jnp.exp2(st - m_new)
        acc_ref[...] = alpha * acc_ref[...] + _dot(vlt_ref[0, 0, :, c * tk:(c + 1) * tk],
                                                  pt.astype(BF16))
        return m_new, alpha * l + jnp.sum(pt, axis=0, keepdims=True)

    latent_scores(0, 0)
    st = _dot_nt(kc_ref[0], q_ref[...])
    m = jnp.max(st, axis=0, keepdims=True)
    pt = jnp.exp2(st - m)
    l = jnp.sum(pt, axis=0, keepdims=True)
    acc_ref[...] = _dot(vct_ref[0, 0], pt.astype(BF16))

    for c in range(n_chunks):
        if c + 1 < n_chunks:
            latent_scores(c + 1, (c + 1) & 1)
        m, l = update(c, c & 1, m, l)
    return (acc_ref[...] / l).T


def _flash_scratch(m_rows, dk, dv, tk):
    return [pltpu.VMEM((m_rows, dk), BF16),
            pltpu.VMEM((2, tk, m_rows), F32),
            pltpu.VMEM((dv, m_rows), F32)]


class _SideCasts:
    def __init__(self, arrays, grid):
        self.n_steps = math.prod(grid)
        strides = [math.prod(grid[k + 1:]) for k in range(len(grid))]
        step = lambda *g: sum(i * st for i, st in zip(g, strides))
        self.shapes = [a.shape for a in arrays]
        self.views, self.in_specs, self.out_specs, self.out_shapes = [], [], [], []
        for a in arrays:
            cols = a.shape[-1]
            rows = math.prod(a.shape[:-1]) // self.n_steps
            assert rows * self.n_steps == math.prod(a.shape[:-1]) and rows % 16 == 0
            self.views.append(a.reshape(self.n_steps, rows, cols))
            spec = pl.BlockSpec((1, rows, cols), lambda *g: (step(*g), 0, 0))
            self.in_specs.append(spec)
            self.out_specs.append(spec)
            self.out_shapes.append(jax.ShapeDtypeStruct((self.n_steps, rows, cols), BF16))

    @staticmethod
    def run(src_refs, dst_refs):
        for src, dst in zip(src_refs, dst_refs):
            dst[...] = src[...].astype(BF16)

    def restore(self, outs):
        return [o.reshape(shape) for o, shape in zip(outs, self.shapes)]


def _gqa_kernel(tk, n_side, q_ref, kl_ref, vlt_ref, kc_ref, vct_ref, *refs):
    side_src, (o_ref, *side_dst) = refs[:n_side], refs[n_side:2 * n_side + 1]
    qs_ref, st_ref, acc_ref = refs[2 * n_side + 1:]
    _SideCasts.run(side_src, side_dst)
    group = A_HEADS // A_KV_HEADS
    tq = q_ref.shape[1]
    for g in range(group):
        qs_ref[g * tq:(g + 1) * tq, :] = q_ref[0, :, g * A_HEAD_DIM:(g + 1) * A_HEAD_DIM]
    out = _flash(qs_ref, kc_ref, vct_ref, kl_ref, vlt_ref, st_ref, acc_ref, tk)
    for g in range(group):
        o_ref[0, :, g * A_HEAD_DIM:(g + 1) * A_HEAD_DIM] = out[g * tq:(g + 1) * tq].astype(BF16)


def _gqa_attention(p_lat, vat_lat, p_ctx, vat_ctx, tq, tk, cast_arrays):
    b, s, _ = p_lat.shape
    n_ctx = p_ctx.shape[1]
    group_w = (A_HEADS // A_KV_HEADS) * A_HEAD_DIM
    grid = (b, A_KV_HEADS, s // tq)
    side = _SideCasts(cast_arrays, grid)
    oa, *cast = pl.pallas_call(
        functools.partial(_gqa_kernel, tk, len(cast_arrays)),
        grid=grid,
        in_specs=[pl.BlockSpec((1, tq, group_w), lambda bb, k, i: (bb, i, P_QA // group_w + k)),
                  pl.BlockSpec((1, s, LANES), lambda bb, k, i: (bb, 0, P_KA // LANES + k)),
                  pl.BlockSpec((1, 1, A_HEAD_DIM, s), lambda bb, k, i: (bb, k, 0, 0)),
                  pl.BlockSpec((1, n_ctx, LANES), lambda bb, k, i: (bb, 0, P_KA // LANES + k)),
                  pl.BlockSpec((1, 1, A_HEAD_DIM, n_ctx), lambda bb, k, i: (bb, k, 0, 0)),
                  *side.in_specs],
        out_specs=[pl.BlockSpec((1, tq, group_w), lambda bb, k, i: (bb, i, k)), *side.out_specs],
        out_shape=[jax.ShapeDtypeStruct((b, s, A_HEADS * A_HEAD_DIM), BF16), *side.out_shapes],
        scratch_shapes=_flash_scratch((A_HEADS // A_KV_HEADS) * tq, A_HEAD_DIM, A_HEAD_DIM, tk),
        compiler_params=_cparams("parallel", "parallel", "parallel"),
        name="gqa_attention",
    )(p_lat, p_lat, vat_lat, p_ctx, vat_ctx, *side.views)
    return oa, side.restore(cast)


def _mla_kernel(tk, n_side, qn_ref, qr_ref, kl_ref, vlt_ref, kc_ref, vct_ref, *refs):
    side_src, (o_ref, *side_dst) = refs[:n_side], refs[n_side:2 * n_side + 1]
    qs_ref, st_ref, acc_ref = refs[2 * n_side + 1:]
    _SideCasts.run(side_src, side_dst)
    hd = pl.program_id(1)
    qr = qr_ref[0]
    lane = lax.broadcasted_iota(jnp.int32, qr.shape, 1)
    qs_ref[:, :B_NOPE] = qn_ref[0]
    qs_ref[:, B_NOPE:] = jnp.where((lane >> 6) == (hd & 1), qr, jnp.zeros_like(qr))
    o_ref[0] = _flash(qs_ref, kc_ref, vct_ref, kl_ref, vlt_ref, st_ref, acc_ref, tk).astype(BF16)


def _mla_attention(p_lat, kb_lat, vbt_lat, kb_ctx, vbt_ctx, tq, tk, cast_arrays):
    b, s, _ = p_lat.shape
    n_ctx = kb_ctx.shape[1]
    grid = (b, B_HEADS, s // tq)
    side = _SideCasts(cast_arrays, grid)
    ob, *cast = pl.pallas_call(
        functools.partial(_mla_kernel, tk, len(cast_arrays)),
        grid=grid,
        in_specs=[pl.BlockSpec((1, tq, LANES), lambda bb, h, i: (bb, i, P_QBN // LANES + h)),
                  pl.BlockSpec((1, tq, LANES), lambda bb, h, i: (bb, i, P_QBR // LANES + h // 2)),
                  pl.BlockSpec((1, s, B_KEY_W), lambda bb, h, i: (bb, 0, h)),
                  pl.BlockSpec((1, 1, B_V, s), lambda bb, h, i: (bb, h, 0, 0)),
                  pl.BlockSpec((1, n_ctx, B_KEY_W), lambda bb, h, i: (bb, 0, h)),
                  pl.BlockSpec((1, 1, B_V, n_ctx), lambda bb, h, i: (bb, h, 0, 0)),
                  *side.in_specs],
        out_specs=[pl.BlockSpec((1, tq, B_V), lambda bb, h, i: (bb, i, h)), *side.out_specs],
        out_shape=[jax.ShapeDtypeStruct((b, s, B_HEADS * B_V), BF16), *side.out_shapes],
        scratch_shapes=_flash_scratch(tq, B_KEY_W, B_V, tk),
        compiler_params=_cparams("parallel", "parallel", "parallel"),
        name="mla_attention",
    )(p_lat, p_lat, kb_lat, vbt_lat, kb_ctx, vbt_ctx, *side.views)
    return ob, side.restore(cast)


def _merge_kernel(oa_ref, ob_ref, ga_ref, gb_ref, x_ref, g1_ref, sh_ref, sc_ref, gn_ref,
                  wa_ref, wb_ref, wo_ref, wr_ref, br_ref, xn_ref, h2_ref, rt_ref, rtt_ref, cnt_ref):
    ya = _dot(oa_ref[0], wa_ref[...])
    yb = _dot(ob_ref[0], wb_ref[...])
    mix = (ga_ref[0].astype(F32) * ya + gb_ref[0].astype(F32) * yb).astype(BF16)
    xn = x_ref[0] + g1_ref[0] * _dot(mix, wo_ref[...])
    xn_ref[0] = xn
    h2 = _rms(xn, gn_ref[...]) * (1.0 + sc_ref[0]) + sh_ref[0]
    h2_ref[0] = h2
    logits = _dot(h2.astype(BF16), wr_ref[...]) + br_ref[...]
    lt = logits.T
    gl = [lt[g:g + 1, :] for g in range(N_GROUPS)]
    gmax = functools.reduce(jnp.maximum, gl)
    gsum = functools.reduce(lambda a, b_: a + b_, [jnp.exp(v - gmax) for v in gl])
    g_val = 1.0 / gsum
    g_idx = jnp.full(gmax.shape, N_GROUPS - 1, jnp.int32)
    for g in range(N_GROUPS - 2, -1, -1):
        g_idx = jnp.where(gl[g] == gmax, g, g_idx)
    el = []
    for e in range(EXPERTS_PER_GROUP):
        v = lt[N_GROUPS + e:N_GROUPS + e + 1, :]
        for g in range(1, N_GROUPS):
            row = N_GROUPS + g * EXPERTS_PER_GROUP + e
            v = jnp.where(g_idx == g, lt[row:row + 1, :], v)
        el.append(v)
    emax = functools.reduce(jnp.maximum, el)
    i1 = jnp.full(emax.shape, EXPERTS_PER_GROUP - 1, jnp.int32)
    for e in range(EXPERTS_PER_GROUP - 2, -1, -1):
        i1 = jnp.where(el[e] == emax, e, i1)
    neg = jnp.full(emax.shape, -jnp.inf, F32)
    el2 = [jnp.where(i1 == e, neg, el[e]) for e in range(EXPERTS_PER_GROUP)]
    emax2 = functools.reduce(jnp.maximum, el2)
    i2 = jnp.full(emax.shape, EXPERTS_PER_GROUP - 1, jnp.int32)
    for e in range(EXPERTS_PER_GROUP - 2, -1, -1):
        i2 = jnp.where(el2[e] == emax2, e, i2)
    p2 = jnp.exp(emax2 - emax)
    w1 = g_val / (1.0 + p2)
    w2 = g_val * p2 / (1.0 + p2)
    e1 = g_idx * EXPERTS_PER_GROUP + i1
    e2 = g_idx * EXPERTS_PER_GROUP + i2
    zero = jnp.zeros_like(w1)
    rt = jnp.concatenate([e1.astype(F32), e2.astype(F32), w1, w2, zero, zero, zero, zero], axis=0)
    rt_ref[...] = rt
    rtt_ref[...] = jnp.concatenate([rt, jnp.zeros((LANES - SUBLANES, rt.shape[1]), F32)], axis=0).T
    lane = lax.broadcasted_iota(jnp.int32, (1, LANES), 1)
    counts = jnp.zeros((1, LANES), F32)
    for e in range(N_EXPERTS):
        hits = jnp.where(e1 == e, 1.0, 0.0) + jnp.where(e2 == e, 1.0, 0.0)
        counts = counts + jnp.where(lane == e, jnp.sum(hits, axis=1, keepdims=True), 0.0)
    cnt_ref[0] = jnp.broadcast_to(counts, (SUBLANES, LANES))


def _merge_route(oa, ob, p_lat, x, g1, sh2, sc2, g_ffn, w_a, w_b, w_o, w_r, b_r, tm):
    b, s, d = x.shape
    n_i = s // tm
    row = lambda bb, i: (bb, i, 0)
    per_b = lambda bb, i: (bb, 0, 0)
    fixed = lambda bb, i: (0, 0)
    return pl.pallas_call(
        _merge_kernel,
        grid=(b, n_i),
        in_specs=[pl.BlockSpec((1, tm, oa.shape[2]), row),
                  pl.BlockSpec((1, tm, ob.shape[2]), row),
                  pl.BlockSpec((1, tm, d), lambda bb, i: (bb, i, P_GA // d)),
                  pl.BlockSpec((1, tm, d), lambda bb, i: (bb, i, P_GB // d)),
                  pl.BlockSpec((1, tm, d), row),
                  pl.BlockSpec((1, 1, d), per_b),
                  pl.BlockSpec((1, 1, d), per_b),
                  pl.BlockSpec((1, 1, d), per_b),
                  pl.BlockSpec((1, d), fixed),
                  _resident(w_a.shape),
                  _resident(w_b.shape),
                  _resident(w_o.shape),
                  _resident(w_r.shape),
                  pl.BlockSpec((1, LANES), fixed)],
        out_specs=[pl.BlockSpec((1, tm, d), row),
                   pl.BlockSpec((1, tm, d), row),
                   pl.BlockSpec((SUBLANES, tm), lambda bb, i: (0, bb * n_i + i)),
                   pl.BlockSpec((tm, LANES), lambda bb, i: (bb * n_i + i, 0)),
                   pl.BlockSpec((1, SUBLANES, LANES), lambda bb, i: (bb * n_i + i, 0, 0))],
        out_shape=[jax.ShapeDtypeStruct((b, s, d), F32),
                   jax.ShapeDtypeStruct((b, s, d), F32),
                   jax.ShapeDtypeStruct((SUBLANES, b * s), F32),
                   jax.ShapeDtypeStruct((b * s, LANES), F32),
                   jax.ShapeDtypeStruct((b * n_i, SUBLANES, LANES), F32)],
        compiler_params=_cparams("parallel", "parallel"),
        name="merge_route",
    )(oa, ob, p_lat, p_lat, x, g1, sh2, sc2, g_ffn, w_a, w_b, w_o, w_r, b_r)


def _moe_kernel(tm, na_ref, te_ref, src_ref, dst_ref, h_hbm, wg_ref, wu_ref, wd_ref,
                y_hbm, xbuf, ybuf, gsem, ssem):
    t = pl.program_id(0)
    n_active = na_ref[0]
    phase = lax.rem(t, MOE_SLOTS)

    def gather_start(tile, slot):
        for r in range(tm):
            pltpu.make_async_copy(h_hbm.at[pl.ds(src_ref[tile * tm + r], 1)],
                                  xbuf.at[slot, pl.ds(r, 1)], gsem.at[slot]).start()

    def gather_wait(slot):
        pltpu.make_async_copy(h_hbm.at[pl.ds(0, tm)], xbuf.at[slot], gsem.at[slot]).wait()

    def scatter_start(tile, slot):
        for r in range(tm):
            pltpu.make_async_copy(ybuf.at[slot, pl.ds(r, 1)],
                                  y_hbm.at[pl.ds(dst_ref[(tile + MOE_LEAD_TILES) * tm + r], 1)],
                                  ssem.at[slot]).start()

    def scatter_wait(slot):
        pltpu.make_async_copy(ybuf.at[slot], y_hbm.at[pl.ds(0, tm)], ssem.at[slot]).wait()

    @pl.when(t == 0)
    def _():
        for k in range(1, MOE_LEAD_TILES + 1):
            ybuf[(-k) % MOE_SLOTS] = jnp.zeros(ybuf.shape[1:], ybuf.dtype)
        for k in range(2, MOE_LEAD_TILES + 1):
            scatter_start(-k, (-k) % MOE_SLOTS)
        gather_start(0, 0)

    def step(cur):
        nxt = (cur + 1) % MOE_SLOTS
        gather_wait(cur)
        gate = _dot(xbuf[cur].astype(BF16), wg_ref[0])
        gather_start(t + 1, nxt)
        up = _dot(xbuf[cur].astype(BF16), wu_ref[0])
        scatter_start(t - 1, (cur - 1) % MOE_SLOTS)
        hid = (gate * jax.nn.sigmoid(gate) * up).astype(BF16)
        ybuf[cur] = _dot(hid, wd_ref[0])
        scatter_wait((cur - MOE_LEAD_TILES) % MOE_SLOTS)

    def drain(cur):
        scatter_start(t, cur)
        for k in range(MOE_LEAD_TILES - 1, -1, -1):
            scatter_wait((cur - k) % MOE_SLOTS)
        gather_wait((cur + 1) % MOE_SLOTS)

    for cur in range(MOE_SLOTS):
        mine = phase == cur
        pl.when(mine & (t < n_active))(functools.partial(step, cur))
        pl.when(mine & (t == n_active - 1))(functools.partial(drain, cur))


def _moe(h2, n_active, tile_expert, src_tok, dst_row, w_gate, w_up, w_down, tm):
    n, d = h2.shape
    n_tiles = tile_expert.shape[0]
    w_in_map = lambda t, na, te, sr, ds: (te[t], 0, 0)
    grid_spec = pltpu.PrefetchScalarGridSpec(
        num_scalar_prefetch=4,
        grid=(n_tiles,),
        in_specs=[pl.BlockSpec(memory_space=pl.ANY),
                  pl.BlockSpec((1, d, D_EXPERT), w_in_map),
                  pl.BlockSpec((1, d, D_EXPERT), w_in_map),
                  pl.BlockSpec((1, D_EXPERT, d), w_in_map)],
        out_specs=pl.BlockSpec(memory_space=pl.ANY),
        scratch_shapes=[pltpu.VMEM((MOE_SLOTS, tm, d), F32),
                        pltpu.VMEM((MOE_SLOTS, tm, d), F32),
                        pltpu.SemaphoreType.DMA((MOE_SLOTS,)),
                        pltpu.SemaphoreType.DMA((MOE_SLOTS,))],
    )
    return pl.pallas_call(
        functools.partial(_moe_kernel, tm),
        grid_spec=grid_spec,
        out_shape=jax.ShapeDtypeStruct((2 * n + MOE_LEAD_TILES * tm, d), F32),
        compiler_params=_cparams("arbitrary"),
        name="moe_experts",
    )(n_active, tile_expert, src_tok, dst_row, h2, w_gate, w_up, w_down)


def _moe_plan(eid, counts, tm):
    n = eid.shape[1]
    pairs = 2 * n
    n_tiles = pairs // tm + N_EXPERTS
    _, order = lax.sort((eid.reshape(pairs), lax.iota(jnp.int32, pairs)), num_keys=1)
    padded = ((counts + tm - 1) // tm) * tm
    pad_end = jnp.cumsum(padded)
    pad_start = pad_end - padded
    raw_start = jnp.cumsum(counts) - counts
    tile_row0 = jnp.arange(n_tiles, dtype=jnp.int32) * tm
    tile_expert = jnp.minimum(
        jnp.sum(tile_row0[:, None] >= pad_end[None, :], axis=1), N_EXPERTS - 1).astype(jnp.int32)
    shift = (raw_start - pad_start)[tile_expert]
    limit = (pad_start + counts)[tile_expert]
    pos = jnp.arange(n_tiles * tm, dtype=jnp.int32).reshape(n_tiles, tm)
    valid = pos < limit[:, None]
    pair = jnp.where(valid, order[jnp.clip(pos + shift[:, None], 0, pairs - 1)], 0).reshape(-1)
    valid = valid.reshape(-1)
    dummy = pairs + jnp.arange(MOE_LEAD_TILES * tm, dtype=jnp.int32)
    pad_dst = (pairs + ((pos // tm) % MOE_LEAD_TILES) * tm + pos % tm).reshape(-1)
    src_tok = jnp.concatenate([pair % n, jnp.zeros((tm,), jnp.int32)]).astype(jnp.int32)
    dst_row = jnp.concatenate([dummy, jnp.where(valid, pair, pad_dst)]).astype(jnp.int32)
    n_active = (pad_end[-1:] // tm).astype(jnp.int32)
    return n_active, tile_expert, src_tok, dst_row


def _final_kernel(x_ref, y0_ref, y1_ref, rtt_ref, g2_ref, gn_ref, o_ref):
    wt = rtt_ref[...]
    moe = wt[:, 2:3] * y0_ref[...] + wt[:, 3:4] * y1_ref[...]
    o_ref[0] = _rms(x_ref[0] + g2_ref[0] * moe, gn_ref[...])


def _final(x_new, y2, route, g2, g_final, tm):
    b, s, d = x_new.shape
    row = lambda bb, i: (bb, i, 0)
    n_i = s // tm
    return pl.pallas_call(
        _final_kernel,
        grid=(b, n_i),
        in_specs=[pl.BlockSpec((1, tm, d), row),
                  pl.BlockSpec((tm, d), lambda bb, i: (bb * n_i + i, 0)),
                  pl.BlockSpec((tm, d), lambda bb, i: (b * n_i + bb * n_i + i, 0)),
                  pl.BlockSpec((tm, LANES), lambda bb, i: (bb * n_i + i, 0)),
                  pl.BlockSpec((1, 1, d), lambda bb, i: (bb, 0, 0)),
                  pl.BlockSpec((1, d), lambda bb, i: (0, 0))],
        out_specs=pl.BlockSpec((1, tm, d), row),
        out_shape=jax.ShapeDtypeStruct((b, s, d), F32),
        compiler_params=_cparams("parallel", "parallel"),
        name="final_norm",
    )(x_new, y2, y2, route, g2, g_final)


def _rope_tables(n_tokens, rot_dim):
    rows = n_tokens // GRID_W
    row = np.repeat(np.arange(rows), GRID_W).astype(np.float64)
    col = np.tile(np.arange(GRID_W), rows).astype(np.float64)
    n_freq = rot_dim // 4
    freqs = ROPE_THETA ** (-np.arange(n_freq, dtype=np.float64) / n_freq)
    ang = np.concatenate([row[:, None] * freqs, col[:, None] * freqs], axis=-1)
    cos, sin = np.cos(ang), np.sin(ang)
    reps = LANES // rot_dim
    cos_t = np.tile(np.concatenate([cos, cos], axis=-1), (1, reps))
    sin_t = np.tile(np.concatenate([-sin, sin], axis=-1), (1, reps))
    return jnp.asarray(cos_t, F32), jnp.asarray(sin_t, F32)


def kernel(x, c, ctx, c_ctx, w_mod, b_mod, norm_mix, norm_ffn, w_in, a_q_norm, a_k_norm, b_kv_norm, w_ukv, w_br_a, w_br_b, w_out, w_group, b_group, w_router, b_router, w_e_gate, w_e_up, w_e_down, norm_final):
    b, s, d = x.shape
    n_ctx = ctx.shape[1]
    assert w_mod.shape[0] == 1, "single-layer block"
    assert s % GRID_W == 0

    wt = jnp.transpose(w_in[0])
    qa_w = A_HEADS * A_HEAD_DIM
    kv_w = A_KV_HEADS * A_HEAD_DIM
    o_qb = qa_w + 2 * kv_w
    qb_w = B_HEADS * (B_NOPE + B_ROPE)
    o_ckv = o_qb + qb_w
    o_kr = o_ckv + B_KV_RANK
    o_gl = o_kr + B_ROPE
    qb_rows = o_qb + (B_NOPE + B_ROPE) * np.arange(B_HEADS)[:, None]
    p_rows = np.concatenate(
        [np.arange(o_gl, wt.shape[0]), np.arange(o_qb),
         (qb_rows + np.arange(B_NOPE)).reshape(-1),
         (qb_rows + B_NOPE + np.arange(B_ROPE)).reshape(-1)])
    p_blocks = p_rows.reshape(-1, W_REGROUP_ROWS)
    assert (np.diff(p_blocks, axis=1) == 1).all() and (p_blocks[:, 0] % W_REGROUP_ROWS == 0).all()
    regroup = (wt.reshape(-1, W_REGROUP_ROWS, d),
               jnp.asarray(p_blocks[:, 0] // W_REGROUP_ROWS, jnp.int32))
    ck = lax.slice_in_dim(w_in[0], o_ckv, o_gl, axis=1)
    w_ck = jnp.concatenate([ck, ck[:, B_KV_RANK:]], axis=1).T.astype(BF16)
    w_kv = w_ukv[0].reshape(B_KV_RANK, B_HEADS, B_NOPE + B_V)
    w_k = w_kv[:, :, :B_NOPE].reshape(B_KV_RANK, B_HEADS * B_NOPE).astype(BF16)
    w_vt = w_kv[:, :, B_NOPE:].reshape(B_KV_RANK, B_HEADS * B_V).T.astype(BF16)
    w_r = jnp.concatenate(
        [w_group[0], jnp.transpose(w_router[0], (1, 0, 2)).reshape(d, N_EXPERTS),
         jnp.zeros((d, LANES - N_GROUPS - N_EXPERTS), F32)], axis=1).astype(BF16)
    b_r = jnp.concatenate([b_group[0], b_router[0].reshape(N_EXPERTS),
                           jnp.zeros((LANES - N_GROUPS - N_EXPERTS,), F32)])[None, :]

    cond = jnp.concatenate([c, c_ctx[None, :], jnp.zeros((SUBLANES - b - 1, d), F32)], axis=0)
    mod = _modulation(cond, w_mod[0], b_mod[0])
    mx = mod[:b].reshape(b, N_MOD, 1, d)
    sh1, sc1, g1, sh2, sc2, g2 = [mx[:, k] for k in range(N_MOD)]
    mc = jnp.broadcast_to(mod[b].reshape(N_MOD, 1, 1, d), (N_MOD, b, 1, d))
    csh1, csc1 = mc[0], mc[1]

    cos_a, sin_a = _rope_tables(s, A_HEAD_DIM)
    cos_b, sin_b = _rope_tables(s, B_ROPE)
    g_mix = norm_mix[0][None, :]
    g_kv = b_kv_norm[0][None, :]
    g_q = a_q_norm[0][None, :]
    g_k = a_k_norm[0][None, :]

    tm = min(TOKEN_TM, s)
    h, kb_lat, vbt_lat, (w_p,) = _prologue(x, sh1, sc1, g_mix, w_ck, g_kv, w_k, w_vt, cos_b, sin_b,
                                           True, tm, regroup)
    w_p = w_p.reshape(P_COLS, d)
    p_lat, vat_lat = _projection(h, w_p, g_q, g_k, cos_a, sin_a, cos_b, sin_b, True, tm)

    tab_c = jnp.zeros((n_ctx, LANES), F32)
    hc, kb_ctx, vbt_ctx, _ = _prologue(ctx, csh1, csc1, g_mix, w_ck, g_kv, w_k, w_vt, tab_c, tab_c,
                                       False, n_ctx)
    p_ctx, vat_ctx = _projection(hc, w_p, g_q, g_k, tab_c, tab_c, tab_c, tab_c, False, n_ctx)

    oa, (w_down, w_a, w_b, w_o) = _gqa_attention(
        p_lat, vat_lat, p_ctx, vat_ctx, min(GQA_TQ, s), min(GQA_TK, s),
        [w_e_down[0], w_br_a[0], w_br_b[0], w_out[0]])
    ob, (w_gate, w_up) = _mla_attention(p_lat, kb_lat, vbt_lat, kb_ctx, vbt_ctx, min(MLA_TQ, s),
                                        min(MLA_TK, s), [w_e_gate[0], w_e_up[0]])

    x_new, h2, route, route_t, cnt = _merge_route(
        oa, ob, p_lat, x, g1, sh2, sc2, norm_ffn[0][None, :],
        w_a, w_b, w_o, w_r, b_r,
        min(MERGE_TM, s))

    n = b * s
    counts = jnp.sum(cnt[:, 0, :N_EXPERTS], axis=0).astype(jnp.int32)
    plan = _moe_plan(route[0:2].astype(jnp.int32), counts, MOE_TM)
    y2 = _moe(h2.reshape(n, d), *plan, w_gate, w_up, w_down, MOE_TM)

    return _final(x_new, y2, route_t, g2, norm_final[None, :], tm)
```

```python
import functools
import math

import jax
import jax.numpy as jnp
import numpy as np
from jax import lax
from jax.experimental import pallas as pl
from jax.experimental.pallas import tpu as pltpu

GRID_W = 64
ROPE_THETA = 10000.0
EPS = 1e-6
A_HEADS = 8
A_KV_HEADS = 2
A_HEAD_DIM = 128
B_HEADS = 8
B_NOPE = 128
B_ROPE = 64
B_V = 128
B_KV_RANK = 512
N_GROUPS = 4
EXPERTS_PER_GROUP = 4
N_EXPERTS = N_GROUPS * EXPERTS_PER_GROUP
D_EXPERT = 1024
N_MOD = 6

LANES = 128
SUBLANES = 8
V7X_VMEM_LIMIT_BYTES = 56 * 1024 * 1024

BF16 = jnp.bfloat16
F32 = jnp.float32

P_GA = 0
P_GB = 2048
P_QA = 4096
P_KA = 5120
P_VA = 5376
P_QBN = 5632
P_QBR = 6656
P_COLS = 7168
PROJ_TN = 512

LOG2_E = math.log2(math.e)
A_SCORE_SCALE = LOG2_E / math.sqrt(A_HEAD_DIM)
B_SCORE_SCALE = LOG2_E / math.sqrt(B_NOPE + B_ROPE)

B_KEY_W = B_NOPE + 2 * B_ROPE
W_REGROUP_ROWS = 32
MOD_TN = 1024

TOKEN_TM = 512
MERGE_TM = 256
GQA_TQ = 512
MLA_TQ = 2048
GQA_TK = 512
MLA_TK = 512
MOE_TM = 512
MOE_LEAD_TILES = 1
MOE_SLOTS = MOE_LEAD_TILES + 1


def _cparams(*sem):
    return pltpu.CompilerParams(dimension_semantics=sem, vmem_limit_bytes=V7X_VMEM_LIMIT_BYTES)


def _dot(a, b):
    return jnp.dot(a, b, preferred_element_type=F32)


def _dot_nt(a, b):
    return lax.dot_general(a, b, (((1,), (1,)), ((), ())), preferred_element_type=F32)


def _resident(shape):
    return pl.BlockSpec(shape, lambda *_: (0,) * len(shape), pipeline_mode=pl.Buffered(1))


def _rms(v, gain):
    return v * lax.rsqrt(jnp.mean(v * v, axis=-1, keepdims=True) + EPS) * gain


def _mod_kernel(c_ref, w_ref, b_ref, o_ref):
    c = c_ref[...]
    s = (c * jax.nn.sigmoid(c)).astype(BF16)
    o_ref[...] = _dot(s, w_ref[...].astype(BF16)) + b_ref[...]


def _modulation(cond, w_mod, b_mod):
    rows, d = cond.shape
    n = w_mod.shape[1]
    tn = MOD_TN
    return pl.pallas_call(
        _mod_kernel,
        grid=(n // tn,),
        in_specs=[pl.BlockSpec((rows, d), lambda j: (0, 0)),
                  pl.BlockSpec((d, tn), lambda j: (0, j)),
                  pl.BlockSpec((1, tn), lambda j: (0, j))],
        out_specs=pl.BlockSpec((rows, tn), lambda j: (0, j)),
        out_shape=jax.ShapeDtypeStruct((rows, n), F32),
        compiler_params=_cparams("parallel"),
        name="modulation",
    )(cond, w_mod, b_mod.reshape(1, n))


def _cast_rows_kernel(tbl_ref, src_ref, dst_ref):
    del tbl_ref
    dst_ref[...] = src_ref[...].astype(BF16)


def _cast_rows(blocks, table):
    blk = (1,) + blocks.shape[1:]
    grid_spec = pltpu.PrefetchScalarGridSpec(
        num_scalar_prefetch=1,
        grid=(table.shape[0],),
        in_specs=[pl.BlockSpec(blk, lambda i, tbl: (tbl[i], 0, 0))],
        out_specs=pl.BlockSpec(blk, lambda i, tbl: (i, 0, 0)))
    return pl.pallas_call(
        _cast_rows_kernel,
        grid_spec=grid_spec,
        out_shape=jax.ShapeDtypeStruct((table.shape[0],) + blocks.shape[1:], BF16),
        compiler_params=_cparams("parallel"),
        name="latent_key_weight",
    )(table, blocks)


def _swap_halves_64(v):
    lane = lax.broadcasted_iota(jnp.int32, v.shape, 1)
    return jnp.where((lane & 63) < 32, pltpu.roll(v, LANES - 32, 1), pltpu.roll(v, 32, 1))


def _pre_kernel(use_rope, n_side, tbl_ref, x_ref, sh_ref, sc_ref, g_ref, wck_ref, gkv_ref, wk_ref,
                wvt_ref, cb_ref, sb_ref, *refs):
    del tbl_ref
    side_src, (h_ref, kb_ref, vbt_ref), side_dst = refs[:n_side], refs[n_side:n_side + 3], refs[n_side + 3:]
    for k, src in enumerate(side_src):
        side_dst[0][k] = src[0].astype(BF16)
    xf = x_ref[0]
    h = _rms(xf, g_ref[...]) * (1.0 + sc_ref[0]) + sh_ref[0]
    hb = h.astype(BF16)
    h_ref[0] = hb
    p = _dot_nt(hb, wck_ref[...])
    cn = _rms(p[:, :B_KV_RANK], gkv_ref[...]).astype(BF16)
    kr2 = p[:, B_KV_RANK:]
    if use_rope:
        kr2 = kr2 * cb_ref[...] + _swap_halves_64(kr2) * sb_ref[...]
    kr2 = kr2.astype(BF16)
    kbn = _dot(cn, wk_ref[...]).astype(BF16)
    vbt_ref[0] = _dot_nt(wvt_ref[...], cn).astype(BF16).reshape(vbt_ref.shape[1:])
    for hd in range(B_HEADS):
        kb_ref[0, :, hd * B_KEY_W:hd * B_KEY_W + B_NOPE] = kbn[:, hd * B_NOPE:(hd + 1) * B_NOPE]
        kb_ref[0, :, hd * B_KEY_W + B_NOPE:(hd + 1) * B_KEY_W] = kr2


def _prologue(x, shift, scale, gain, w_ck, g_kv, w_k, w_vt, cos_b, sin_b, use_rope, tm,
              regroup=None):
    bt, st, d = x.shape
    n_i = st // tm
    row = lambda b, i, tbl: (b, i, 0)
    per_b = lambda b, i, tbl: (b, 0, 0)
    fixed = lambda b, i, tbl: (0, 0)
    tab = lambda b, i, tbl: (i, 0)
    side_in, side_out, side_shapes, side_args = [], [], [], []
    table = jnp.zeros((1,), jnp.int32)
    if regroup is not None:
        blocks, table = regroup
        per_step = table.shape[0] // (bt * n_i)
        assert per_step * bt * n_i == table.shape[0]
        for k in range(per_step):
            side_in.append(pl.BlockSpec(
                (1,) + blocks.shape[1:],
                lambda b, i, tbl, k=k: (tbl[(b * n_i + i) * per_step + k], 0, 0)))
            side_args.append(blocks)
        side_out.append(pl.BlockSpec((per_step,) + blocks.shape[1:],
                                     lambda b, i, tbl: (b * n_i + i, 0, 0)))
        side_shapes.append(jax.ShapeDtypeStruct((table.shape[0],) + blocks.shape[1:], BF16))
    grid_spec = pltpu.PrefetchScalarGridSpec(
        num_scalar_prefetch=1,
        grid=(bt, n_i),
        in_specs=[pl.BlockSpec((1, tm, d), row),
                  pl.BlockSpec((1, 1, d), per_b),
                  pl.BlockSpec((1, 1, d), per_b),
                  pl.BlockSpec((1, d), fixed),
                  _resident(w_ck.shape),
                  pl.BlockSpec((1, B_KV_RANK), fixed),
                  _resident(w_k.shape),
                  _resident(w_vt.shape),
                  pl.BlockSpec((tm, LANES), tab),
                  pl.BlockSpec((tm, LANES), tab),
                  *side_in],
        out_specs=[pl.BlockSpec((1, tm, d), row),
                   pl.BlockSpec((1, tm, B_HEADS * B_KEY_W), row),
                   pl.BlockSpec((1, B_HEADS, B_V, tm), lambda b, i, tbl: (b, 0, 0, i)),
                   *side_out],
    )
    h, kb, vbt, *side = pl.pallas_call(
        functools.partial(_pre_kernel, use_rope, len(side_in)),
        grid_spec=grid_spec,
        out_shape=[jax.ShapeDtypeStruct((bt, st, d), BF16),
                   jax.ShapeDtypeStruct((bt, st, B_HEADS * B_KEY_W), BF16),
                   jax.ShapeDtypeStruct((bt, B_HEADS, B_V, st), BF16),
                   *side_shapes],
        compiler_params=_cparams("parallel", "parallel"),
        name="prologue_rope" if use_rope else "prologue_ctx",
    )(table, x, shift, scale, gain, w_ck, g_kv, w_k, w_vt, cos_b, sin_b, *side_args)
    return h, kb, vbt, side


def _proj_kernel(use_rope, h_ref, w_ref, gq_ref, gk_ref, ca_ref, sa_ref, cb_ref, sb_ref,
                 o_ref, vat_ref):
    h = h_ref[0]
    n_blk = PROJ_TN // LANES

    def rope_a(v):
        if not use_rope:
            return v
        return v * ca_ref[...] + pltpu.roll(v, A_HEAD_DIM // 2, 1) * sa_ref[...]

    def rope_b(v):
        if not use_rope:
            return v
        return v * cb_ref[...] + _swap_halves_64(v) * sb_ref[...]

    for j in range(P_COLS // PROJ_TN):
        c0 = j * PROJ_TN
        acc = _dot_nt(h, w_ref[c0:c0 + PROJ_TN, :])

        def blk(k):
            return acc[:, k * LANES:(k + 1) * LANES]

        def put(k, v):
            o_ref[0, :, c0 + k * LANES:c0 + (k + 1) * LANES] = v.astype(BF16)

        if c0 < P_QA:
            o_ref[0, :, c0:c0 + PROJ_TN] = jax.nn.sigmoid(acc).astype(BF16)
        elif c0 < P_KA:
            for k in range(n_blk):
                put(k, rope_a(_rms(blk(k), gq_ref[...])) * A_SCORE_SCALE)
        elif c0 < P_QBN:
            for k in range(A_KV_HEADS):
                put(k, rope_a(_rms(blk(k), gk_ref[...])))
            for k in range(A_KV_HEADS, n_blk):
                put(k, blk(k))
                vat_ref[0, k - A_KV_HEADS] = blk(k).T.astype(BF16)
        elif c0 < P_QBR:
            o_ref[0, :, c0:c0 + PROJ_TN] = (acc * B_SCORE_SCALE).astype(BF16)
        else:
            for k in range(n_blk):
                put(k, rope_b(blk(k)) * B_SCORE_SCALE)


def _projection(h, w_p, g_q, g_k, cos_a, sin_a, cos_b, sin_b, use_rope, tm):
    bt, st, d = h.shape
    fixed = lambda b, i: (0, 0)
    tab = lambda b, i: (i, 0)
    return pl.pallas_call(
        functools.partial(_proj_kernel, use_rope),
        grid=(bt, st // tm),
        in_specs=[pl.BlockSpec((1, tm, d), lambda b, i: (b, i, 0)),
                  _resident(w_p.shape),
                  pl.BlockSpec((1, LANES), fixed),
                  pl.BlockSpec((1, LANES), fixed),
                  pl.BlockSpec((tm, LANES), tab),
                  pl.BlockSpec((tm, LANES), tab),
                  pl.BlockSpec((tm, LANES), tab),
                  pl.BlockSpec((tm, LANES), tab)],
        out_specs=[pl.BlockSpec((1, tm, P_COLS), lambda b, i: (b, i, 0)),
                   pl.BlockSpec((1, A_KV_HEADS, A_HEAD_DIM, tm), lambda b, i: (b, 0, 0, i))],
        out_shape=[jax.ShapeDtypeStruct((bt, st, P_COLS), BF16),
                   jax.ShapeDtypeStruct((bt, A_KV_HEADS, A_HEAD_DIM, st), BF16)],
        compiler_params=_cparams("parallel", "parallel"),
        name="projection_rope" if use_rope else "projection_ctx",
    )(h, w_p, g_q, g_k, cos_a, sin_a, cos_b, sin_b)


def _flash(q_ref, kc_ref, vct_ref, kl_ref, vlt_ref, st_ref, acc_ref, tk):
    n_chunks = kl_ref.shape[1] // tk

    def latent_scores(c, slot):
        st_ref[slot] = _dot_nt(kl_ref[0, c * tk:(c + 1) * tk, :], q_ref[...])

    def update(c, slot, m, l):
        st = st_ref[slot]
        m_new = jnp.maximum(m, jnp.max(st, axis=0, keepdims=True))
        alpha = jnp.exp2(m - m_new)
        pt = jnp.exp2(st - m_new)
        acc_ref[...] = alpha * acc_ref[...] + _dot(vlt_ref[0, 0, :, c * tk:(c + 1) * tk],
                                                  pt.astype(BF16))
        return m_new, alpha * l + jnp.sum(pt, axis=0, keepdims=True)

    latent_scores(0, 0)
    st = _dot_nt(kc_ref[0], q_ref[...])
    m = jnp.max(st, axis=0, keepdims=True)
    pt = jnp.exp2(st - m)
    l = jnp.sum(pt, axis=0, keepdims=True)
    acc_ref[...] = _dot(vct_ref[0, 0], pt.astype(BF16))

    for c in range(n_chunks):
        if c + 1 < n_chunks:
            latent_scores(c + 1, (c + 1) & 1)
        m, l = update(c, c & 1, m, l)
    return (acc_ref[...] / l).T


def _flash_scratch(m_rows, dk, dv, tk):
    return [pltpu.VMEM((m_rows, dk), BF16),
            pltpu.VMEM((2, tk, m_rows), F32),
            pltpu.VMEM((dv, m_rows), F32)]


class _SideCasts:
    def __init__(self, arrays, grid):
        self.n_steps = math.prod(grid)
        strides = [math.prod(grid[k + 1:]) for k in range(len(grid))]
        step = lambda *g: sum(i * st for i, st in zip(g, strides))
        self.shapes = [a.shape for a in arrays]
        self.views, self.in_specs, self.out_specs, self.out_shapes = [], [], [], []
        for a in arrays:
            cols = a.shape[-1]
            rows = math.prod(a.shape[:-1]) // self.n_steps
            assert rows * self.n_steps == math.prod(a.shape[:-1]) and rows % 16 == 0
            self.views.append(a.reshape(self.n_steps, rows, cols))
            spec = pl.BlockSpec((1, rows, cols), lambda *g: (step(*g), 0, 0))
            self.in_specs.append(spec)
            self.out_specs.append(spec)
            self.out_shapes.append(jax.ShapeDtypeStruct((self.n_steps, rows, cols), BF16))

    @staticmethod
    def run(src_refs, dst_refs):
        for src, dst in zip(src_refs, dst_refs):
            dst[...] = src[...].astype(BF16)

    def restore(self, outs):
        return [o.reshape(shape) for o, shape in zip(outs, self.shapes)]


def _gqa_kernel(tk, n_side, q_ref, kl_ref, vlt_ref, kc_ref, vct_ref, *refs):
    side_src, (o_ref, *side_dst) = refs[:n_side], refs[n_side:2 * n_side + 1]
    qs_ref, st_ref, acc_ref = refs[2 * n_side + 1:]
    _SideCasts.run(side_src, side_dst)
    group = A_HEADS // A_KV_HEADS
    tq = q_ref.shape[1]
    for g in range(group):
        qs_ref[g * tq:(g + 1) * tq, :] = q_ref[0, :, g * A_HEAD_DIM:(g + 1) * A_HEAD_DIM]
    out = _flash(qs_ref, kc_ref, vct_ref, kl_ref, vlt_ref, st_ref, acc_ref, tk)
    for g in range(group):
        o_ref[0, :, g * A_HEAD_DIM:(g + 1) * A_HEAD_DIM] = out[g * tq:(g + 1) * tq].astype(BF16)


def _gqa_attention(p_lat, vat_lat, p_ctx, vat_ctx, tq, tk, cast_arrays):
    b, s, _ = p_lat.shape
    n_ctx = p_ctx.shape[1]
    group_w = (A_HEADS // A_KV_HEADS) * A_HEAD_DIM
    grid = (b, A_KV_HEADS, s // tq)
    side = _SideCasts(cast_arrays, grid)
    oa, *cast = pl.pallas_call(
        functools.partial(_gqa_kernel, tk, len(cast_arrays)),
        grid=grid,
        in_specs=[pl.BlockSpec((1, tq, group_w), lambda bb, k, i: (bb, i, P_QA // group_w + k)),
                  pl.BlockSpec((1, s, LANES), lambda bb, k, i: (bb, 0, P_KA // LANES + k)),
                  pl.BlockSpec((1, 1, A_HEAD_DIM, s), lambda bb, k, i: (bb, k, 0, 0)),
                  pl.BlockSpec((1, n_ctx, LANES), lambda bb, k, i: (bb, 0, P_KA // LANES + k)),
                  pl.BlockSpec((1, 1, A_HEAD_DIM, n_ctx), lambda bb, k, i: (bb, k, 0, 0)),
                  *side.in_specs],
        out_specs=[pl.BlockSpec((1, tq, group_w), lambda bb, k, i: (bb, i, k)), *side.out_specs],
        out_shape=[jax.ShapeDtypeStruct((b, s, A_HEADS * A_HEAD_DIM), BF16), *side.out_shapes],
        scratch_shapes=_flash_scratch((A_HEADS // A_KV_HEADS) * tq, A_HEAD_DIM, A_HEAD_DIM, tk),
        compiler_params=_cparams("parallel", "parallel", "parallel"),
        name="gqa_attention",
    )(p_lat, p_lat, vat_lat, p_ctx, vat_ctx, *side.views)
    return oa, side.restore(cast)


def _mla_kernel(tk, n_side, qn_ref, qr_ref, kl_ref, vlt_ref, kc_ref, vct_ref, *refs):
    side_src, (o_ref, *side_dst) = refs[:n_side], refs[n_side:2 * n_side + 1]
    qs_ref, st_ref, acc_ref = refs[2 * n_side + 1:]
    _SideCasts.run(side_src, side_dst)
    hd = pl.program_id(1)
    qr = qr_ref[0]
    lane = lax.broadcasted_iota(jnp.int32, qr.shape, 1)
    qs_ref[:, :B_NOPE] = qn_ref[0]
    qs_ref[:, B_NOPE:] = jnp.where((lane >> 6) == (hd & 1), qr, jnp.zeros_like(qr))
    o_ref[0] = _flash(qs_ref, kc_ref, vct_ref, kl_ref, vlt_ref, st_ref, acc_ref, tk).astype(BF16)


def _mla_attention(p_lat, kb_lat, vbt_lat, kb_ctx, vbt_ctx, tq, tk, cast_arrays):
    b, s, _ = p_lat.shape
    n_ctx = kb_ctx.shape[1]
    grid = (b, B_HEADS, s // tq)
    side = _SideCasts(cast_arrays, grid)
    ob, *cast = pl.pallas_call(
        functools.partial(_mla_kernel, tk, len(cast_arrays)),
        grid=grid,
        in_specs=[pl.BlockSpec((1, tq, LANES), lambda bb, h, i: (bb, i, P_QBN // LANES + h)),
                  pl.BlockSpec((1, tq, LANES), lambda bb, h, i: (bb, i, P_QBR // LANES + h // 2)),
                  pl.BlockSpec((1, s, B_KEY_W), lambda bb, h, i: (bb, 0, h)),
                  pl.BlockSpec((1, 1, B_V, s), lambda bb, h, i: (bb, h, 0, 0)),
                  pl.BlockSpec((1, n_ctx, B_KEY_W), lambda bb, h, i: (bb, 0, h)),
                  pl.BlockSpec((1, 1, B_V, n_ctx), lambda bb, h, i: (bb, h, 0, 0)),
                  *side.in_specs],
        out_specs=[pl.BlockSpec((1, tq, B_V), lambda bb, h, i: (bb, i, h)), *side.out_specs],
        out_shape=[jax.ShapeDtypeStruct((b, s, B_HEADS * B_V), BF16), *side.out_shapes],
        scratch_shapes=_flash_scratch(tq, B_KEY_W, B_V, tk),
        compiler_params=_cparams("parallel", "parallel", "parallel"),
        name="mla_attention",
    )(p_lat, p_lat, kb_lat, vbt_lat, kb_ctx, vbt_ctx, *side.views)
    return ob, side.restore(cast)


def _merge_kernel(oa_ref, ob_ref, ga_ref, gb_ref, x_ref, g1_ref, sh_ref, sc_ref, gn_ref,
                  wa_ref, wb_ref, wo_ref, wr_ref, br_ref, xn_ref, h2_ref, rt_ref, rtt_ref, cnt_ref):
    ya = _dot(oa_ref[0], wa_ref[...])
    yb = _dot(ob_ref[0], wb_ref[...])
    mix = (ga_ref[0].astype(F32) * ya + gb_ref[0].astype(F32) * yb).astype(BF16)
    xn = x_ref[0] + g1_ref[0] * _dot(mix, wo_ref[...])
    xn_ref[0] = xn
    h2 = _rms(xn, gn_ref[...]) * (1.0 + sc_ref[0]) + sh_ref[0]
    h2_ref[0] = h2
    logits = _dot(h2.astype(BF16), wr_ref[...]) + br_ref[...]
    lt = logits.T
    gl = [lt[g:g + 1, :] for g in range(N_GROUPS)]
    gmax = functools.reduce(jnp.maximum, gl)
    gsum = functools.reduce(lambda a, b_: a + b_, [jnp.exp(v - gmax) for v in gl])
    g_val = 1.0 / gsum
    g_idx = jnp.full(gmax.shape, N_GROUPS - 1, jnp.int32)
    for g in range(N_GROUPS - 2, -1, -1):
        g_idx = jnp.where(gl[g] == gmax, g, g_idx)
    el = []
    for e in range(EXPERTS_PER_GROUP):
        v = lt[N_GROUPS + e:N_GROUPS + e + 1, :]
        for g in range(1, N_GROUPS):
            row = N_GROUPS + g * EXPERTS_PER_GROUP + e
            v = jnp.where(g_idx == g, lt[row:row + 1, :], v)
        el.append(v)
    emax = functools.reduce(jnp.maximum, el)
    i1 = jnp.full(emax.shape, EXPERTS_PER_GROUP - 1, jnp.int32)
    for e in range(EXPERTS_PER_GROUP - 2, -1, -1):
        i1 = jnp.where(el[e] == emax, e, i1)
    neg = jnp.full(emax.shape, -jnp.inf, F32)
    el2 = [jnp.where(i1 == e, neg, el[e]) for e in range(EXPERTS_PER_GROUP)]
    emax2 = functools.reduce(jnp.maximum, el2)
    i2 = jnp.full(emax.shape, EXPERTS_PER_GROUP - 1, jnp.int32)
    for e in range(EXPERTS_PER_GROUP - 2, -1, -1):
        i2 = jnp.where(el2[e] == emax2, e, i2)
    p2 = jnp.exp(emax2 - emax)
    w1 = g_val / (1.0 + p2)
    w2 = g_val * p2 / (1.0 + p2)
    e1 = g_idx * EXPERTS_PER_GROUP + i1
    e2 = g_idx * EXPERTS_PER_GROUP + i2
    zero = jnp.zeros_like(w1)
    rt = jnp.concatenate([e1.astype(F32), e2.astype(F32), w1, w2, zero, zero, zero, zero], axis=0)
    rt_ref[...] = rt
    rtt_ref[...] = jnp.concatenate([rt, jnp.zeros((LANES - SUBLANES, rt.shape[1]), F32)], axis=0).T
    lane = lax.broadcasted_iota(jnp.int32, (1, LANES), 1)
    counts = jnp.zeros((1, LANES), F32)
    for e in range(N_EXPERTS):
        hits = jnp.where(e1 == e, 1.0, 0.0) + jnp.where(e2 == e, 1.0, 0.0)
        counts = counts + jnp.where(lane == e, jnp.sum(hits, axis=1, keepdims=True), 0.0)
    cnt_ref[0] = jnp.broadcast_to(counts, (SUBLANES, LANES))


def _merge_route(oa, ob, p_lat, x, g1, sh2, sc2, g_ffn, w_a, w_b, w_o, w_r, b_r, tm):
    b, s, d = x.shape
    n_i = s // tm
    row = lambda bb, i: (bb, i, 0)
    per_b = lambda bb, i: (bb, 0, 0)
    fixed = lambda bb, i: (0, 0)
    return pl.pallas_call(
        _merge_kernel,
        grid=(b, n_i),
        in_specs=[pl.BlockSpec((1, tm, oa.shape[2]), row),
                  pl.BlockSpec((1, tm, ob.shape[2]), row),
                  pl.BlockSpec((1, tm, d), lambda bb, i: (bb, i, P_GA // d)),
                  pl.BlockSpec((1, tm, d), lambda bb, i: (bb, i, P_GB // d)),
                  pl.BlockSpec((1, tm, d), row),
                  pl.BlockSpec((1, 1, d), per_b),
                  pl.BlockSpec((1, 1, d), per_b),
                  pl.BlockSpec((1, 1, d), per_b),
                  pl.BlockSpec((1, d), fixed),
                  _resident(w_a.shape),
                  _resident(w_b.shape),
                  _resident(w_o.shape),
                  _resident(w_r.shape),
                  pl.BlockSpec((1, LANES), fixed)],
        out_specs=[pl.BlockSpec((1, tm, d), row),
                   pl.BlockSpec((1, tm, d), row),
                   pl.BlockSpec((SUBLANES, tm), lambda bb, i: (0, bb * n_i + i)),
                   pl.BlockSpec((tm, LANES), lambda bb, i: (bb * n_i + i, 0)),
                   pl.BlockSpec((1, SUBLANES, LANES), lambda bb, i: (bb * n_i + i, 0, 0))],
        out_shape=[jax.ShapeDtypeStruct((b, s, d), F32),
                   jax.ShapeDtypeStruct((b, s, d), F32),
                   jax.ShapeDtypeStruct((SUBLANES, b * s), F32),
                   jax.ShapeDtypeStruct((b * s, LANES), F32),
                   jax.ShapeDtypeStruct((b * n_i, SUBLANES, LANES), F32)],
        compiler_params=_cparams("parallel", "parallel"),
        name="merge_route",
    )(oa, ob, p_lat, p_lat, x, g1, sh2, sc2, g_ffn, w_a, w_b, w_o, w_r, b_r)


def _moe_kernel(tm, na_ref, te_ref, src_ref, dst_ref, h_hbm, wg_ref, wu_ref, wd_ref,
                y_hbm, xbuf, ybuf, gsem, ssem):
    t = pl.program_id(0)
    n_active = na_ref[0]
    phase = lax.rem(t, MOE_SLOTS)

    def gather_start(tile, slot):
        for r in range(tm):
            pltpu.make_async_copy(h_hbm.at[pl.ds(src_ref[tile * tm + r], 1)],
                                  xbuf.at[slot, pl.ds(r, 1)], gsem.at[slot]).start()

    def gather_wait(slot):
        pltpu.make_async_copy(h_hbm.at[pl.ds(0, tm)], xbuf.at[slot], gsem.at[slot]).wait()

    def scatter_start(tile, slot):
        for r in range(tm):
            pltpu.make_async_copy(ybuf.at[slot, pl.ds(r, 1)],
                                  y_hbm.at[pl.ds(dst_ref[(tile + MOE_LEAD_TILES) * tm + r], 1)],
                                  ssem.at[slot]).start()

    def scatter_wait(slot):
        pltpu.make_async_copy(ybuf.at[slot], y_hbm.at[pl.ds(0, tm)], ssem.at[slot]).wait()

    @pl.when(t == 0)
    def _():
        for k in range(1, MOE_LEAD_TILES + 1):
            ybuf[(-k) % MOE_SLOTS] = jnp.zeros(ybuf.shape[1:], ybuf.dtype)
        for k in range(2, MOE_LEAD_TILES + 1):
            scatter_start(-k, (-k) % MOE_SLOTS)
        gather_start(0, 0)

    def step(cur):
        nxt = (cur + 1) % MOE_SLOTS
        gather_wait(cur)
        gate = _dot(xbuf[cur].astype(BF16), wg_ref[0])
        gather_start(t + 1, nxt)
        up = _dot(xbuf[cur].astype(BF16), wu_ref[0])
        scatter_start(t - 1, (cur - 1) % MOE_SLOTS)
        hid = (gate * jax.nn.sigmoid(gate) * up).astype(BF16)
        ybuf[cur] = _dot(hid, wd_ref[0])
        scatter_wait((cur - MOE_LEAD_TILES) % MOE_SLOTS)

    def drain(cur):
        scatter_start(t, cur)
        for k in range(MOE_LEAD_TILES - 1, -1, -1):
            scatter_wait((cur - k) % MOE_SLOTS)
        gather_wait((cur + 1) % MOE_SLOTS)

    for cur in range(MOE_SLOTS):
        mine = phase == cur
        pl.when(mine & (t < n_active))(functools.partial(step, cur))
        pl.when(mine & (t == n_active - 1))(functools.partial(drain, cur))


def _moe(h2, n_active, tile_expert, src_tok, dst_row, w_gate, w_up, w_down, tm):
    n, d = h2.shape
    n_tiles = tile_expert.shape[0]
    w_in_map = lambda t, na, te, sr, ds: (te[t], 0, 0)
    grid_spec = pltpu.PrefetchScalarGridSpec(
        num_scalar_prefetch=4,
        grid=(n_tiles,),
        in_specs=[pl.BlockSpec(memory_space=pl.ANY),
                  pl.BlockSpec((1, d, D_EXPERT), w_in_map),
                  pl.BlockSpec((1, d, D_EXPERT), w_in_map),
                  pl.BlockSpec((1, D_EXPERT, d), w_in_map)],
        out_specs=pl.BlockSpec(memory_space=pl.ANY),
        scratch_shapes=[pltpu.VMEM((MOE_SLOTS, tm, d), F32),
                        pltpu.VMEM((MOE_SLOTS, tm, d), F32),
                        pltpu.SemaphoreType.DMA((MOE_SLOTS,)),
                        pltpu.SemaphoreType.DMA((MOE_SLOTS,))],
    )
    return pl.pallas_call(
        functools.partial(_moe_kernel, tm),
        grid_spec=grid_spec,
        out_shape=jax.ShapeDtypeStruct((2 * n + MOE_LEAD_TILES * tm, d), F32),
        compiler_params=_cparams("arbitrary"),
        name="moe_experts",
    )(n_active, tile_expert, src_tok, dst_row, h2, w_gate, w_up, w_down)


def _moe_plan(eid, counts, tm):
    n = eid.shape[1]
    pairs = 2 * n
    n_tiles = pairs // tm + N_EXPERTS
    _, order = lax.sort((eid.reshape(pairs), lax.iota(jnp.int32, pairs)), num_keys=1)
    padded = ((counts + tm - 1) // tm) * tm
    pad_end = jnp.cumsum(padded)
    pad_start = pad_end - padded
    raw_start = jnp.cumsum(counts) - counts
    tile_row0 = jnp.arange(n_tiles, dtype=jnp.int32) * tm
    tile_expert = jnp.minimum(
        jnp.sum(tile_row0[:, None] >= pad_end[None, :], axis=1), N_EXPERTS - 1).astype(jnp.int32)
    shift = (raw_start - pad_start)[tile_expert]
    limit = (pad_start + counts)[tile_expert]
    pos = jnp.arange(n_tiles * tm, dtype=jnp.int32).reshape(n_tiles, tm)
    valid = pos < limit[:, None]
    pair = jnp.where(valid, order[jnp.clip(pos + shift[:, None], 0, pairs - 1)], 0).reshape(-1)
    valid = valid.reshape(-1)
    dummy = pairs + jnp.arange(MOE_LEAD_TILES * tm, dtype=jnp.int32)
    pad_dst = (pairs + ((pos // tm) % MOE_LEAD_TILES) * tm + pos % tm).reshape(-1)
    src_tok = jnp.concatenate([pair % n, jnp.zeros((tm,), jnp.int32)]).astype(jnp.int32)
    dst_row = jnp.concatenate([dummy, jnp.where(valid, pair, pad_dst)]).astype(jnp.int32)
    n_active = (pad_end[-1:] // tm).astype(jnp.int32)
    return n_active, tile_expert, src_tok, dst_row


def _final_kernel(x_ref, y0_ref, y1_ref, rtt_ref, g2_ref, gn_ref, o_ref):
    wt = rtt_ref[...]
    moe = wt[:, 2:3] * y0_ref[...] + wt[:, 3:4] * y1_ref[...]
    o_ref[0] = _rms(x_ref[0] + g2_ref[0] * moe, gn_ref[...])


def _final(x_new, y2, route, g2, g_final, tm):
    b, s, d = x_new.shape
    row = lambda bb, i: (bb, i, 0)
    n_i = s // tm
    return pl.pallas_call(
        _final_kernel,
        grid=(b, n_i),
        in_specs=[pl.BlockSpec((1, tm, d), row),
                  pl.BlockSpec((tm, d), lambda bb, i: (bb * n_i + i, 0)),
                  pl.BlockSpec((tm, d), lambda bb, i: (b * n_i + bb * n_i + i, 0)),
                  pl.BlockSpec((tm, LANES), lambda bb, i: (bb * n_i + i, 0)),
                  pl.BlockSpec((1, 1, d), lambda bb, i: (bb, 0, 0)),
                  pl.BlockSpec((1, d), lambda bb, i: (0, 0))],
        out_specs=pl.BlockSpec((1, tm, d), row),
        out_shape=jax.ShapeDtypeStruct((b, s, d), F32),
        compiler_params=_cparams("parallel", "parallel"),
        name="final_norm",
    )(x_new, y2, y2, route, g2, g_final)


def _rope_tables(n_tokens, rot_dim):
    rows = n_tokens // GRID_W
    row = np.repeat(np.arange(rows), GRID_W).astype(np.float64)
    col = np.tile(np.arange(GRID_W), rows).astype(np.float64)
    n_freq = rot_dim // 4
    freqs = ROPE_THETA ** (-np.arange(n_freq, dtype=np.float64) / n_freq)
    ang = np.concatenate([row[:, None] * freqs, col[:, None] * freqs], axis=-1)
    cos, sin = np.cos(ang), np.sin(ang)
    reps = LANES // rot_dim
    cos_t = np.tile(np.concatenate([cos, cos], axis=-1), (1, reps))
    sin_t = np.tile(np.concatenate([-sin, sin], axis=-1), (1, reps))
    return jnp.asarray(cos_t, F32), jnp.asarray(sin_t, F32)


def kernel(x, c, ctx, c_ctx, w_mod, b_mod, norm_mix, norm_ffn, w_in, a_q_norm, a_k_norm, b_kv_norm, w_ukv, w_br_a, w_br_b, w_out, w_group, b_group, w_router, b_router, w_e_gate, w_e_up, w_e_down, norm_final):
    b, s, d = x.shape
    n_ctx = ctx.shape[1]
    assert w_mod.shape[0] == 1, "single-layer block"
    assert s % GRID_W == 0

    wt = jnp.transpose(w_in[0])
    qa_w = A_HEADS * A_HEAD_DIM
    kv_w = A_KV_HEADS * A_HEAD_DIM
    o_qb = qa_w + 2 * kv_w
    qb_w = B_HEADS * (B_NOPE + B_ROPE)
    o_ckv = o_qb + qb_w
    o_kr = o_ckv + B_KV_RANK
    o_gl = o_kr + B_ROPE
    qb_rows = o_qb + (B_NOPE + B_ROPE) * np.arange(B_HEADS)[:, None]
    p_rows = np.concatenate(
        [np.arange(o_gl, wt.shape[0]), np.arange(o_qb),
         (qb_rows + np.arange(B_NOPE)).reshape(-1),
         (qb_rows + B_NOPE + np.arange(B_ROPE)).reshape(-1)])
    p_blocks = p_rows.reshape(-1, W_REGROUP_ROWS)
    assert (np.diff(p_blocks, axis=1) == 1).all() and (p_blocks[:, 0] % W_REGROUP_ROWS == 0).all()
    w_blocks = wt.reshape(-1, W_REGROUP_ROWS, d)
    regroup = (w_blocks, jnp.asarray(p_blocks[:, 0] // W_REGROUP_ROWS, jnp.int32))
    ck_blocks = np.concatenate([np.arange(o_ckv, o_gl), np.arange(o_kr, o_gl)])[::W_REGROUP_ROWS]
    w_ck = _cast_rows(w_blocks, jnp.asarray(ck_blocks // W_REGROUP_ROWS, jnp.int32))
    w_ck = w_ck.reshape(-1, d)
    w_kv = w_ukv[0].reshape(B_KV_RANK, B_HEADS, B_NOPE + B_V)
    w_k = w_kv[:, :, :B_NOPE].reshape(B_KV_RANK, B_HEADS * B_NOPE).astype(BF16)
    w_vt = w_kv[:, :, B_NOPE:].reshape(B_KV_RANK, B_HEADS * B_V).T.astype(BF16)
    w_r = jnp.concatenate(
        [w_group[0], jnp.transpose(w_router[0], (1, 0, 2)).reshape(d, N_EXPERTS),
         jnp.zeros((d, LANES - N_GROUPS - N_EXPERTS), F32)], axis=1).astype(BF16)
    b_r = jnp.concatenate([b_group[0], b_router[0].reshape(N_EXPERTS),
                           jnp.zeros((LANES - N_GROUPS - N_EXPERTS,), F32)])[None, :]

    cond = jnp.concatenate([c, c_ctx[None, :], jnp.zeros((SUBLANES - b - 1, d), F32)], axis=0)
    mod = _modulation(cond, w_mod[0], b_mod[0])
    mx = mod[:b].reshape(b, N_MOD, 1, d)
    sh1, sc1, g1, sh2, sc2, g2 = [mx[:, k] for k in range(N_MOD)]
    mc = jnp.broadcast_to(mod[b].reshape(N_MOD, 1, 1, d), (N_MOD, b, 1, d))
    csh1, csc1 = mc[0], mc[1]

    cos_a, sin_a = _rope_tables(s, A_HEAD_DIM)
    cos_b, sin_b = _rope_tables(s, B_ROPE)
    g_mix = norm_mix[0][None, :]
    g_kv = b_kv_norm[0][None, :]
    g_q = a_q_norm[0][None, :]
    g_k = a_k_norm[0][None, :]

    tm = min(TOKEN_TM, s)
    h, kb_lat, vbt_lat, (w_p,) = _prologue(x, sh1, sc1, g_mix, w_ck, g_kv, w_k, w_vt, cos_b, sin_b,
                                           True, tm, regroup)
    w_p = w_p.reshape(P_COLS, d)
    p_lat, vat_lat = _projection(h, w_p, g_q, g_k, cos_a, sin_a, cos_b, sin_b, True, tm)

    tab_c = jnp.zeros((n_ctx, LANES), F32)
    hc, kb_ctx, vbt_ctx, _ = _prologue(ctx, csh1, csc1, g_mix, w_ck, g_kv, w_k, w_vt, tab_c, tab_c,
                                       False, n_ctx)
    p_ctx, vat_ctx = _projection(hc, w_p, g_q, g_k, tab_c, tab_c, tab_c, tab_c, False, n_ctx)

    oa, (w_down, w_a, w_b, w_o) = _gqa_attention(
        p_lat, vat_lat, p_ctx, vat_ctx, min(GQA_TQ, s), min(GQA_TK, s),
        [w_e_down[0], w_br_a[0], w_br_b[0], w_out[0]])
    ob, (w_gate, w_up) = _mla_attention(p_lat, kb_lat, vbt_lat, kb_ctx, vbt_ctx, min(MLA_TQ, s),
                                        min(MLA_TK, s), [w_e_gate[0], w_e_up[0]])

    x_new, h2, route, route_t, cnt = _merge_route(
        oa, ob, p_lat, x, g1, sh2, sc2, norm_ffn[0][None, :],
        w_a, w_b, w_o, w_r, b_r,
        min(MERGE_TM, s))

    n = b * s
    counts = jnp.sum(cnt[:, 0, :N_EXPERTS], axis=0).astype(jnp.int32)
    plan = _moe_plan(route[0:2].astype(jnp.int32), counts, MOE_TM)
    y2 = _moe(h2.reshape(n, d), *plan, w_gate, w_up, w_down, MOE_TM)

    return _final(x_new, y2, route_t, g2, norm_final[None, :], tm)
```

```python
import functools
import math

import jax
import jax.numpy as jnp
import numpy as np
from jax import lax
from jax.experimental import pallas as pl
from jax.experimental.pallas import tpu as pltpu

GRID_W = 64
ROPE_THETA = 10000.0
EPS = 1e-6
A_HEADS = 8
A_KV_HEADS = 2
A_HEAD_DIM = 128
B_HEADS = 8
B_NOPE = 128
B_ROPE = 64
B_V = 128
B_KV_RANK = 512
N_GROUPS = 4
EXPERTS_PER_GROUP = 4
N_EXPERTS = N_GROUPS * EXPERTS_PER_GROUP
D_EXPERT = 1024
N_MOD = 6

LANES = 128
SUBLANES = 8
V7X_VMEM_LIMIT_BYTES = 56 * 1024 * 1024

BF16 = jnp.bfloat16
F32 = jnp.float32

P_GA = 0
P_GB = 2048
P_QA = 4096
P_KA = 5120
P_VA = 5376
P_QBN = 5632
P_QBR = 6656
P_COLS = 7168
PROJ_TN = 512

LOG2_E = math.log2(math.e)
A_SCORE_SCALE = LOG2_E / math.sqrt(A_HEAD_DIM)
B_SCORE_SCALE = LOG2_E / math.sqrt(B_NOPE + B_ROPE)

B_KEY_W = B_NOPE + 2 * B_ROPE
W_REGROUP_ROWS = 32
MOD_TN = 1024

TOKEN_TM = 512
MERGE_TM = 256
GQA_TQ = 512
MLA_TQ = 2048
GQA_TK = 512
MLA_TK = 512
MOE_TM = 256
MOE_LEAD_TILES = 1
MOE_SLOTS = MOE_LEAD_TILES + 1


def _cparams(*sem):
    return pltpu.CompilerParams(dimension_semantics=sem, vmem_limit_bytes=V7X_VMEM_LIMIT_BYTES)


def _dot(a, b):
    return jnp.dot(a, b, preferred_element_type=F32)


def _dot_nt(a, b):
    return lax.dot_general(a, b, (((1,), (1,)), ((), ())), preferred_element_type=F32)


def _resident(shape):
    return pl.BlockSpec(shape, lambda *_: (0,) * len(shape), pipeline_mode=pl.Buffered(1))


def _rms(v, gain):
    return v * lax.rsqrt(jnp.mean(v * v, axis=-1, keepdims=True) + EPS) * gain


def _mod_kernel(c_ref, w_ref, b_ref, o_ref):
    c = c_ref[...]
    s = (c * jax.nn.sigmoid(c)).astype(BF16)
    o_ref[...] = _dot(s, w_ref[...].astype(BF16)) + b_ref[...]


def _modulation(cond, w_mod, b_mod):
    rows, d = cond.shape
    n = w_mod.shape[1]
    tn = MOD_TN
    return pl.pallas_call(
        _mod_kernel,
        grid=(n // tn,),
        in_specs=[pl.BlockSpec((rows, d), lambda j: (0, 0)),
                  pl.BlockSpec((d, tn), lambda j: (0, j)),
                  pl.BlockSpec((1, tn), lambda j: (0, j))],
        out_specs=pl.BlockSpec((rows, tn), lambda j: (0, j)),
        out_shape=jax.ShapeDtypeStruct((rows, n), F32),
        compiler_params=_cparams("parallel"),
        name="modulation",
    )(cond, w_mod, b_mod.reshape(1, n))


def _cast_rows_kernel(tbl_ref, src_ref, dst_ref):
    del tbl_ref
    dst_ref[...] = src_ref[...].astype(BF16)


def _cast_rows(blocks, table):
    blk = (1,) + blocks.shape[1:]
    grid_spec = pltpu.PrefetchScalarGridSpec(
        num_scalar_prefetch=1,
        grid=(table.shape[0],),
        in_specs=[pl.BlockSpec(blk, lambda i, tbl: (tbl[i], 0, 0))],
        out_specs=pl.BlockSpec(blk, lambda i, tbl: (i, 0, 0)))
    return pl.pallas_call(
        _cast_rows_kernel,
        grid_spec=grid_spec,
        out_shape=jax.ShapeDtypeStruct((table.shape[0],) + blocks.shape[1:], BF16),
        compiler_params=_cparams("parallel"),
        name="latent_key_weight",
    )(table, blocks)


def _swap_halves_64(v):
    lane = lax.broadcasted_iota(jnp.int32, v.shape, 1)
    return jnp.where((lane & 63) < 32, pltpu.roll(v, LANES - 32, 1), pltpu.roll(v, 32, 1))


def _pre_kernel(use_rope, n_side, tbl_ref, x_ref, sh_ref, sc_ref, g_ref, wck_ref, gkv_ref, wk_ref,
                wvt_ref, cb_ref, sb_ref, *refs):
    del tbl_ref
    side_src, (h_ref, kb_ref, vbt_ref), side_dst = refs[:n_side], refs[n_side:n_side + 3], refs[n_side + 3:]
    for k, src in enumerate(side_src):
        side_dst[0][k] = src[0].astype(BF16)
    xf = x_ref[0]
    h = _rms(xf, g_ref[...]) * (1.0 + sc_ref[0]) + sh_ref[0]
    hb = h.astype(BF16)
    h_ref[0] = hb
    p = _dot_nt(hb, wck_ref[...])
    cn = _rms(p[:, :B_KV_RANK], gkv_ref[...]).astype(BF16)
    kr2 = p[:, B_KV_RANK:]
    if use_rope:
        kr2 = kr2 * cb_ref[...] + _swap_halves_64(kr2) * sb_ref[...]
    kr2 = kr2.astype(BF16)
    kbn = _dot(cn, wk_ref[...]).astype(BF16)
    vbt_ref[0] = _dot_nt(wvt_ref[...], cn).astype(BF16).reshape(vbt_ref.shape[1:])
    for hd in range(B_HEADS):
        kb_ref[0, :, hd * B_KEY_W:hd * B_KEY_W + B_NOPE] = kbn[:, hd * B_NOPE:(hd + 1) * B_NOPE]
        kb_ref[0, :, hd * B_KEY_W + B_NOPE:(hd + 1) * B_KEY_W] = kr2


def _prologue(x, shift, scale, gain, w_ck, g_kv, w_k, w_vt, cos_b, sin_b, use_rope, tm,
              regroup=None):
    bt, st, d = x.shape
    n_i = st // tm
    row = lambda b, i, tbl: (b, i, 0)
    per_b = lambda b, i, tbl: (b, 0, 0)
    fixed = lambda b, i, tbl: (0, 0)
    tab = lambda b, i, tbl: (i, 0)
    side_in, side_out, side_shapes, side_args = [], [], [], []
    table = jnp.zeros((1,), jnp.int32)
    if regroup is not None:
        blocks, table = regroup
        per_step = table.shape[0] // (bt * n_i)
        assert per_step * bt * n_i == table.shape[0]
        for k in range(per_step):
            side_in.append(pl.BlockSpec(
                (1,) + blocks.shape[1:],
                lambda b, i, tbl, k=k: (tbl[(b * n_i + i) * per_step + k], 0, 0)))
            side_args.append(blocks)
        side_out.append(pl.BlockSpec((per_step,) + blocks.shape[1:],
                                     lambda b, i, tbl: (b * n_i + i, 0, 0)))
        side_shapes.append(jax.ShapeDtypeStruct((table.shape[0],) + blocks.shape[1:], BF16))
    grid_spec = pltpu.PrefetchScalarGridSpec(
        num_scalar_prefetch=1,
        grid=(bt, n_i),
        in_specs=[pl.BlockSpec((1, tm, d), row),
                  pl.BlockSpec((1, 1, d), per_b),
                  pl.BlockSpec((1, 1, d), per_b),
                  pl.BlockSpec((1, d), fixed),
                  _resident(w_ck.shape),
                  pl.BlockSpec((1, B_KV_RANK), fixed),
                  _resident(w_k.shape),
                  _resident(w_vt.shape),
                  pl.BlockSpec((tm, LANES), tab),
                  pl.BlockSpec((tm, LANES), tab),
                  *side_in],
        out_specs=[pl.BlockSpec((1, tm, d), row),
                   pl.BlockSpec((1, tm, B_HEADS * B_KEY_W), row),
                   pl.BlockSpec((1, B_HEADS, B_V, tm), lambda b, i, tbl: (b, 0, 0, i)),
                   *side_out],
    )
    h, kb, vbt, *side = pl.pallas_call(
        functools.partial(_pre_kernel, use_rope, len(side_in)),
        grid_spec=grid_spec,
        out_shape=[jax.ShapeDtypeStruct((bt, st, d), BF16),
                   jax.ShapeDtypeStruct((bt, st, B_HEADS * B_KEY_W), BF16),
                   jax.ShapeDtypeStruct((bt, B_HEADS, B_V, st), BF16),
                   *side_shapes],
        compiler_params=_cparams("parallel", "parallel"),
        name="prologue_rope" if use_rope else "prologue_ctx",
    )(table, x, shift, scale, gain, w_ck, g_kv, w_k, w_vt, cos_b, sin_b, *side_args)
    return h, kb, vbt, side


def _proj_kernel(use_rope, h_ref, w_ref, gq_ref, gk_ref, ca_ref, sa_ref, cb_ref, sb_ref,
                 o_ref, vat_ref):
    h = h_ref[0]
    n_blk = PROJ_TN // LANES

    def rope_a(v):
        if not use_rope:
            return v
        return v * ca_ref[...] + pltpu.roll(v, A_HEAD_DIM // 2, 1) * sa_ref[...]

    def rope_b(v):
        if not use_rope:
            return v
        return v * cb_ref[...] + _swap_halves_64(v) * sb_ref[...]

    for j in range(P_COLS // PROJ_TN):
        c0 = j * PROJ_TN
        acc = _dot_nt(h, w_ref[c0:c0 + PROJ_TN, :])

        def blk(k):
            return acc[:, k * LANES:(k + 1) * LANES]

        def put(k, v):
            o_ref[0, :, c0 + k * LANES:c0 + (k + 1) * LANES] = v.astype(BF16)

        if c0 < P_QA:
            o_ref[0, :, c0:c0 + PROJ_TN] = jax.nn.sigmoid(acc).astype(BF16)
        elif c0 < P_KA:
            for k in range(n_blk):
                put(k, rope_a(_rms(blk(k), gq_ref[...])) * A_SCORE_SCALE)
        elif c0 < P_QBN:
            for k in range(A_KV_HEADS):
                put(k, rope_a(_rms(blk(k), gk_ref[...])))
            for k in range(A_KV_HEADS, n_blk):
                put(k, blk(k))
                vat_ref[0, k - A_KV_HEADS] = blk(k).T.astype(BF16)
        elif c0 < P_QBR:
            o_ref[0, :, c0:c0 + PROJ_TN] = (acc * B_SCORE_SCALE).astype(BF16)
        else:
            for k in range(n_blk):
                put(k, rope_b(blk(k)) * B_SCORE_SCALE)


def _projection(h, w_p, g_q, g_k, cos_a, sin_a, cos_b, sin_b, use_rope, tm):
    bt, st, d = h.shape
    fixed = lambda b, i: (0, 0)
    tab = lambda b, i: (i, 0)
    return pl.pallas_call(
        functools.partial(_proj_kernel, use_rope),
        grid=(bt, st // tm),
        in_specs=[pl.BlockSpec((1, tm, d), lambda b, i: (b, i, 0)),
                  _resident(w_p.shape),
                  pl.BlockSpec((1, LANES), fixed),
                  pl.BlockSpec((1, LANES), fixed),
                  pl.BlockSpec((tm, LANES), tab),
                  pl.BlockSpec((tm, LANES), tab),
                  pl.BlockSpec((tm, LANES), tab),
                  pl.BlockSpec((tm, LANES), tab)],
        out_specs=[pl.BlockSpec((1, tm, P_COLS), lambda b, i: (b, i, 0)),
                   pl.BlockSpec((1, A_KV_HEADS, A_HEAD_DIM, tm), lambda b, i: (b, 0, 0, i))],
        out_shape=[jax.ShapeDtypeStruct((bt, st, P_COLS), BF16),
                   jax.ShapeDtypeStruct((bt, A_KV_HEADS, A_HEAD_DIM, st), BF16)],
        compiler_params=_cparams("parallel", "parallel"),
        name="projection_rope" if use_rope else "projection_ctx",
    )(h, w_p, g_q, g_k, cos_a, sin_a, cos_b, sin_b)


def _flash(q_ref, kc_ref, vct_ref, kl_ref, vlt_ref, st_ref, acc_ref, tk):
    n_chunks = kl_ref.shape[1] // tk

    def latent_scores(c, slot):
        st_ref[slot] = _dot_nt(kl_ref[0, c * tk:(c + 1) * tk, :], q_ref[...])

    def update(c, slot, m, l):
        st = st_ref[slot]
        m_new = jnp.maximum(m, jnp.max(st, axis=0, keepdims=True))
        alpha = jnp.exp2(m - m_new)
        pt = jnp.exp2(st - m_new)
        acc_ref[...] = alpha * acc_ref[...] + _dot(vlt_ref[0, 0, :, c * tk:(c + 1) * tk],
                                                  pt.astype(BF16))
        return m_new, alpha * l + jnp.sum(pt, axis=0, keepdims=True)

    latent_scores(0, 0)
    st = _dot_nt(kc_ref[0], q_ref[...])
    m = jnp.max(st, axis=0, keepdims=True)
    pt = jnp.exp2(st - m)
    l = jnp.sum(pt, axis=0, keepdims=True)
    acc_ref[...] = _dot(vct_ref[0, 0], pt.astype(BF16))

    for c in range(n_chunks):
        if c + 1 < n_chunks:
            latent_scores(c + 1, (c + 1) & 1)
        m, l = update(c, c & 1, m, l)
    return (acc_ref[...] / l).T


def _flash_scratch(m_rows, dk, dv, tk):
    return [pltpu.VMEM((m_rows, dk), BF16),
            pltpu.VMEM((2, tk, m_rows), F32),
            pltpu.VMEM((dv, m_rows), F32)]


class _SideCasts:
    def __init__(self, arrays, grid):
        self.n_steps = math.prod(grid)
        strides = [math.prod(grid[k + 1:]) for k in range(len(grid))]
        step = lambda *g: sum(i * st for i, st in zip(g, strides))
        self.shapes = [a.shape for a in arrays]
        self.views, self.in_specs, self.out_specs, self.out_shapes = [], [], [], []
        for a in arrays:
            cols = a.shape[-1]
            rows = math.prod(a.shape[:-1]) // self.n_steps
            assert rows * self.n_steps == math.prod(a.shape[:-1]) and rows % 16 == 0
            self.views.append(a.reshape(self.n_steps, rows, cols))
            spec = pl.BlockSpec((1, rows, cols), lambda *g: (step(*g), 0, 0))
            self.in_specs.append(spec)
            self.out_specs.append(spec)
            self.out_shapes.append(jax.ShapeDtypeStruct((self.n_steps, rows, cols), BF16))

    @staticmethod
    def run(src_refs, dst_refs):
        for src, dst in zip(src_refs, dst_refs):
            dst[...] = src[...].astype(BF16)

    def restore(self, outs):
        return [o.reshape(shape) for o, shape in zip(outs, self.shapes)]


def _gqa_kernel(tk, n_side, q_ref, kl_ref, vlt_ref, kc_ref, vct_ref, *refs):
    side_src, (o_ref, *side_dst) = refs[:n_side], refs[n_side:2 * n_side + 1]
    qs_ref, st_ref, acc_ref = refs[2 * n_side + 1:]
    _SideCasts.run(side_src, side_dst)
    group = A_HEADS // A_KV_HEADS
    tq = q_ref.shape[1]
    for g in range(group):
        qs_ref[g * tq:(g + 1) * tq, :] = q_ref[0, :, g * A_HEAD_DIM:(g + 1) * A_HEAD_DIM]
    out = _flash(qs_ref, kc_ref, vct_ref, kl_ref, vlt_ref, st_ref, acc_ref, tk)
    for g in range(group):
        o_ref[0, :, g * A_HEAD_DIM:(g + 1) * A_HEAD_DIM] = out[g * tq:(g + 1) * tq].astype(BF16)


def _gqa_attention(p_lat, vat_lat, p_ctx, vat_ctx, tq, tk, cast_arrays):
    b, s, _ = p_lat.shape
    n_ctx = p_ctx.shape[1]
    group_w = (A_HEADS // A_KV_HEADS) * A_HEAD_DIM
    grid = (b, A_KV_HEADS, s // tq)
    side = _SideCasts(cast_arrays, grid)
    oa, *cast = pl.pallas_call(
        functools.partial(_gqa_kernel, tk, len(cast_arrays)),
        grid=grid,
        in_specs=[pl.BlockSpec((1, tq, group_w), lambda bb, k, i: (bb, i, P_QA // group_w + k)),
                  pl.BlockSpec((1, s, LANES), lambda bb, k, i: (bb, 0, P_KA // LANES + k)),
                  pl.BlockSpec((1, 1, A_HEAD_DIM, s), lambda bb, k, i: (bb, k, 0, 0)),
                  pl.BlockSpec((1, n_ctx, LANES), lambda bb, k, i: (bb, 0, P_KA // LANES + k)),
                  pl.BlockSpec((1, 1, A_HEAD_DIM, n_ctx), lambda bb, k, i: (bb, k, 0, 0)),
                  *side.in_specs],
        out_specs=[pl.BlockSpec((1, tq, group_w), lambda bb, k, i: (bb, i, k)), *side.out_specs],
        out_shape=[jax.ShapeDtypeStruct((b, s, A_HEADS * A_HEAD_DIM), BF16), *side.out_shapes],
        scratch_shapes=_flash_scratch((A_HEADS // A_KV_HEADS) * tq, A_HEAD_DIM, A_HEAD_DIM, tk),
        compiler_params=_cparams("parallel", "parallel", "parallel"),
        name="gqa_attention",
    )(p_lat, p_lat, vat_lat, p_ctx, vat_ctx, *side.views)
    return oa, side.restore(cast)


def _mla_kernel(tk, n_side, qn_ref, qr_ref, kl_ref, vlt_ref, kc_ref, vct_ref, *refs):
    side_src, (o_ref, *side_dst) = refs[:n_side], refs[n_side:2 * n_side + 1]
    qs_ref, st_ref, acc_ref = refs[2 * n_side + 1:]
    _SideCasts.run(side_src, side_dst)
    hd = pl.program_id(1)
    qr = qr_ref[0]
    lane = lax.broadcasted_iota(jnp.int32, qr.shape, 1)
    qs_ref[:, :B_NOPE] = qn_ref[0]
    qs_ref[:, B_NOPE:] = jnp.where((lane >> 6) == (hd & 1), qr, jnp.zeros_like(qr))
    o_ref[0] = _flash(qs_ref, kc_ref, vct_ref, kl_ref, vlt_ref, st_ref, acc_ref, tk).astype(BF16)


def _mla_attention(p_lat, kb_lat, vbt_lat, kb_ctx, vbt_ctx, tq, tk, cast_arrays):
    b, s, _ = p_lat.shape
    n_ctx = kb_ctx.shape[1]
    grid = (b, B_HEADS, s // tq)
    side = _SideCasts(cast_arrays, grid)
    ob, *cast = pl.pallas_call(
        functools.partial(_mla_kernel, tk, len(cast_arrays)),
        grid=grid,
        in_specs=[pl.BlockSpec((1, tq, LANES), lambda bb, h, i: (bb, i, P_QBN // LANES + h)),
                  pl.BlockSpec((1, tq, LANES), lambda bb, h, i: (bb, i, P_QBR // LANES + h // 2)),
                  pl.BlockSpec((1, s, B_KEY_W), lambda bb, h, i: (bb, 0, h)),
                  pl.BlockSpec((1, 1, B_V, s), lambda bb, h, i: (bb, h, 0, 0)),
                  pl.BlockSpec((1, n_ctx, B_KEY_W), lambda bb, h, i: (bb, 0, h)),
                  pl.BlockSpec((1, 1, B_V, n_ctx), lambda bb, h, i: (bb, h, 0, 0)),
                  *side.in_specs],
        out_specs=[pl.BlockSpec((1, tq, B_V), lambda bb, h, i: (bb, i, h)), *side.out_specs],
        out_shape=[jax.ShapeDtypeStruct((b, s, B_HEADS * B_V), BF16), *side.out_shapes],
        scratch_shapes=_flash_scratch(tq, B_KEY_W, B_V, tk),
        compiler_params=_cparams("parallel", "parallel", "parallel"),
        name="mla_attention",
    )(p_lat, p_lat, kb_lat, vbt_lat, kb_ctx, vbt_ctx, *side.views)
    return ob, side.restore(cast)


def _merge_kernel(oa_ref, ob_ref, ga_ref, gb_ref, x_ref, g1_ref, sh_ref, sc_ref, gn_ref,
                  wa_ref, wb_ref, wo_ref, wr_ref, br_ref, xn_ref, h2_ref, rt_ref, rtt_ref, cnt_ref):
    ya = _dot(oa_ref[0], wa_ref[...])
    yb = _dot(ob_ref[0], wb_ref[...])
    mix = (ga_ref[0].astype(F32) * ya + gb_ref[0].astype(F32) * yb).astype(BF16)
    xn = x_ref[0] + g1_ref[0] * _dot(mix, wo_ref[...])
    xn_ref[0] = xn
    h2 = _rms(xn, gn_ref[...]) * (1.0 + sc_ref[0]) + sh_ref[0]
    h2_ref[0] = h2
    logits = _dot(h2.astype(BF16), wr_ref[...]) + br_ref[...]
    lt = logits.T
    gl = [lt[g:g + 1, :] for g in range(N_GROUPS)]
    gmax = functools.reduce(jnp.maximum, gl)
    gsum = functools.reduce(lambda a, b_: a + b_, [jnp.exp(v - gmax) for v in gl])
    g_val = 1.0 / gsum
    g_idx = jnp.full(gmax.shape, N_GROUPS - 1, jnp.int32)
    for g in range(N_GROUPS - 2, -1, -1):
        g_idx = jnp.where(gl[g] == gmax, g, g_idx)
    el = []
    for e in range(EXPERTS_PER_GROUP):
        v = lt[N_GROUPS + e:N_GROUPS + e + 1, :]
        for g in range(1, N_GROUPS):
            row = N_GROUPS + g * EXPERTS_PER_GROUP + e
            v = jnp.where(g_idx == g, lt[row:row + 1, :], v)
        el.append(v)
    emax = functools.reduce(jnp.maximum, el)
    i1 = jnp.full(emax.shape, EXPERTS_PER_GROUP - 1, jnp.int32)
    for e in range(EXPERTS_PER_GROUP - 2, -1, -1):
        i1 = jnp.where(el[e] == emax, e, i1)
    neg = jnp.full(emax.shape, -jnp.inf, F32)
    el2 = [jnp.where(i1 == e, neg, el[e]) for e in range(EXPERTS_PER_GROUP)]
    emax2 = functools.reduce(jnp.maximum, el2)
    i2 = jnp.full(emax.shape, EXPERTS_PER_GROUP - 1, jnp.int32)
    for e in range(EXPERTS_PER_GROUP - 2, -1, -1):
        i2 = jnp.where(el2[e] == emax2, e, i2)
    p2 = jnp.exp(emax2 - emax)
    w1 = g_val / (1.0 + p2)
    w2 = g_val * p2 / (1.0 + p2)
    e1 = g_idx * EXPERTS_PER_GROUP + i1
    e2 = g_idx * EXPERTS_PER_GROUP + i2
    zero = jnp.zeros_like(w1)
    rt = jnp.concatenate([e1.astype(F32), e2.astype(F32), w1, w2, zero, zero, zero, zero], axis=0)
    rt_ref[...] = rt
    rtt_ref[...] = jnp.concatenate([rt, jnp.zeros((LANES - SUBLANES, rt.shape[1]), F32)], axis=0).T
    lane = lax.broadcasted_iota(jnp.int32, (1, LANES), 1)
    counts = jnp.zeros((1, LANES), F32)
    for e in range(N_EXPERTS):
        hits = jnp.where(e1 == e, 1.0, 0.0) + jnp.where(e2 == e, 1.0, 0.0)
        counts = counts + jnp.where(lane == e, jnp.sum(hits, axis=1, keepdims=True), 0.0)
    cnt_ref[0] = jnp.broadcast_to(counts, (SUBLANES, LANES))


def _merge_route(oa, ob, p_lat, x, g1, sh2, sc2, g_ffn, w_a, w_b, w_o, w_r, b_r, tm):
    b, s, d = x.shape
    n_i = s // tm
    row = lambda bb, i: (bb, i, 0)
    per_b = lambda bb, i: (bb, 0, 0)
    fixed = lambda bb, i: (0, 0)
    return pl.pallas_call(
        _merge_kernel,
        grid=(b, n_i),
        in_specs=[pl.BlockSpec((1, tm, oa.shape[2]), row),
                  pl.BlockSpec((1, tm, ob.shape[2]), row),
                  pl.BlockSpec((1, tm, d), lambda bb, i: (bb, i, P_GA // d)),
                  pl.BlockSpec((1, tm, d), lambda bb, i: (bb, i, P_GB // d)),
                  pl.BlockSpec((1, tm, d), row),
                  pl.BlockSpec((1, 1, d), per_b),
                  pl.BlockSpec((1, 1, d), per_b),
                  pl.BlockSpec((1, 1, d), per_b),
                  pl.BlockSpec((1, d), fixed),
                  _resident(w_a.shape),
                  _resident(w_b.shape),
                  _resident(w_o.shape),
                  _resident(w_r.shape),
                  pl.BlockSpec((1, LANES), fixed)],
        out_specs=[pl.BlockSpec((1, tm, d), row),
                   pl.BlockSpec((1, tm, d), row),
                   pl.BlockSpec((SUBLANES, tm), lambda bb, i: (0, bb * n_i + i)),
                   pl.BlockSpec((tm, LANES), lambda bb, i: (bb * n_i + i, 0)),
                   pl.BlockSpec((1, SUBLANES, LANES), lambda bb, i: (bb * n_i + i, 0, 0))],
        out_shape=[jax.ShapeDtypeStruct((b, s, d), F32),
                   jax.ShapeDtypeStruct((b, s, d), F32),
                   jax.ShapeDtypeStruct((SUBLANES, b * s), F32),
                   jax.ShapeDtypeStruct((b * s, LANES), F32),
                   jax.ShapeDtypeStruct((b * n_i, SUBLANES, LANES), F32)],
        compiler_params=_cparams("parallel", "parallel"),
        name="merge_route",
    )(oa, ob, p_lat, p_lat, x, g1, sh2, sc2, g_ffn, w_a, w_b, w_o, w_r, b_r)


def _moe_kernel(tm, na_ref, te_ref, src_ref, dst_ref, h_hbm, wg_ref, wu_ref, wd_ref,
                y_hbm, xbuf, ybuf, gsem, ssem):
    t = pl.program_id(0)
    n_active = na_ref[0]
    phase = lax.rem(t, MOE_SLOTS)

    def gather_start(tile, slot):
        for r in range(tm):
            pltpu.make_async_copy(h_hbm.at[pl.ds(src_ref[tile * tm + r], 1)],
                                  xbuf.at[slot, pl.ds(r, 1)], gsem.at[slot]).start()

    def gather_wait(slot):
        pltpu.make_async_copy(h_hbm.at[pl.ds(0, tm)], xbuf.at[slot], gsem.at[slot]).wait()

    def scatter_start(tile, slot):
        for r in range(tm):
            pltpu.make_async_copy(ybuf.at[slot, pl.ds(r, 1)],
                                  y_hbm.at[pl.ds(dst_ref[(tile + MOE_LEAD_TILES) * tm + r], 1)],
                                  ssem.at[slot]).start()

    def scatter_wait(slot):
        pltpu.make_async_copy(ybuf.at[slot], y_hbm.at[pl.ds(0, tm)], ssem.at[slot]).wait()

    @pl.when(t == 0)
    def _():
        for k in range(1, MOE_LEAD_TILES + 1):
            ybuf[(-k) % MOE_SLOTS] = jnp.zeros(ybuf.shape[1:], ybuf.dtype)
        for k in range(2, MOE_LEAD_TILES + 1):
            scatter_start(-k, (-k) % MOE_SLOTS)
        gather_start(0, 0)

    def step(cur):
        nxt = (cur + 1) % MOE_SLOTS
        gather_wait(cur)
        gate = _dot(xbuf[cur].astype(BF16), wg_ref[0])
        gather_start(t + 1, nxt)
        up = _dot(xbuf[cur].astype(BF16), wu_ref[0])
        scatter_start(t - 1, (cur - 1) % MOE_SLOTS)
        hid = (gate * jax.nn.sigmoid(gate) * up).astype(BF16)
        ybuf[cur] = _dot(hid, wd_ref[0])
        scatter_wait((cur - MOE_LEAD_TILES) % MOE_SLOTS)

    def drain(cur):
        scatter_start(t, cur)
        for k in range(MOE_LEAD_TILES - 1, -1, -1):
            scatter_wait((cur - k) % MOE_SLOTS)
        gather_wait((cur + 1) % MOE_SLOTS)

    for cur in range(MOE_SLOTS):
        mine = phase == cur
        pl.when(mine & (t < n_active))(functools.partial(step, cur))
        pl.when(mine & (t == n_active - 1))(functools.partial(drain, cur))


def _moe(h2, n_active, tile_expert, src_tok, dst_row, w_gate, w_up, w_down, tm):
    n, d = h2.shape
    n_tiles = tile_expert.shape[0]
    w_in_map = lambda t, na, te, sr, ds: (te[t], 0, 0)
    grid_spec = pltpu.PrefetchScalarGridSpec(
        num_scalar_prefetch=4,
        grid=(n_tiles,),
        in_specs=[pl.BlockSpec(memory_space=pl.ANY),
                  pl.BlockSpec((1, d, D_EXPERT), w_in_map),
                  pl.BlockSpec((1, d, D_EXPERT), w_in_map),
                  pl.BlockSpec((1, D_EXPERT, d), w_in_map)],
        out_specs=pl.BlockSpec(memory_space=pl.ANY),
        scratch_shapes=[pltpu.VMEM((MOE_SLOTS, tm, d), F32),
                        pltpu.VMEM((MOE_SLOTS, tm, d), F32),
                        pltpu.SemaphoreType.DMA((MOE_SLOTS,)),
                        pltpu.SemaphoreType.DMA((MOE_SLOTS,))],
    )
    return pl.pallas_call(
        functools.partial(_moe_kernel, tm),
        grid_spec=grid_spec,
        out_shape=jax.ShapeDtypeStruct((2 * n + MOE_LEAD_TILES * tm, d), F32),
        compiler_params=_cparams("arbitrary"),
        name="moe_experts",
    )(n_active, tile_expert, src_tok, dst_row, h2, w_gate, w_up, w_down)


def _moe_plan(eid, counts, tm):
    n = eid.shape[1]
    pairs = 2 * n
    n_tiles = pairs // tm + N_EXPERTS
    _, order = lax.sort((eid.reshape(pairs), lax.iota(jnp.int32, pairs)), num_keys=1)
    padded = ((counts + tm - 1) // tm) * tm
    pad_end = jnp.cumsum(padded)
    pad_start = pad_end - padded
    raw_start = jnp.cumsum(counts) - counts
    tile_row0 = jnp.arange(n_tiles, dtype=jnp.int32) * tm
    tile_expert = jnp.minimum(
        jnp.sum(tile_row0[:, None] >= pad_end[None, :], axis=1), N_EXPERTS - 1).astype(jnp.int32)
    shift = (raw_start - pad_start)[tile_expert]
    limit = (pad_start + counts)[tile_expert]
    pos = jnp.arange(n_tiles * tm, dtype=jnp.int32).reshape(n_tiles, tm)
    valid = pos < limit[:, None]
    pair = jnp.where(valid, order[jnp.clip(pos + shift[:, None], 0, pairs - 1)], 0).reshape(-1)
    valid = valid.reshape(-1)
    dummy = pairs + jnp.arange(MOE_LEAD_TILES * tm, dtype=jnp.int32)
    pad_dst = (pairs + ((pos // tm) % MOE_LEAD_TILES) * tm + pos % tm).reshape(-1)
    src_tok = jnp.concatenate([pair % n, jnp.zeros((tm,), jnp.int32)]).astype(jnp.int32)
    dst_row = jnp.concatenate([dummy, jnp.where(valid, pair, pad_dst)]).astype(jnp.int32)
    n_active = (pad_end[-1:] // tm).astype(jnp.int32)
    return n_active, tile_expert, src_tok, dst_row


def _final_kernel(x_ref, y0_ref, y1_ref, rtt_ref, g2_ref, gn_ref, o_ref):
    wt = rtt_ref[...]
    moe = wt[:, 2:3] * y0_ref[...] + wt[:, 3:4] * y1_ref[...]
    o_ref[0] = _rms(x_ref[0] + g2_ref[0] * moe, gn_ref[...])


def _final(x_new, y2, route, g2, g_final, tm):
    b, s, d = x_new.shape
    row = lambda bb, i: (bb, i, 0)
    n_i = s // tm
    return pl.pallas_call(
        _final_kernel,
        grid=(b, n_i),
        in_specs=[pl.BlockSpec((1, tm, d), row),
                  pl.BlockSpec((tm, d), lambda bb, i: (bb * n_i + i, 0)),
                  pl.BlockSpec((tm, d), lambda bb, i: (b * n_i + bb * n_i + i, 0)),
                  pl.BlockSpec((tm, LANES), lambda bb, i: (bb * n_i + i, 0)),
                  pl.BlockSpec((1, 1, d), lambda bb, i: (bb, 0, 0)),
                  pl.BlockSpec((1, d), lambda bb, i: (0, 0))],
        out_specs=pl.BlockSpec((1, tm, d), row),
        out_shape=jax.ShapeDtypeStruct((b, s, d), F32),
        compiler_params=_cparams("parallel", "parallel"),
        name="final_norm",
    )(x_new, y2, y2, route, g2, g_final)


def _rope_tables(n_tokens, rot_dim):
    rows = n_tokens // GRID_W
    row = np.repeat(np.arange(rows), GRID_W).astype(np.float64)
    col = np.tile(np.arange(GRID_W), rows).astype(np.float64)
    n_freq = rot_dim // 4
    freqs = ROPE_THETA ** (-np.arange(n_freq, dtype=np.float64) / n_freq)
    ang = np.concatenate([row[:, None] * freqs, col[:, None] * freqs], axis=-1)
    cos, sin = np.cos(ang), np.sin(ang)
    reps = LANES // rot_dim
    cos_t = np.tile(np.concatenate([cos, cos], axis=-1), (1, reps))
    sin_t = np.tile(np.concatenate([-sin, sin], axis=-1), (1, reps))
    return jnp.asarray(cos_t, F32), jnp.asarray(sin_t, F32)


def kernel(x, c, ctx, c_ctx, w_mod, b_mod, norm_mix, norm_ffn, w_in, a_q_norm, a_k_norm, b_kv_norm, w_ukv, w_br_a, w_br_b, w_out, w_group, b_group, w_router, b_router, w_e_gate, w_e_up, w_e_down, norm_final):
    b, s, d = x.shape
    n_ctx = ctx.shape[1]
    assert w_mod.shape[0] == 1, "single-layer block"
    assert s % GRID_W == 0

    wt = jnp.transpose(w_in[0])
    qa_w = A_HEADS * A_HEAD_DIM
    kv_w = A_KV_HEADS * A_HEAD_DIM
    o_qb = qa_w + 2 * kv_w
    qb_w = B_HEADS * (B_NOPE + B_ROPE)
    o_ckv = o_qb + qb_w
    o_kr = o_ckv + B_KV_RANK
    o_gl = o_kr + B_ROPE
    qb_rows = o_qb + (B_NOPE + B_ROPE) * np.arange(B_HEADS)[:, None]
    p_rows = np.concatenate(
        [np.arange(o_gl, wt.shape[0]), np.arange(o_qb),
         (qb_rows + np.arange(B_NOPE)).reshape(-1),
         (qb_rows + B_NOPE + np.arange(B_ROPE)).reshape(-1)])
    p_blocks = p_rows.reshape(-1, W_REGROUP_ROWS)
    assert (np.diff(p_blocks, axis=1) == 1).all() and (p_blocks[:, 0] % W_REGROUP_ROWS == 0).all()
    w_blocks = wt.reshape(-1, W_REGROUP_ROWS, d)
    regroup = (w_blocks, jnp.asarray(p_blocks[:, 0] // W_REGROUP_ROWS, jnp.int32))
    ck_blocks = np.concatenate([np.arange(o_ckv, o_gl), np.arange(o_kr, o_gl)])[::W_REGROUP_ROWS]
    w_ck = _cast_rows(w_blocks, jnp.asarray(ck_blocks // W_REGROUP_ROWS, jnp.int32))
    w_ck = w_ck.reshape(-1, d)
    w_kv = w_ukv[0].reshape(B_KV_RANK, B_HEADS, B_NOPE + B_V)
    w_k = w_kv[:, :, :B_NOPE].reshape(B_KV_RANK, B_HEADS * B_NOPE).astype(BF16)
    w_vt = w_kv[:, :, B_NOPE:].reshape(B_KV_RANK, B_HEADS * B_V).T.astype(BF16)
    w_r = jnp.concatenate(
        [w_group[0], jnp.transpose(w_router[0], (1, 0, 2)).reshape(d, N_EXPERTS),
         jnp.zeros((d, LANES - N_GROUPS - N_EXPERTS), F32)], axis=1).astype(BF16)
    b_r = jnp.concatenate([b_group[0], b_router[0].reshape(N_EXPERTS),
                           jnp.zeros((LANES - N_GROUPS - N_EXPERTS,), F32)])[None, :]

    cond = jnp.concatenate([c, c_ctx[None, :], jnp.zeros((SUBLANES - b - 1, d), F32)], axis=0)
    mod = _modulation(cond, w_mod[0], b_mod[0])
    mx = mod[:b].reshape(b, N_MOD, 1, d)
    sh1, sc1, g1, sh2, sc2, g2 = [mx[:, k] for k in range(N_MOD)]
    mc = jnp.broadcast_to(mod[b].reshape(N_MOD, 1, 1, d), (N_MOD, b, 1, d))
    csh1, csc1 = mc[0], mc[1]

    cos_a, sin_a = _rope_tables(s, A_HEAD_DIM)
    cos_b, sin_b = _rope_tables(s, B_ROPE)
    g_mix = norm_mix[0][None, :]
    g_kv = b_kv_norm[0][None, :]
    g_q = a_q_norm[0][None, :]
    g_k = a_k_norm[0][None, :]

    tm = min(TOKEN_TM, s)
    h, kb_lat, vbt_lat, (w_p,) = _prologue(x, sh1, sc1, g_mix, w_ck, g_kv, w_k, w_vt, cos_b, sin_b,
                                           True, tm, regroup)
    w_p = w_p.reshape(P_COLS, d)
    p_lat, vat_lat = _projection(h, w_p, g_q, g_k, cos_a, sin_a, cos_b, sin_b, True, tm)

    tab_c = jnp.zeros((n_ctx, LANES), F32)
    hc, kb_ctx, vbt_ctx, _ = _prologue(ctx, csh1, csc1, g_mix, w_ck, g_kv, w_k, w_vt, tab_c, tab_c,
                                       False, n_ctx)
    p_ctx, vat_ctx = _projection(hc, w_p, g_q, g_k, tab_c, tab_c, tab_c, tab_c, False, n_ctx)

    oa, (w_down, w_a, w_b, w_o) = _gqa_attention(
        p_lat, vat_lat, p_ctx, vat_ctx, min(GQA_TQ, s), min(GQA_TK, s),
        [w_e_down[0], w_br_a[0], w_br_b[0], w_out[0]])
    ob, (w_gate, w_up) = _mla_attention(p_lat, kb_lat, vbt_lat, kb_ctx, vbt_ctx, min(MLA_TQ, s),
                                        min(MLA_TK, s), [w_e_gate[0], w_e_up[0]])

    x_new, h2, route, route_t, cnt = _merge_route(
        oa, ob, p_lat, x, g1, sh2, sc2, norm_ffn[0][None, :],
        w_a, w_b, w_o, w_r, b_r,
        min(MERGE_TM, s))

    n = b * s
    counts = jnp.sum(cnt[:, 0, :N_EXPERTS], axis=0).astype(jnp.int32)
    plan = _moe_plan(route[0:2].astype(jnp.int32), counts, MOE_TM)
    y2 = _moe(h2.reshape(n, d), *plan, w_gate, w_up, w_down, MOE_TM)

    return _final(x_new, y2, route_t, g2, norm_final[None, :], tm)
```

```python
import functools
import math

import jax
import jax.numpy as jnp
import numpy as np
from jax import lax
from jax.experimental import pallas as pl
from jax.experimental.pallas import tpu as pltpu

GRID_W = 64
ROPE_THETA = 10000.0
EPS = 1e-6
A_HEADS = 8
A_KV_HEADS = 2
A_HEAD_DIM = 128
B_HEADS = 8
B_NOPE = 128
B_ROPE = 64
B_V = 128
B_KV_RANK = 512
N_GROUPS = 4
EXPERTS_PER_GROUP = 4
N_EXPERTS = N_GROUPS * EXPERTS_PER_GROUP
D_EXPERT = 1024
N_MOD = 6

LANES = 128
SUBLANES = 8
V7X_VMEM_LIMIT_BYTES = 56 * 1024 * 1024

BF16 = jnp.bfloat16
F32 = jnp.float32

P_GA = 0
P_GB = 2048
P_QA = 4096
P_KA = 5120
P_VA = 5376
P_QBN = 5632
P_QBR = 6656
P_COLS = 7168
PROJ_TN = 512

LOG2_E = math.log2(math.e)
A_SCORE_SCALE = LOG2_E / math.sqrt(A_HEAD_DIM)
B_SCORE_SCALE = LOG2_E / math.sqrt(B_NOPE + B_ROPE)

B_KEY_W = B_NOPE + 2 * B_ROPE
W_REGROUP_ROWS = 32
MOD_TN = 1024

TOKEN_TM = 512
MERGE_TM = 256
GQA_TQ = 512
MLA_TQ = 2048
GQA_TK = 512
MLA_TK = 512
MOE_TM = 256
MOE_LEAD_TILES = 1
MOE_SLOTS = MOE_LEAD_TILES + 1


def _cparams(*sem):
    return pltpu.CompilerParams(dimension_semantics=sem, vmem_limit_bytes=V7X_VMEM_LIMIT_BYTES)


def _dot(a, b):
    return jnp.dot(a, b, preferred_element_type=F32)


def _dot_nt(a, b):
    return lax.dot_general(a, b, (((1,), (1,)), ((), ())), preferred_element_type=F32)


def _resident(shape):
    return pl.BlockSpec(shape, lambda *_: (0,) * len(shape), pipeline_mode=pl.Buffered(1))


def _rms(v, gain):
    return v * lax.rsqrt(jnp.mean(v * v, axis=-1, keepdims=True) + EPS) * gain


def _mod_kernel(c_ref, w_ref, b_ref, o_ref):
    c = c_ref[...]
    s = (c * jax.nn.sigmoid(c)).astype(BF16)
    o_ref[...] = _dot(s, w_ref[...].astype(BF16)) + b_ref[...]


def _modulation(cond, w_mod, b_mod):
    rows, d = cond.shape
    n = w_mod.shape[1]
    tn = MOD_TN
    return pl.pallas_call(
        _mod_kernel,
        grid=(n // tn,),
        in_specs=[pl.BlockSpec((rows, d), lambda j: (0, 0)),
                  pl.BlockSpec((d, tn), lambda j: (0, j)),
                  pl.BlockSpec((1, tn), lambda j: (0, j))],
        out_specs=pl.BlockSpec((rows, tn), lambda j: (0, j)),
        out_shape=jax.ShapeDtypeStruct((rows, n), F32),
        compiler_params=_cparams("parallel"),
        name="modulation",
    )(cond, w_mod, b_mod.reshape(1, n))


def _cast_rows_kernel(tbl_ref, src_ref, dst_ref):
    del tbl_ref
    dst_ref[...] = src_ref[...].astype(BF16)


def _cast_rows(blocks, table):
    blk = (1,) + blocks.shape[1:]
    grid_spec = pltpu.PrefetchScalarGridSpec(
        num_scalar_prefetch=1,
        grid=(table.shape[0],),
        in_specs=[pl.BlockSpec(blk, lambda i, tbl: (tbl[i], 0, 0))],
        out_specs=pl.BlockSpec(blk, lambda i, tbl: (i, 0, 0)))
    return pl.pallas_call(
        _cast_rows_kernel,
        grid_spec=grid_spec,
        out_shape=jax.ShapeDtypeStruct((table.shape[0],) + blocks.shape[1:], BF16),
        compiler_params=_cparams("parallel"),
        name="latent_key_weight",
    )(table, blocks)


def _swap_halves_64(v):
    lane = lax.broadcasted_iota(jnp.int32, v.shape, 1)
    return jnp.where((lane & 63) < 32, pltpu.roll(v, LANES - 32, 1), pltpu.roll(v, 32, 1))


def _pre_kernel(use_rope, n_side, tbl_ref, x_ref, sh_ref, sc_ref, g_ref, wck_ref, gkv_ref, wk_ref,
                wvt_ref, cb_ref, sb_ref, *refs):
    del tbl_ref
    side_src, (h_ref, kb_ref, vbt_ref), side_dst = refs[:n_side], refs[n_side:n_side + 3], refs[n_side + 3:]
    for k, src in enumerate(side_src):
        side_dst[0][k] = src[0].astype(BF16)
    xf = x_ref[0]
    h = _rms(xf, g_ref[...]) * (1.0 + sc_ref[0]) + sh_ref[0]
    hb = h.astype(BF16)
    h_ref[0] = hb
    p = _dot_nt(hb, wck_ref[...])
    cn = _rms(p[:, :B_KV_RANK], gkv_ref[...]).astype(BF16)
    kr2 = p[:, B_KV_RANK:]
    if use_rope:
        kr2 = kr2 * cb_ref[...] + _swap_halves_64(kr2) * sb_ref[...]
    kr2 = kr2.astype(BF16)
    kbn = _dot(cn, wk_ref[...]).astype(BF16)
    vbt_ref[0] = _dot_nt(wvt_ref[...], cn).astype(BF16).reshape(vbt_ref.shape[1:])
    for hd in range(B_HEADS):
        kb_ref[0, :, hd * B_KEY_W:hd * B_KEY_W + B_NOPE] = kbn[:, hd * B_NOPE:(hd + 1) * B_NOPE]
        kb_ref[0, :, hd * B_KEY_W + B_NOPE:(hd + 1) * B_KEY_W] = kr2


def _prologue(x, shift, scale, gain, w_ck, g_kv, w_k, w_vt, cos_b, sin_b, use_rope, tm,
              regroup=None):
    bt, st, d = x.shape
    n_i = st // tm
    row = lambda b, i, tbl: (b, i, 0)
    per_b = lambda b, i, tbl: (b, 0, 0)
    fixed = lambda b, i, tbl: (0, 0)
    tab = lambda b, i, tbl: (i, 0)
    side_in, side_out, side_shapes, side_args = [], [], [], []
    table = jnp.zeros((1,), jnp.int32)
    if regroup is not None:
        blocks, table = regroup
        per_step = table.shape[0] // (bt * n_i)
        assert per_step * bt * n_i == table.shape[0]
        for k in range(per_step):
            side_in.append(pl.BlockSpec(
                (1,) + blocks.shape[1:],
                lambda b, i, tbl, k=k: (tbl[(b * n_i + i) * per_step + k], 0, 0)))
            side_args.append(blocks)
        side_out.append(pl.BlockSpec((per_step,) + blocks.shape[1:],
                                     lambda b, i, tbl: (b * n_i + i, 0, 0)))
        side_shapes.append(jax.ShapeDtypeStruct((table.shape[0],) + blocks.shape[1:], BF16))
    grid_spec = pltpu.PrefetchScalarGridSpec(
        num_scalar_prefetch=1,
        grid=(bt, n_i),
        in_specs=[pl.BlockSpec((1, tm, d), row),
                  pl.BlockSpec((1, 1, d), per_b),
                  pl.BlockSpec((1, 1, d), per_b),
                  pl.BlockSpec((1, d), fixed),
                  _resident(w_ck.shape),
                  pl.BlockSpec((1, B_KV_RANK), fixed),
                  _resident(w_k.shape),
                  _resident(w_vt.shape),
                  pl.BlockSpec((tm, LANES), tab),
                  pl.BlockSpec((tm, LANES), tab),
                  *side_in],
        out_specs=[pl.BlockSpec((1, tm, d), row),
                   pl.BlockSpec((1, tm, B_HEADS * B_KEY_W), row),
                   pl.BlockSpec((1, B_HEADS, B_V, tm), lambda b, i, tbl: (b, 0, 0, i)),
                   *side_out],
    )
    h, kb, vbt, *side = pl.pallas_call(
        functools.partial(_pre_kernel, use_rope, len(side_in)),
        grid_spec=grid_spec,
        out_shape=[jax.ShapeDtypeStruct((bt, st, d), BF16),
                   jax.ShapeDtypeStruct((bt, st, B_HEADS * B_KEY_W), BF16),
                   jax.ShapeDtypeStruct((bt, B_HEADS, B_V, st), BF16),
                   *side_shapes],
        compiler_params=_cparams("parallel", "parallel"),
        name="prologue_rope" if use_rope else "prologue_ctx",
    )(table, x, shift, scale, gain, w_ck, g_kv, w_k, w_vt, cos_b, sin_b, *side_args)
    return h, kb, vbt, side


def _proj_kernel(use_rope, h_ref, w_ref, gq_ref, gk_ref, ca_ref, sa_ref, cb_ref, sb_ref,
                 o_ref, vat_ref):
    h = h_ref[0]
    n_blk = PROJ_TN // LANES

    def rope_a(v):
        if not use_rope:
            return v
        return v * ca_ref[...] + pltpu.roll(v, A_HEAD_DIM // 2, 1) * sa_ref[...]

    def rope_b(v):
        if not use_rope:
            return v
        return v * cb_ref[...] + _swap_halves_64(v) * sb_ref[...]

    for j in range(P_COLS // PROJ_TN):
        c0 = j * PROJ_TN
        acc = _dot_nt(h, w_ref[c0:c0 + PROJ_TN, :])

        def blk(k):
            return acc[:, k * LANES:(k + 1) * LANES]

        def put(k, v):
            o_ref[0, :, c0 + k * LANES:c0 + (k + 1) * LANES] = v.astype(BF16)

        if c0 < P_QA:
            o_ref[0, :, c0:c0 + PROJ_TN] = jax.nn.sigmoid(acc).astype(BF16)
        elif c0 < P_KA:
            for k in range(n_blk):
                put(k, rope_a(_rms(blk(k), gq_ref[...])) * A_SCORE_SCALE)
        elif c0 < P_QBN:
            for k in range(A_KV_HEADS):
                put(k, rope_a(_rms(blk(k), gk_ref[...])))
            for k in range(A_KV_HEADS, n_blk):
                put(k, blk(k))
                vat_ref[0, k - A_KV_HEADS] = blk(k).T.astype(BF16)
        elif c0 < P_QBR:
            o_ref[0, :, c0:c0 + PROJ_TN] = (acc * B_SCORE_SCALE).astype(BF16)
        else:
            for k in range(n_blk):
                put(k, rope_b(blk(k)) * B_SCORE_SCALE)


def _projection(h, w_p, g_q, g_k, cos_a, sin_a, cos_b, sin_b, use_rope, tm):
    bt, st, d = h.shape
    fixed = lambda b, i: (0, 0)
    tab = lambda b, i: (i, 0)
    return pl.pallas_call(
        functools.partial(_proj_kernel, use_rope),
        grid=(bt, st // tm),
        in_specs=[pl.BlockSpec((1, tm, d), lambda b, i: (b, i, 0)),
                  _resident(w_p.shape),
                  pl.BlockSpec((1, LANES), fixed),
                  pl.BlockSpec((1, LANES), fixed),
                  pl.BlockSpec((tm, LANES), tab),
                  pl.BlockSpec((tm, LANES), tab),
                  pl.BlockSpec((tm, LANES), tab),
                  pl.BlockSpec((tm, LANES), tab)],
        out_specs=[pl.BlockSpec((1, tm, P_COLS), lambda b, i: (b, i, 0)),
                   pl.BlockSpec((1, A_KV_HEADS, A_HEAD_DIM, tm), lambda b, i: (b, 0, 0, i))],
        out_shape=[jax.ShapeDtypeStruct((bt, st, P_COLS), BF16),
                   jax.ShapeDtypeStruct((bt, A_KV_HEADS, A_HEAD_DIM, st), BF16)],
        compiler_params=_cparams("parallel", "parallel"),
        name="projection_rope" if use_rope else "projection_ctx",
    )(h, w_p, g_q, g_k, cos_a, sin_a, cos_b, sin_b)


def _flash(q_ref, kc_ref, vct_ref, kl_ref, vlt_ref, st_ref, acc_ref, tk):
    n_chunks = kl_ref.shape[1] // tk

    def latent_scores(c, slot):
        st_ref[slot] = _dot_nt(kl_ref[0, c * tk:(c + 1) * tk, :], q_ref[...])

    def update(c, slot, m, l):
        st = st_ref[slot]
        m_new = jnp.maximum(m, jnp.max(st, axis=0, keepdims=True))
        alpha = jnp.exp2(m - m_new)
        pt = jnp.exp2(st - m_new)
        acc_ref[...] = alpha * acc_ref[...] + _dot(vlt_ref[0, 0, :, c * tk:(c + 1) * tk],
                                                  pt.astype(BF16))
        return m_new, alpha * l + jnp.sum(pt, axis=0, keepdims=True)

    latent_scores(0, 0)
    st = _dot_nt(kc_ref[0], q_ref[...])
    m = jnp.max(st, axis=0, keepdims=True)
    pt = jnp.exp2(st - m)
    l = jnp.sum(pt, axis=0, keepdims=True)
    acc_ref[...] = _dot(vct_ref[0, 0], pt.astype(BF16))

    for c in range(n_chunks):
        if c + 1 < n_chunks:
            latent_scores(c + 1, (c + 1) & 1)
        m, l = update(c, c & 1, m, l)
    return (acc_ref[...] / l).T


def _flash_scratch(m_rows, dk, dv, tk):
    return [pltpu.VMEM((m_rows, dk), BF16),
            pltpu.VMEM((2, tk, m_rows), F32),
            pltpu.VMEM((dv, m_rows), F32)]


class _SideCasts:
    def __init__(self, arrays, grid):
        self.n_steps = math.prod(grid)
        strides = [math.prod(grid[k + 1:]) for k in range(len(grid))]
        step = lambda *g: sum(i * st for i, st in zip(g, strides))
        self.shapes = [a.shape for a in arrays]
        self.views, self.in_specs, self.out_specs, self.out_shapes = [], [], [], []
        for a in arrays:
            cols = a.shape[-1]
            rows = math.prod(a.shape[:-1]) // self.n_steps
            assert rows * self.n_steps == math.prod(a.shape[:-1]) and rows % 16 == 0
            self.views.append(a.reshape(self.n_steps, rows, cols))
            spec = pl.BlockSpec((1, rows, cols), lambda *g: (step(*g), 0, 0))
            self.in_specs.append(spec)
            self.out_specs.append(spec)
            self.out_shapes.append(jax.ShapeDtypeStruct((self.n_steps, rows, cols), BF16))

    @staticmethod
    def run(src_refs, dst_refs):
        for src, dst in zip(src_refs, dst_refs):
            dst[...] = src[...].astype(BF16)

    def restore(self, outs):
        return [o.reshape(shape) for o, shape in zip(outs, self.shapes)]


def _gqa_kernel(tk, n_side, q_ref, kl_ref, vlt_ref, kc_ref, vct_ref, *refs):
    side_src, (o_ref, *side_dst) = refs[:n_side], refs[n_side:2 * n_side + 1]
    qs_ref, st_ref, acc_ref = refs[2 * n_side + 1:]
    _SideCasts.run(side_src, side_dst)
    group = A_HEADS // A_KV_HEADS
    tq = q_ref.shape[1]
    for g in range(group):
        qs_ref[g * tq:(g + 1) * tq, :] = q_ref[0, :, g * A_HEAD_DIM:(g + 1) * A_HEAD_DIM]
    out = _flash(qs_ref, kc_ref, vct_ref, kl_ref, vlt_ref, st_ref, acc_ref, tk)
    for g in range(group):
        o_ref[0, :, g * A_HEAD_DIM:(g + 1) * A_HEAD_DIM] = out[g * tq:(g + 1) * tq].astype(BF16)


def _gqa_attention(p_lat, vat_lat, p_ctx, vat_ctx, tq, tk, cast_arrays):
    b, s, _ = p_lat.shape
    n_ctx = p_ctx.shape[1]
    group_w = (A_HEADS // A_KV_HEADS) * A_HEAD_DIM
    grid = (b, A_KV_HEADS, s // tq)
    side = _SideCasts(cast_arrays, grid)
    oa, *cast = pl.pallas_call(
        functools.partial(_gqa_kernel, tk, len(cast_arrays)),
        grid=grid,
        in_specs=[pl.BlockSpec((1, tq, group_w), lambda bb, k, i: (bb, i, P_QA // group_w + k)),
                  pl.BlockSpec((1, s, LANES), lambda bb, k, i: (bb, 0, P_KA // LANES + k)),
                  pl.BlockSpec((1, 1, A_HEAD_DIM, s), lambda bb, k, i: (bb, k, 0, 0)),
                  pl.BlockSpec((1, n_ctx, LANES), lambda bb, k, i: (bb, 0, P_KA // LANES + k)),
                  pl.BlockSpec((1, 1, A_HEAD_DIM, n_ctx), lambda bb, k, i: (bb, k, 0, 0)),
                  *side.in_specs],
        out_specs=[pl.BlockSpec((1, tq, group_w), lambda bb, k, i: (bb, i, k)), *side.out_specs],
        out_shape=[jax.ShapeDtypeStruct((b, s, A_HEADS * A_HEAD_DIM), BF16), *side.out_shapes],
        scratch_shapes=_flash_scratch((A_HEADS // A_KV_HEADS) * tq, A_HEAD_DIM, A_HEAD_DIM, tk),
        compiler_params=_cparams("parallel", "parallel", "parallel"),
        name="gqa_attention",
    )(p_lat, p_lat, vat_lat, p_ctx, vat_ctx, *side.views)
    return oa, side.restore(cast)


def _mla_kernel(tk, n_side, qn_ref, qr_ref, kl_ref, vlt_ref, kc_ref, vct_ref, *refs):
    side_src, (o_ref, *side_dst) = refs[:n_side], refs[n_side:2 * n_side + 1]
    qs_ref, st_ref, acc_ref = refs[2 * n_side + 1:]
    _SideCasts.run(side_src, side_dst)
    hd = pl.program_id(1)
    qr = qr_ref[0]
    lane = lax.broadcasted_iota(jnp.int32, qr.shape, 1)
    qs_ref[:, :B_NOPE] = qn_ref[0]
    qs_ref[:, B_NOPE:] = jnp.where((lane >> 6) == (hd & 1), qr, jnp.zeros_like(qr))
    o_ref[0] = _flash(qs_ref, kc_ref, vct_ref, kl_ref, vlt_ref, st_ref, acc_ref, tk).astype(BF16)


def _mla_attention(p_lat, kb_lat, vbt_lat, kb_ctx, vbt_ctx, tq, tk, cast_arrays):
    b, s, _ = p_lat.shape
    n_ctx = kb_ctx.shape[1]
    grid = (b, B_HEADS, s // tq)
    side = _SideCasts(cast_arrays, grid)
    ob, *cast = pl.pallas_call(
        functools.partial(_mla_kernel, tk, len(cast_arrays)),
        grid=grid,
        in_specs=[pl.BlockSpec((1, tq, LANES), lambda bb, h, i: (bb, i, P_QBN // LANES + h)),
                  pl.BlockSpec((1, tq, LANES), lambda bb, h, i: (bb, i, P_QBR // LANES + h // 2)),
                  pl.BlockSpec((1, s, B_KEY_W), lambda bb, h, i: (bb, 0, h)),
                  pl.BlockSpec((1, 1, B_V, s), lambda bb, h, i: (bb, h, 0, 0)),
                  pl.BlockSpec((1, n_ctx, B_KEY_W), lambda bb, h, i: (bb, 0, h)),
                  pl.BlockSpec((1, 1, B_V, n_ctx), lambda bb, h, i: (bb, h, 0, 0)),
                  *side.in_specs],
        out_specs=[pl.BlockSpec((1, tq, B_V), lambda bb, h, i: (bb, i, h)), *side.out_specs],
        out_shape=[jax.ShapeDtypeStruct((b, s, B_HEADS * B_V), BF16), *side.out_shapes],
        scratch_shapes=_flash_scratch(tq, B_KEY_W, B_V, tk),
        compiler_params=_cparams("parallel", "parallel", "parallel"),
        name="mla_attention",
    )(p_lat, p_lat, kb_lat, vbt_lat, kb_ctx, vbt_ctx, *side.views)
    return ob, side.restore(cast)


def _merge_kernel(oa_ref, ob_ref, ga_ref, gb_ref, x_ref, g1_ref, sh_ref, sc_ref, gn_ref,
                  wa_ref, wb_ref, wo_ref, wr_ref, br_ref, xn_ref, h2_ref, rt_ref, rtt_ref, cnt_ref):
    ya = _dot(oa_ref[0], wa_ref[...])
    yb = _dot(ob_ref[0], wb_ref[...])
    mix = (ga_ref[0].astype(F32) * ya + gb_ref[0].astype(F32) * yb).astype(BF16)
    xn = x_ref[0] + g1_ref[0] * _dot(mix, wo_ref[...])
    xn_ref[0] = xn
    h2 = _rms(xn, gn_ref[...]) * (1.0 + sc_ref[0]) + sh_ref[0]
    h2_ref[0] = h2
    logits = _dot(h2.astype(BF16), wr_ref[...]) + br_ref[...]
    lt = logits.T
    gl = [lt[g:g + 1, :] for g in range(N_GROUPS)]
    gmax = functools.reduce(jnp.maximum, gl)
    gsum = functools.reduce(lambda a, b_: a + b_, [jnp.exp(v - gmax) for v in gl])
    g_val = 1.0 / gsum
    g_idx = jnp.full(gmax.shape, N_GROUPS - 1, jnp.int32)
    for g in range(N_GROUPS - 2, -1, -1):
        g_idx = jnp.where(gl[g] == gmax, g, g_idx)
    el = []
    for e in range(EXPERTS_PER_GROUP):
        v = lt[N_GROUPS + e:N_GROUPS + e + 1, :]
        for g in range(1, N_GROUPS):
            row = N_GROUPS + g * EXPERTS_PER_GROUP + e
            v = jnp.where(g_idx == g, lt[row:row + 1, :], v)
        el.append(v)
    emax = functools.reduce(jnp.maximum, el)
    i1 = jnp.full(emax.shape, EXPERTS_PER_GROUP - 1, jnp.int32)
    for e in range(EXPERTS_PER_GROUP - 2, -1, -1):
        i1 = jnp.where(el[e] == emax, e, i1)
    neg = jnp.full(emax.shape, -jnp.inf, F32)
    el2 = [jnp.where(i1 == e, neg, el[e]) for e in range(EXPERTS_PER_GROUP)]
    emax2 = functools.reduce(jnp.maximum, el2)
    i2 = jnp.full(emax.shape, EXPERTS_PER_GROUP - 1, jnp.int32)
    for e in range(EXPERTS_PER_GROUP - 2, -1, -1):
        i2 = jnp.where(el2[e] == emax2, e, i2)
    p2 = jnp.exp(emax2 - emax)
    w1 = g_val / (1.0 + p2)
    w2 = g_val * p2 / (1.0 + p2)
    e1 = g_idx * EXPERTS_PER_GROUP + i1
    e2 = g_idx * EXPERTS_PER_GROUP + i2
    zero = jnp.zeros_like(w1)
    rt = jnp.concatenate([e1.astype(F32), e2.astype(F32), w1, w2, zero, zero, zero, zero], axis=0)
    rt_ref[...] = rt
    rtt_ref[...] = jnp.concatenate([rt, jnp.zeros((LANES - SUBLANES, rt.shape[1]), F32)], axis=0).T
    lane = lax.broadcasted_iota(jnp.int32, (1, LANES), 1)
    counts = jnp.zeros((1, LANES), F32)
    for e in range(N_EXPERTS):
        hits = jnp.where(e1 == e, 1.0, 0.0) + jnp.where(e2 == e, 1.0, 0.0)
        counts = counts + jnp.where(lane == e, jnp.sum(hits, axis=1, keepdims=True), 0.0)
    cnt_ref[0] = jnp.broadcast_to(counts, (SUBLANES, LANES))


def _merge_route(oa, ob, p_lat, x, g1, sh2, sc2, g_ffn, w_a, w_b, w_o, w_r, b_r, tm):
    b, s, d = x.shape
    n_i = s // tm
    row = lambda bb, i: (bb, i, 0)
    per_b = lambda bb, i: (bb, 0, 0)
    fixed = lambda bb, i: (0, 0)
    return pl.pallas_call(
        _merge_kernel,
        grid=(b, n_i),
        in_specs=[pl.BlockSpec((1, tm, oa.shape[2]), row),
                  pl.BlockSpec((1, tm, ob.shape[2]), row),
                  pl.BlockSpec((1, tm, d), lambda bb, i: (bb, i, P_GA // d)),
                  pl.BlockSpec((1, tm, d), lambda bb, i: (bb, i, P_GB // d)),
                  pl.BlockSpec((1, tm, d), row),
                  pl.BlockSpec((1, 1, d), per_b),
                  pl.BlockSpec((1, 1, d), per_b),
                  pl.BlockSpec((1, 1, d), per_b),
                  pl.BlockSpec((1, d), fixed),
                  _resident(w_a.shape),
                  _resident(w_b.shape),
                  _resident(w_o.shape),
                  _resident(w_r.shape),
                  pl.BlockSpec((1, LANES), fixed)],
        out_specs=[pl.BlockSpec((1, tm, d), row),
                   pl.BlockSpec((1, tm, d), row),
                   pl.BlockSpec((SUBLANES, tm), lambda bb, i: (0, bb * n_i + i)),
                   pl.BlockSpec((tm, LANES), lambda bb, i: (bb * n_i + i, 0)),
                   pl.BlockSpec((1, SUBLANES, LANES), lambda bb, i: (bb * n_i + i, 0, 0))],
        out_shape=[jax.ShapeDtypeStruct((b, s, d), F32),
                   jax.ShapeDtypeStruct((b, s, d), F32),
                   jax.ShapeDtypeStruct((SUBLANES, b * s), F32),
                   jax.ShapeDtypeStruct((b * s, LANES), F32),
                   jax.ShapeDtypeStruct((b * n_i, SUBLANES, LANES), F32)],
        compiler_params=_cparams("parallel", "parallel"),
        name="merge_route",
    )(oa, ob, p_lat, p_lat, x, g1, sh2, sc2, g_ffn, w_a, w_b, w_o, w_r, b_r)


def _moe_kernel(tm, na_ref, te_ref, src_ref, dst_ref, h_hbm, wg_ref, wu_ref, wd_ref,
                y_hbm, xbuf, ybuf, gsem, ssem):
    t = pl.program_id(0)
    n_active = na_ref[0]
    phase = lax.rem(t, MOE_SLOTS)

    def gather_start(tile, slot):
        for r in range(tm):
            pltpu.make_async_copy(h_hbm.at[pl.ds(src_ref[tile * tm + r], 1)],
                                  xbuf.at[slot, pl.ds(r, 1)], gsem.at[slot]).start(priority=r % 2)

    def gather_wait(slot):
        pltpu.make_async_copy(h_hbm.at[pl.ds(0, tm)], xbuf.at[slot], gsem.at[slot]).wait()

    def scatter_start(tile, slot):
        for r in range(tm):
            pltpu.make_async_copy(ybuf.at[slot, pl.ds(r, 1)],
                                  y_hbm.at[pl.ds(dst_ref[(tile + MOE_LEAD_TILES) * tm + r], 1)],
                                  ssem.at[slot]).start(priority=r % 2)

    def scatter_wait(slot):
        pltpu.make_async_copy(ybuf.at[slot], y_hbm.at[pl.ds(0, tm)], ssem.at[slot]).wait()

    @pl.when(t == 0)
    def _():
        for k in range(1, MOE_LEAD_TILES + 1):
            ybuf[(-k) % MOE_SLOTS] = jnp.zeros(ybuf.shape[1:], ybuf.dtype)
        for k in range(2, MOE_LEAD_TILES + 1):
            scatter_start(-k, (-k) % MOE_SLOTS)
        gather_start(0, 0)

    def step(cur):
        nxt = (cur + 1) % MOE_SLOTS
        gather_wait(cur)
        gate = _dot(xbuf[cur].astype(BF16), wg_ref[0])
        gather_start(t + 1, nxt)
        up = _dot(xbuf[cur].astype(BF16), wu_ref[0])
        scatter_start(t - 1, (cur - 1) % MOE_SLOTS)
        hid = (gate * jax.nn.sigmoid(gate) * up).astype(BF16)
        ybuf[cur] = _dot(hid, wd_ref[0])
        scatter_wait((cur - MOE_LEAD_TILES) % MOE_SLOTS)

    def drain(cur):
        scatter_start(t, cur)
        for k in range(MOE_LEAD_TILES - 1, -1, -1):
            scatter_wait((cur - k) % MOE_SLOTS)
        gather_wait((cur + 1) % MOE_SLOTS)

    for cur in range(MOE_SLOTS):
        mine = phase == cur
        pl.when(mine & (t < n_active))(functools.partial(step, cur))
        pl.when(mine & (t == n_active - 1))(functools.partial(drain, cur))


def _moe(h2, n_active, tile_expert, src_tok, dst_row, w_gate, w_up, w_down, tm):
    n, d = h2.shape
    n_tiles = tile_expert.shape[0]
    w_in_map = lambda t, na, te, sr, ds: (te[t], 0, 0)
    grid_spec = pltpu.PrefetchScalarGridSpec(
        num_scalar_prefetch=4,
        grid=(n_tiles,),
        in_specs=[pl.BlockSpec(memory_space=pl.ANY),
                  pl.BlockSpec((1, d, D_EXPERT), w_in_map),
                  pl.BlockSpec((1, d, D_EXPERT), w_in_map),
                  pl.BlockSpec((1, D_EXPERT, d), w_in_map)],
        out_specs=pl.BlockSpec(memory_space=pl.ANY),
        scratch_shapes=[pltpu.VMEM((MOE_SLOTS, tm, d), F32),
                        pltpu.VMEM((MOE_SLOTS, tm, d), F32),
                        pltpu.SemaphoreType.DMA((MOE_SLOTS,)),
                        pltpu.SemaphoreType.DMA((MOE_SLOTS,))],
    )
    return pl.pallas_call(
        functools.partial(_moe_kernel, tm),
        grid_spec=grid_spec,
        out_shape=jax.ShapeDtypeStruct((2 * n + MOE_LEAD_TILES * tm, d), F32),
        compiler_params=_cparams("arbitrary"),
        name="moe_experts",
    )(n_active, tile_expert, src_tok, dst_row, h2, w_gate, w_up, w_down)


def _moe_plan(eid, counts, tm):
    n = eid.shape[1]
    pairs = 2 * n
    n_tiles = pairs // tm + N_EXPERTS
    _, order = lax.sort((eid.reshape(pairs), lax.iota(jnp.int32, pairs)), num_keys=1)
    padded = ((counts + tm - 1) // tm) * tm
    pad_end = jnp.cumsum(padded)
    pad_start = pad_end - padded
    raw_start = jnp.cumsum(counts) - counts
    tile_row0 = jnp.arange(n_tiles, dtype=jnp.int32) * tm
    tile_expert = jnp.minimum(
        jnp.sum(tile_row0[:, None] >= pad_end[None, :], axis=1), N_EXPERTS - 1).astype(jnp.int32)
    shift = (raw_start - pad_start)[tile_expert]
    limit = (pad_start + counts)[tile_expert]
    pos = jnp.arange(n_tiles * tm, dtype=jnp.int32).reshape(n_tiles, tm)
    valid = pos < limit[:, None]
    pair = jnp.where(valid, order[jnp.clip(pos + shift[:, None], 0, pairs - 1)], 0).reshape(-1)
    valid = valid.reshape(-1)
    dummy = pairs + jnp.arange(MOE_LEAD_TILES * tm, dtype=jnp.int32)
    pad_dst = (pairs + ((pos // tm) % MOE_LEAD_TILES) * tm + pos % tm).reshape(-1)
    src_tok = jnp.concatenate([pair % n, jnp.zeros((tm,), jnp.int32)]).astype(jnp.int32)
    dst_row = jnp.concatenate([dummy, jnp.where(valid, pair, pad_dst)]).astype(jnp.int32)
    n_active = (pad_end[-1:] // tm).astype(jnp.int32)
    return n_active, tile_expert, src_tok, dst_row


def _final_kernel(x_ref, y0_ref, y1_ref, rtt_ref, g2_ref, gn_ref, o_ref):
    wt = rtt_ref[...]
    moe = wt[:, 2:3] * y0_ref[...] + wt[:, 3:4] * y1_ref[...]
    o_ref[0] = _rms(x_ref[0] + g2_ref[0] * moe, gn_ref[...])


def _final(x_new, y2, route, g2, g_final, tm):
    b, s, d = x_new.shape
    row = lambda bb, i: (bb, i, 0)
    n_i = s // tm
    return pl.pallas_call(
        _final_kernel,
        grid=(b, n_i),
        in_specs=[pl.BlockSpec((1, tm, d), row),
                  pl.BlockSpec((tm, d), lambda bb, i: (bb * n_i + i, 0)),
                  pl.BlockSpec((tm, d), lambda bb, i: (b * n_i + bb * n_i + i, 0)),
                  pl.BlockSpec((tm, LANES), lambda bb, i: (bb * n_i + i, 0)),
                  pl.BlockSpec((1, 1, d), lambda bb, i: (bb, 0, 0)),
                  pl.BlockSpec((1, d), lambda bb, i: (0, 0))],
        out_specs=pl.BlockSpec((1, tm, d), row),
        out_shape=jax.ShapeDtypeStruct((b, s, d), F32),
        compiler_params=_cparams("parallel", "parallel"),
        name="final_norm",
    )(x_new, y2, y2, route, g2, g_final)


def _rope_tables(n_tokens, rot_dim):
    rows = n_tokens // GRID_W
    row = np.repeat(np.arange(rows), GRID_W).astype(np.float64)
    col = np.tile(np.arange(GRID_W), rows).astype(np.float64)
    n_freq = rot_dim // 4
    freqs = ROPE_THETA ** (-np.arange(n_freq, dtype=np.float64) / n_freq)
    ang = np.concatenate([row[:, None] * freqs, col[:, None] * freqs], axis=-1)
    cos, sin = np.cos(ang), np.sin(ang)
    reps = LANES // rot_dim
    cos_t = np.tile(np.concatenate([cos, cos], axis=-1), (1, reps))
    sin_t = np.tile(np.concatenate([-sin, sin], axis=-1), (1, reps))
    return jnp.asarray(cos_t, F32), jnp.asarray(sin_t, F32)


def kernel(x, c, ctx, c_ctx, w_mod, b_mod, norm_mix, norm_ffn, w_in, a_q_norm, a_k_norm, b_kv_norm, w_ukv, w_br_a, w_br_b, w_out, w_group, b_group, w_router, b_router, w_e_gate, w_e_up, w_e_down, norm_final):
    b, s, d = x.shape
    n_ctx = ctx.shape[1]
    assert w_mod.shape[0] == 1, "single-layer block"
    assert s % GRID_W == 0

    wt = jnp.transpose(w_in[0])
    qa_w = A_HEADS * A_HEAD_DIM
    kv_w = A_KV_HEADS * A_HEAD_DIM
    o_qb = qa_w + 2 * kv_w
    qb_w = B_HEADS * (B_NOPE + B_ROPE)
    o_ckv = o_qb + qb_w
    o_kr = o_ckv + B_KV_RANK
    o_gl = o_kr + B_ROPE
    qb_rows = o_qb + (B_NOPE + B_ROPE) * np.arange(B_HEADS)[:, None]
    p_rows = np.concatenate(
        [np.arange(o_gl, wt.shape[0]), np.arange(o_qb),
         (qb_rows + np.arange(B_NOPE)).reshape(-1),
         (qb_rows + B_NOPE + np.arange(B_ROPE)).reshape(-1)])
    p_blocks = p_rows.reshape(-1, W_REGROUP_ROWS)
    assert (np.diff(p_blocks, axis=1) == 1).all() and (p_blocks[:, 0] % W_REGROUP_ROWS == 0).all()
    w_blocks = wt.reshape(-1, W_REGROUP_ROWS, d)
    regroup = (w_blocks, jnp.asarray(p_blocks[:, 0] // W_REGROUP_ROWS, jnp.int32))
    ck_blocks = np.concatenate([np.arange(o_ckv, o_gl), np.arange(o_kr, o_gl)])[::W_REGROUP_ROWS]
    w_ck = _cast_rows(w_blocks, jnp.asarray(ck_blocks // W_REGROUP_ROWS, jnp.int32))
    w_ck = w_ck.reshape(-1, d)
    w_kv = w_ukv[0].reshape(B_KV_RANK, B_HEADS, B_NOPE + B_V)
    w_k = w_kv[:, :, :B_NOPE].reshape(B_KV_RANK, B_HEADS * B_NOPE).astype(BF16)
    w_vt = w_kv[:, :, B_NOPE:].reshape(B_KV_RANK, B_HEADS * B_V).T.astype(BF16)
    w_r = jnp.concatenate(
        [w_group[0], jnp.transpose(w_router[0], (1, 0, 2)).reshape(d, N_EXPERTS),
         jnp.zeros((d, LANES - N_GROUPS - N_EXPERTS), F32)], axis=1).astype(BF16)
    b_r = jnp.concatenate([b_group[0], b_router[0].reshape(N_EXPERTS),
                           jnp.zeros((LANES - N_GROUPS - N_EXPERTS,), F32)])[None, :]

    cond = jnp.concatenate([c, c_ctx[None, :], jnp.zeros((SUBLANES - b - 1, d), F32)], axis=0)
    mod = _modulation(cond, w_mod[0], b_mod[0])
    mx = mod[:b].reshape(b, N_MOD, 1, d)
    sh1, sc1, g1, sh2, sc2, g2 = [mx[:, k] for k in range(N_MOD)]
    mc = jnp.broadcast_to(mod[b].reshape(N_MOD, 1, 1, d), (N_MOD, b, 1, d))
    csh1, csc1 = mc[0], mc[1]

    cos_a, sin_a = _rope_tables(s, A_HEAD_DIM)
    cos_b, sin_b = _rope_tables(s, B_ROPE)
    g_mix = norm_mix[0][None, :]
    g_kv = b_kv_norm[0][None, :]
    g_q = a_q_norm[0][None, :]
    g_k = a_k_norm[0][None, :]

    tm = min(TOKEN_TM, s)
    h, kb_lat, vbt_lat, (w_p,) = _prologue(x, sh1, sc1, g_mix, w_ck, g_kv, w_k, w_vt, cos_b, sin_b,
                                           True, tm, regroup)
    w_p = w_p.reshape(P_COLS, d)
    p_lat, vat_lat = _projection(h, w_p, g_q, g_k, cos_a, sin_a, cos_b, sin_b, True, tm)

    tab_c = jnp.zeros((n_ctx, LANES), F32)
    hc, kb_ctx, vbt_ctx, _ = _prologue(ctx, csh1, csc1, g_mix, w_ck, g_kv, w_k, w_vt, tab_c, tab_c,
                                       False, n_ctx)
    p_ctx, vat_ctx = _projection(hc, w_p, g_q, g_k, tab_c, tab_c, tab_c, tab_c, False, n_ctx)

    oa, (w_down, w_a, w_b, w_o) = _gqa_attention(
        p_lat, vat_lat, p_ctx, vat_ctx, min(GQA_TQ, s), min(GQA_TK, s),
        [w_e_down[0], w_br_a[0], w_br_b[0], w_out[0]])
    ob, (w_gate, w_up) = _mla_attention(p_lat, kb_lat, vbt_lat, kb_ctx, vbt_ctx, min(MLA_TQ, s),
                                        min(MLA_TK, s), [w_e_gate[0], w_e_up[0]])

    x_new, h2, route, route_t, cnt = _merge_route(
        oa, ob, p_lat, x, g1, sh2, sc2, norm_ffn[0][None, :],
        w_a, w_b, w_o, w_r, b_r,
        min(MERGE_TM, s))

    n = b * s
    counts = jnp.sum(cnt[:, 0, :N_EXPERTS], axis=0).astype(jnp.int32)
    plan = _moe_plan(route[0:2].astype(jnp.int32), counts, MOE_TM)
    y2 = _moe(h2.reshape(n, d), *plan, w_gate, w_up, w_down, MOE_TM)

    return _final(x_new, y2, route_t, g2, norm_final[None, :], tm)
```

```python
import functools
import math

import jax
import jax.numpy as jnp
import numpy as np
from jax import lax
from jax.experimental import pallas as pl
from jax.experimental.pallas import tpu as pltpu

GRID_W = 64
ROPE_THETA = 10000.0
EPS = 1e-6
A_HEADS = 8
A_KV_HEADS = 2
A_HEAD_DIM = 128
B_HEADS = 8
B_NOPE = 128
B_ROPE = 64
B_V = 128
B_KV_RANK = 512
N_GROUPS = 4
EXPERTS_PER_GROUP = 4
N_EXPERTS = N_GROUPS * EXPERTS_PER_GROUP
D_EXPERT = 1024
N_MOD = 6

LANES = 128
SUBLANES = 8
V7X_VMEM_LIMIT_BYTES = 56 * 1024 * 1024

BF16 = jnp.bfloat16
F32 = jnp.float32

P_GA = 0
P_GB = 2048
P_QA = 4096
P_KA = 5120
P_VA = 5376
P_QBN = 5632
P_QBR = 6656
P_COLS = 7168
PROJ_TN = 512

LOG2_E = math.log2(math.e)
A_SCORE_SCALE = LOG2_E / math.sqrt(A_HEAD_DIM)
B_SCORE_SCALE = LOG2_E / math.sqrt(B_NOPE + B_ROPE)

B_KEY_W = B_NOPE + 2 * B_ROPE
W_REGROUP_ROWS = 32
MOD_TN = 1024

TOKEN_TM = 512
MERGE_TM = 256
GQA_TQ = 512
MLA_TQ = 2048
GQA_TK = 512
MLA_TK = 512
MOE_TM = 256
MOE_LEAD_TILES = 1
MOE_SLOTS = MOE_LEAD_TILES + 1


def _cparams(*sem):
    return pltpu.CompilerParams(dimension_semantics=sem, vmem_limit_bytes=V7X_VMEM_LIMIT_BYTES)


def _dot(a, b):
    return jnp.dot(a, b, preferred_element_type=F32)


def _dot_nt(a, b):
    return lax.dot_general(a, b, (((1,), (1,)), ((), ())), preferred_element_type=F32)


def _resident(shape):
    return pl.BlockSpec(shape, lambda *_: (0,) * len(shape), pipeline_mode=pl.Buffered(1))


def _rms(v, gain):
    return v * lax.rsqrt(jnp.mean(v * v, axis=-1, keepdims=True) + EPS) * gain


def _mod_kernel(c_ref, w_ref, b_ref, o_ref):
    c = c_ref[...]
    s = (c * jax.nn.sigmoid(c)).astype(BF16)
    o_ref[...] = _dot(s, w_ref[...].astype(BF16)) + b_ref[...]


def _modulation(cond, w_mod, b_mod):
    rows, d = cond.shape
    n = w_mod.shape[1]
    tn = MOD_TN
    return pl.pallas_call(
        _mod_kernel,
        grid=(n // tn,),
        in_specs=[pl.BlockSpec((rows, d), lambda j: (0, 0)),
                  pl.BlockSpec((d, tn), lambda j: (0, j)),
                  pl.BlockSpec((1, tn), lambda j: (0, j))],
        out_specs=pl.BlockSpec((rows, tn), lambda j: (0, j)),
        out_shape=jax.ShapeDtypeStruct((rows, n), F32),
        compiler_params=_cparams("parallel"),
        name="modulation",
    )(cond, w_mod, b_mod.reshape(1, n))


def _cast_rows_kernel(tbl_ref, src_ref, dst_ref):
    del tbl_ref
    dst_ref[...] = src_ref[...].astype(BF16)


def _cast_rows(blocks, table):
    blk = (1,) + blocks.shape[1:]
    grid_spec = pltpu.PrefetchScalarGridSpec(
        num_scalar_prefetch=1,
        grid=(table.shape[0],),
        in_specs=[pl.BlockSpec(blk, lambda i, tbl: (tbl[i], 0, 0))],
        out_specs=pl.BlockSpec(blk, lambda i, tbl: (i, 0, 0)))
    return pl.pallas_call(
        _cast_rows_kernel,
        grid_spec=grid_spec,
        out_shape=jax.ShapeDtypeStruct((table.shape[0],) + blocks.shape[1:], BF16),
        compiler_params=_cparams("parallel"),
        name="latent_key_weight",
    )(table, blocks)


def _swap_halves_64(v):
    lane = lax.broadcasted_iota(jnp.int32, v.shape, 1)
    return jnp.where((lane & 63) < 32, pltpu.roll(v, LANES - 32, 1), pltpu.roll(v, 32, 1))


def _pre_kernel(use_rope, n_side, tbl_ref, x_ref, sh_ref, sc_ref, g_ref, wck_ref, gkv_ref, wk_ref,
                wvt_ref, cb_ref, sb_ref, *refs):
    del tbl_ref
    side_src, (h_ref, kb_ref, vbt_ref), side_dst = refs[:n_side], refs[n_side:n_side + 3], refs[n_side + 3:]
    for k, src in enumerate(side_src):
        side_dst[0][k] = src[0].astype(BF16)
    xf = x_ref[0]
    h = _rms(xf, g_ref[...]) * (1.0 + sc_ref[0]) + sh_ref[0]
    hb = h.astype(BF16)
    h_ref[0] = hb
    p = _dot_nt(hb, wck_ref[...])
    cn = _rms(p[:, :B_KV_RANK], gkv_ref[...]).astype(BF16)
    kr2 = p[:, B_KV_RANK:]
    if use_rope:
        kr2 = kr2 * cb_ref[...] + _swap_halves_64(kr2) * sb_ref[...]
    kr2 = kr2.astype(BF16)
    kbn = _dot(cn, wk_ref[...]).astype(BF16)
    vbt_ref[0] = _dot_nt(wvt_ref[...], cn).astype(BF16).reshape(vbt_ref.shape[1:])
    for hd in range(B_HEADS):
        kb_ref[0, :, hd * B_KEY_W:hd * B_KEY_W + B_NOPE] = kbn[:, hd * B_NOPE:(hd + 1) * B_NOPE]
        kb_ref[0, :, hd * B_KEY_W + B_NOPE:(hd + 1) * B_KEY_W] = kr2


def _prologue(x, shift, scale, gain, w_ck, g_kv, w_k, w_vt, cos_b, sin_b, use_rope, tm,
              regroup=None):
    bt, st, d = x.shape
    n_i = st // tm
    row = lambda b, i, tbl: (b, i, 0)
    per_b = lambda b, i, tbl: (b, 0, 0)
    fixed = lambda b, i, tbl: (0, 0)
    tab = lambda b, i, tbl: (i, 0)
    side_in, side_out, side_shapes, side_args = [], [], [], []
    table = jnp.zeros((1,), jnp.int32)
    if regroup is not None:
        blocks, table = regroup
        per_step = table.shape[0] // (bt * n_i)
        assert per_step * bt * n_i == table.shape[0]
        for k in range(per_step):
            side_in.append(pl.BlockSpec(
                (1,) + blocks.shape[1:],
                lambda b, i, tbl, k=k: (tbl[(b * n_i + i) * per_step + k], 0, 0)))
            side_args.append(blocks)
        side_out.append(pl.BlockSpec((per_step,) + blocks.shape[1:],
                                     lambda b, i, tbl: (b * n_i + i, 0, 0)))
        side_shapes.append(jax.ShapeDtypeStruct((table.shape[0],) + blocks.shape[1:], BF16))
    grid_spec = pltpu.PrefetchScalarGridSpec(
        num_scalar_prefetch=1,
        grid=(bt, n_i),
        in_specs=[pl.BlockSpec((1, tm, d), row),
                  pl.BlockSpec((1, 1, d), per_b),
                  pl.BlockSpec((1, 1, d), per_b),
                  pl.BlockSpec((1, d), fixed),
                  _resident(w_ck.shape),
                  pl.BlockSpec((1, B_KV_RANK), fixed),
                  _resident(w_k.shape),
                  _resident(w_vt.shape),
                  pl.BlockSpec((tm, LANES), tab),
                  pl.BlockSpec((tm, LANES), tab),
                  *side_in],
        out_specs=[pl.BlockSpec((1, tm, d), row),
                   pl.BlockSpec((1, tm, B_HEADS * B_KEY_W), row),
                   pl.BlockSpec((1, B_HEADS, B_V, tm), lambda b, i, tbl: (b, 0, 0, i)),
                   *side_out],
    )
    h, kb, vbt, *side = pl.pallas_call(
        functools.partial(_pre_kernel, use_rope, len(side_in)),
        grid_spec=grid_spec,
        out_shape=[jax.ShapeDtypeStruct((bt, st, d), BF16),
                   jax.ShapeDtypeStruct((bt, st, B_HEADS * B_KEY_W), BF16),
                   jax.ShapeDtypeStruct((bt, B_HEADS, B_V, st), BF16),
                   *side_shapes],
        compiler_params=_cparams("parallel", "parallel"),
        name="prologue_rope" if use_rope else "prologue_ctx",
    )(table, x, shift, scale, gain, w_ck, g_kv, w_k, w_vt, cos_b, sin_b, *side_args)
    return h, kb, vbt, side


def _proj_kernel(use_rope, h_ref, w_ref, gq_ref, gk_ref, ca_ref, sa_ref, cb_ref, sb_ref,
                 o_ref, vat_ref):
    h = h_ref[0]
    n_blk = PROJ_TN // LANES

    def rope_a(v):
        if not use_rope:
            return v
        return v * ca_ref[...] + pltpu.roll(v, A_HEAD_DIM // 2, 1) * sa_ref[...]

    def rope_b(v):
        if not use_rope:
            return v
        return v * cb_ref[...] + _swap_halves_64(v) * sb_ref[...]

    for j in range(P_COLS // PROJ_TN):
        c0 = j * PROJ_TN
        acc = _dot_nt(h, w_ref[c0:c0 + PROJ_TN, :])

        def blk(k):
            return acc[:, k * LANES:(k + 1) * LANES]

        def put(k, v):
            o_ref[0, :, c0 + k * LANES:c0 + (k + 1) * LANES] = v.astype(BF16)

        if c0 < P_QA:
            o_ref[0, :, c0:c0 + PROJ_TN] = jax.nn.sigmoid(acc).astype(BF16)
        elif c0 < P_KA:
            for k in range(n_blk):
                put(k, rope_a(_rms(blk(k), gq_ref[...])) * A_SCORE_SCALE)
        elif c0 < P_QBN:
            for k in range(A_KV_HEADS):
                put(k, rope_a(_rms(blk(k), gk_ref[...])))
            for k in range(A_KV_HEADS, n_blk):
                put(k, blk(k))
                vat_ref[0, k - A_KV_HEADS] = blk(k).T.astype(BF16)
        elif c0 < P_QBR:
            o_ref[0, :, c0:c0 + PROJ_TN] = (acc * B_SCORE_SCALE).astype(BF16)
        else:
            for k in range(n_blk):
                put(k, rope_b(blk(k)) * B_SCORE_SCALE)


def _projection(h, w_p, g_q, g_k, cos_a, sin_a, cos_b, sin_b, use_rope, tm):
    bt, st, d = h.shape
    fixed = lambda b, i: (0, 0)
    tab = lambda b, i: (i, 0)
    return pl.pallas_call(
        functools.partial(_proj_kernel, use_rope),
        grid=(bt, st // tm),
        in_specs=[pl.BlockSpec((1, tm, d), lambda b, i: (b, i, 0)),
                  _resident(w_p.shape),
                  pl.BlockSpec((1, LANES), fixed),
                  pl.BlockSpec((1, LANES), fixed),
                  pl.BlockSpec((tm, LANES), tab),
                  pl.BlockSpec((tm, LANES), tab),
                  pl.BlockSpec((tm, LANES), tab),
                  pl.BlockSpec((tm, LANES), tab)],
        out_specs=[pl.BlockSpec((1, tm, P_COLS), lambda b, i: (b, i, 0)),
                   pl.BlockSpec((1, A_KV_HEADS, A_HEAD_DIM, tm), lambda b, i: (b, 0, 0, i))],
        out_shape=[jax.ShapeDtypeStruct((bt, st, P_COLS), BF16),
                   jax.ShapeDtypeStruct((bt, A_KV_HEADS, A_HEAD_DIM, st), BF16)],
        compiler_params=_cparams("parallel", "parallel"),
        name="projection_rope" if use_rope else "projection_ctx",
    )(h, w_p, g_q, g_k, cos_a, sin_a, cos_b, sin_b)


def _flash(q_ref, kc_ref, vct_ref, kl_ref, vlt_ref, st_ref, acc_ref, tk):
    n_chunks = kl_ref.shape[1] // tk

    def latent_scores(c, slot):
        st_ref[slot] = _dot_nt(kl_ref[0, c * tk:(c + 1) * tk, :], q_ref[...])

    def update(c, slot, m, l):
        st = st_ref[slot]
        m_new = jnp.maximum(m, jnp.max(st, axis=0, keepdims=True))
        alpha = jnp.exp2(m - m_new)
        pt = jnp.exp2(st - m_new)
        acc_ref[...] = alpha * acc_ref[...] + _dot(vlt_ref[0, 0, :, c * tk:(c + 1) * tk],
                                                  pt.astype(BF16))
        return m_new, alpha * l + jnp.sum(pt, axis=0, keepdims=True)

    latent_scores(0, 0)
    st = _dot_nt(kc_ref[0], q_ref[...])
    m = jnp.max(st, axis=0, keepdims=True)
    pt = jnp.exp2(st - m)
    l = jnp.sum(pt, axis=0, keepdims=True)
    acc_ref[...] = _dot(vct_ref[0, 0], pt.astype(BF16))

    for c in range(n_chunks):
        if c + 1 < n_chunks:
            latent_scores(c + 1, (c + 1) & 1)
        m, l = update(c, c & 1, m, l)
    return (acc_ref[...] / l).T


def _flash_scratch(m_rows, dk, dv, tk):
    return [pltpu.VMEM((m_rows, dk), BF16),
            pltpu.VMEM((2, tk, m_rows), F32),
            pltpu.VMEM((dv, m_rows), F32)]


class _SideCasts:
    def __init__(self, arrays, grid):
        self.n_steps = math.prod(grid)
        strides = [math.prod(grid[k + 1:]) for k in range(len(grid))]
        step = lambda *g: sum(i * st for i, st in zip(g, strides))
        self.shapes = [a.shape for a in arrays]
        self.views, self.in_specs, self.out_specs, self.out_shapes = [], [], [], []
        for a in arrays:
            cols = a.shape[-1]
            rows = math.prod(a.shape[:-1]) // self.n_steps
            assert rows * self.n_steps == math.prod(a.shape[:-1]) and rows % 16 == 0
            self.views.append(a.reshape(self.n_steps, rows, cols))
            spec = pl.BlockSpec((1, rows, cols), lambda *g: (step(*g), 0, 0))
            self.in_specs.append(spec)
            self.out_specs.append(spec)
            self.out_shapes.append(jax.ShapeDtypeStruct((self.n_steps, rows, cols), BF16))

    @staticmethod
    def run(src_refs, dst_refs):
        for src, dst in zip(src_refs, dst_refs):
            dst[...] = src[...].astype(BF16)

    def restore(self, outs):
        return [o.reshape(shape) for o, shape in zip(outs, self.shapes)]


def _gqa_kernel(tk, n_side, q_ref, kl_ref, vlt_ref, kc_ref, vct_ref, *refs):
    side_src, (o_ref, *side_dst) = refs[:n_side], refs[n_side:2 * n_side + 1]
    qs_ref, st_ref, acc_ref = refs[2 * n_side + 1:]
    _SideCasts.run(side_src, side_dst)
    group = A_HEADS // A_KV_HEADS
    tq = q_ref.shape[1]
    for g in range(group):
        qs_ref[g * tq:(g + 1) * tq, :] = q_ref[0, :, g * A_HEAD_DIM:(g + 1) * A_HEAD_DIM]
    out = _flash(qs_ref, kc_ref, vct_ref, kl_ref, vlt_ref, st_ref, acc_ref, tk)
    for g in range(group):
        o_ref[0, :, g * A_HEAD_DIM:(g + 1) * A_HEAD_DIM] = out[g * tq:(g + 1) * tq].astype(BF16)


def _gqa_attention(p_lat, vat_lat, p_ctx, vat_ctx, tq, tk, cast_arrays):
    b, s, _ = p_lat.shape
    n_ctx = p_ctx.shape[1]
    group_w = (A_HEADS // A_KV_HEADS) * A_HEAD_DIM
    grid = (b, A_KV_HEADS, s // tq)
    side = _SideCasts(cast_arrays, grid)
    oa, *cast = pl.pallas_call(
        functools.partial(_gqa_kernel, tk, len(cast_arrays)),
        grid=grid,
        in_specs=[pl.BlockSpec((1, tq, group_w), lambda bb, k, i: (bb, i, P_QA // group_w + k)),
                  pl.BlockSpec((1, s, LANES), lambda bb, k, i: (bb, 0, P_KA // LANES + k)),
                  pl.BlockSpec((1, 1, A_HEAD_DIM, s), lambda bb, k, i: (bb, k, 0, 0)),
                  pl.BlockSpec((1, n_ctx, LANES), lambda bb, k, i: (bb, 0, P_KA // LANES + k)),
                  pl.BlockSpec((1, 1, A_HEAD_DIM, n_ctx), lambda bb, k, i: (bb, k, 0, 0)),
                  *side.in_specs],
        out_specs=[pl.BlockSpec((1, tq, group_w), lambda bb, k, i: (bb, i, k)), *side.out_specs],
        out_shape=[jax.ShapeDtypeStruct((b, s, A_HEADS * A_HEAD_DIM), BF16), *side.out_shapes],
        scratch_shapes=_flash_scratch((A_HEADS // A_KV_HEADS) * tq, A_HEAD_DIM, A_HEAD_DIM, tk),
        compiler_params=_cparams("parallel", "parallel", "parallel"),
        name="gqa_attention",
    )(p_lat, p_lat, vat_lat, p_ctx, vat_ctx, *side.views)
    return oa, side.restore(cast)


def _mla_kernel(tk, n_side, qn_ref, qr_ref, kl_ref, vlt_ref, kc_ref, vct_ref, *refs):
    side_src, (o_ref, *side_dst) = refs[:n_side], refs[n_side:2 * n_side + 1]
    qs_ref, st_ref, acc_ref = refs[2 * n_side + 1:]
    _SideCasts.run(side_src, side_dst)
    hd = pl.program_id(1)
    qr = qr_ref[0]
    lane = lax.broadcasted_iota(jnp.int32, qr.shape, 1)
    qs_ref[:, :B_NOPE] = qn_ref[0]
    qs_ref[:, B_NOPE:] = jnp.where((lane >> 6) == (hd & 1), qr, jnp.zeros_like(qr))
    o_ref[0] = _flash(qs_ref, kc_ref, vct_ref, kl_ref, vlt_ref, st_ref, acc_ref, tk).astype(BF16)


def _mla_attention(p_lat, kb_lat, vbt_lat, kb_ctx, vbt_ctx, tq, tk, cast_arrays):
    b, s, _ = p_lat.shape
    n_ctx = kb_ctx.shape[1]
    grid = (b, B_HEADS, s // tq)
    side = _SideCasts(cast_arrays, grid)
    ob, *cast = pl.pallas_call(
        functools.partial(_mla_kernel, tk, len(cast_arrays)),
        grid=grid,
        in_specs=[pl.BlockSpec((1, tq, LANES), lambda bb, h, i: (bb, i, P_QBN // LANES + h)),
                  pl.BlockSpec((1, tq, LANES), lambda bb, h, i: (bb, i, P_QBR // LANES + h // 2)),
                  pl.BlockSpec((1, s, B_KEY_W), lambda bb, h, i: (bb, 0, h)),
                  pl.BlockSpec((1, 1, B_V, s), lambda bb, h, i: (bb, h, 0, 0)),
                  pl.BlockSpec((1, n_ctx, B_KEY_W), lambda bb, h, i: (bb, 0, h)),
                  pl.BlockSpec((1, 1, B_V, n_ctx), lambda bb, h, i: (bb, h, 0, 0)),
                  *side.in_specs],
        out_specs=[pl.BlockSpec((1, tq, B_V), lambda bb, h, i: (bb, i, h)), *side.out_specs],
        out_shape=[jax.ShapeDtypeStruct((b, s, B_HEADS * B_V), BF16), *side.out_shapes],
        scratch_shapes=_flash_scratch(tq, B_KEY_W, B_V, tk),
        compiler_params=_cparams("parallel", "parallel", "parallel"),
        name="mla_attention",
    )(p_lat, p_lat, kb_lat, vbt_lat, kb_ctx, vbt_ctx, *side.views)
    return ob, side.restore(cast)


def _merge_kernel(oa_ref, ob_ref, ga_ref, gb_ref, x_ref, g1_ref, sh_ref, sc_ref, gn_ref,
                  wa_ref, wb_ref, wo_ref, wr_ref, br_ref, xn_ref, h2_ref, rt_ref, rtt_ref, cnt_ref):
    ya = _dot(oa_ref[0], wa_ref[...])
    yb = _dot(ob_ref[0], wb_ref[...])
    mix = (ga_ref[0].astype(F32) * ya + gb_ref[0].astype(F32) * yb).astype(BF16)
    xn = x_ref[0] + g1_ref[0] * _dot(mix, wo_ref[...])
    xn_ref[0] = xn
    h2 = _rms(xn, gn_ref[...]) * (1.0 + sc_ref[0]) + sh_ref[0]
    h2_ref[0] = h2
    logits = _dot(h2.astype(BF16), wr_ref[...]) + br_ref[...]
    lt = logits.T
    gl = [lt[g:g + 1, :] for g in range(N_GROUPS)]
    gmax = functools.reduce(jnp.maximum, gl)
    gsum = functools.reduce(lambda a, b_: a + b_, [jnp.exp(v - gmax) for v in gl])
    g_val = 1.0 / gsum
    g_idx = jnp.full(gmax.shape, N_GROUPS - 1, jnp.int32)
    for g in range(N_GROUPS - 2, -1, -1):
        g_idx = jnp.where(gl[g] == gmax, g, g_idx)
    el = []
    for e in range(EXPERTS_PER_GROUP):
        v = lt[N_GROUPS + e:N_GROUPS + e + 1, :]
        for g in range(1, N_GROUPS):
            row = N_GROUPS + g * EXPERTS_PER_GROUP + e
            v = jnp.where(g_idx == g, lt[row:row + 1, :], v)
        el.append(v)
    emax = functools.reduce(jnp.maximum, el)
    i1 = jnp.full(emax.shape, EXPERTS_PER_GROUP - 1, jnp.int32)
    for e in range(EXPERTS_PER_GROUP - 2, -1, -1):
        i1 = jnp.where(el[e] == emax, e, i1)
    neg = jnp.full(emax.shape, -jnp.inf, F32)
    el2 = [jnp.where(i1 == e, neg, el[e]) for e in range(EXPERTS_PER_GROUP)]
    emax2 = functools.reduce(jnp.maximum, el2)
    i2 = jnp.full(emax.shape, EXPERTS_PER_GROUP - 1, jnp.int32)
    for e in range(EXPERTS_PER_GROUP - 2, -1, -1):
        i2 = jnp.where(el2[e] == emax2, e, i2)
    p2 = jnp.exp(emax2 - emax)
    w1 = g_val / (1.0 + p2)
    w2 = g_val * p2 / (1.0 + p2)
    e1 = g_idx * EXPERTS_PER_GROUP + i1
    e2 = g_idx * EXPERTS_PER_GROUP + i2
    zero = jnp.zeros_like(w1)
    rt = jnp.concatenate([e1.astype(F32), e2.astype(F32), w1, w2, zero, zero, zero, zero], axis=0)
    rt_ref[...] = rt
    rtt_ref[...] = jnp.concatenate([rt, jnp.zeros((LANES - SUBLANES, rt.shape[1]), F32)], axis=0).T
    lane = lax.broadcasted_iota(jnp.int32, (1, LANES), 1)
    counts = jnp.zeros((1, LANES), F32)
    for e in range(N_EXPERTS):
        hits = jnp.where(e1 == e, 1.0, 0.0) + jnp.where(e2 == e, 1.0, 0.0)
        counts = counts + jnp.where(lane == e, jnp.sum(hits, axis=1, keepdims=True), 0.0)
    cnt_ref[0] = jnp.broadcast_to(counts, (SUBLANES, LANES))


def _merge_route(oa, ob, p_lat, x, g1, sh2, sc2, g_ffn, w_a, w_b, w_o, w_r, b_r, tm):
    b, s, d = x.shape
    n_i = s // tm
    row = lambda bb, i: (bb, i, 0)
    per_b = lambda bb, i: (bb, 0, 0)
    fixed = lambda bb, i: (0, 0)
    return pl.pallas_call(
        _merge_kernel,
        grid=(b, n_i),
        in_specs=[pl.BlockSpec((1, tm, oa.shape[2]), row),
                  pl.BlockSpec((1, tm, ob.shape[2]), row),
                  pl.BlockSpec((1, tm, d), lambda bb, i: (bb, i, P_GA // d)),
                  pl.BlockSpec((1, tm, d), lambda bb, i: (bb, i, P_GB // d)),
                  pl.BlockSpec((1, tm, d), row),
                  pl.BlockSpec((1, 1, d), per_b),
                  pl.BlockSpec((1, 1, d), per_b),
                  pl.BlockSpec((1, 1, d), per_b),
                  pl.BlockSpec((1, d), fixed),
                  _resident(w_a.shape),
                  _resident(w_b.shape),
                  _resident(w_o.shape),
                  _resident(w_r.shape),
                  pl.BlockSpec((1, LANES), fixed)],
        out_specs=[pl.BlockSpec((1, tm, d), row),
                   pl.BlockSpec((1, tm, d), row),
                   pl.BlockSpec((SUBLANES, tm), lambda bb, i: (0, bb * n_i + i)),
                   pl.BlockSpec((tm, LANES), lambda bb, i: (bb * n_i + i, 0)),
                   pl.BlockSpec((1, SUBLANES, LANES), lambda bb, i: (bb * n_i + i, 0, 0))],
        out_shape=[jax.ShapeDtypeStruct((b, s, d), F32),
                   jax.ShapeDtypeStruct((b, s, d), F32),
                   jax.ShapeDtypeStruct((SUBLANES, b * s), F32),
                   jax.ShapeDtypeStruct((b * s, LANES), F32),
                   jax.ShapeDtypeStruct((b * n_i, SUBLANES, LANES), F32)],
        compiler_params=_cparams("parallel", "parallel"),
        name="merge_route",
    )(oa, ob, p_lat, p_lat, x, g1, sh2, sc2, g_ffn, w_a, w_b, w_o, w_r, b_r)


def _moe_kernel(tm, na_ref, te_ref, src_ref, dst_ref, h_hbm, wg_ref, wu_ref, wd_ref,
                y_hbm, xbuf, ybuf, gsem, ssem):
    t = pl.program_id(0)
    n_active = na_ref[0]
    phase = lax.rem(t, MOE_SLOTS)

    def gather_start(tile, slot):
        for r in range(tm):
            pltpu.make_async_copy(h_hbm.at[pl.ds(src_ref[tile * tm + r], 1)],
                                  xbuf.at[slot, pl.ds(r, 1)], gsem.at[slot]).start(priority=1)

    def gather_wait(slot):
        pltpu.make_async_copy(h_hbm.at[pl.ds(0, tm)], xbuf.at[slot], gsem.at[slot]).wait()

    def scatter_start(tile, slot):
        for r in range(tm):
            pltpu.make_async_copy(ybuf.at[slot, pl.ds(r, 1)],
                                  y_hbm.at[pl.ds(dst_ref[(tile + MOE_LEAD_TILES) * tm + r], 1)],
                                  ssem.at[slot]).start(priority=r % 2)

    def scatter_wait(slot):
        pltpu.make_async_copy(ybuf.at[slot], y_hbm.at[pl.ds(0, tm)], ssem.at[slot]).wait()

    @pl.when(t == 0)
    def _():
        for k in range(1, MOE_LEAD_TILES + 1):
            ybuf[(-k) % MOE_SLOTS] = jnp.zeros(ybuf.shape[1:], ybuf.dtype)
        for k in range(2, MOE_LEAD_TILES + 1):
            scatter_start(-k, (-k) % MOE_SLOTS)
        gather_start(0, 0)

    def step(cur):
        nxt = (cur + 1) % MOE_SLOTS
        gather_wait(cur)
        gate = _dot(xbuf[cur].astype(BF16), wg_ref[0])
        gather_start(t + 1, nxt)
        up = _dot(xbuf[cur].astype(BF16), wu_ref[0])
        scatter_start(t - 1, (cur - 1) % MOE_SLOTS)
        hid = (gate * jax.nn.sigmoid(gate) * up).astype(BF16)
        ybuf[cur] = _dot(hid, wd_ref[0])
        scatter_wait((cur - MOE_LEAD_TILES) % MOE_SLOTS)

    def drain(cur):
        scatter_start(t, cur)
        for k in range(MOE_LEAD_TILES - 1, -1, -1):
            scatter_wait((cur - k) % MOE_SLOTS)
        gather_wait((cur + 1) % MOE_SLOTS)

    for cur in range(MOE_SLOTS):
        mine = phase == cur
        pl.when(mine & (t < n_active))(functools.partial(step, cur))
        pl.when(mine & (t == n_active - 1))(functools.partial(drain, cur))


def _moe(h2, n_active, tile_expert, src_tok, dst_row, w_gate, w_up, w_down, tm):
    n, d = h2.shape
    n_tiles = tile_expert.shape[0]
    w_in_map = lambda t, na, te, sr, ds: (te[t], 0, 0)
    grid_spec = pltpu.PrefetchScalarGridSpec(
        num_scalar_prefetch=4,
        grid=(n_tiles,),
        in_specs=[pl.BlockSpec(memory_space=pl.ANY),
                  pl.BlockSpec((1, d, D_EXPERT), w_in_map),
                  pl.BlockSpec((1, d, D_EXPERT), w_in_map),
                  pl.BlockSpec((1, D_EXPERT, d), w_in_map)],
        out_specs=pl.BlockSpec(memory_space=pl.ANY),
        scratch_shapes=[pltpu.VMEM((MOE_SLOTS, tm, d), F32),
                        pltpu.VMEM((MOE_SLOTS, tm, d), F32),
                        pltpu.SemaphoreType.DMA((MOE_SLOTS,)),
                        pltpu.SemaphoreType.DMA((MOE_SLOTS,))],
    )
    return pl.pallas_call(
        functools.partial(_moe_kernel, tm),
        grid_spec=grid_spec,
        out_shape=jax.ShapeDtypeStruct((2 * n + MOE_LEAD_TILES * tm, d), F32),
        compiler_params=_cparams("arbitrary"),
        name="moe_experts",
    )(n_active, tile_expert, src_tok, dst_row, h2, w_gate, w_up, w_down)


def _moe_plan(eid, counts, tm):
    n = eid.shape[1]
    pairs = 2 * n
    n_tiles = pairs // tm + N_EXPERTS
    _, order = lax.sort((eid.reshape(pairs), lax.iota(jnp.int32, pairs)), num_keys=1)
    padded = ((counts + tm - 1) // tm) * tm
    pad_end = jnp.cumsum(padded)
    pad_start = pad_end - padded
    raw_start = jnp.cumsum(counts) - counts
    tile_row0 = jnp.arange(n_tiles, dtype=jnp.int32) * tm
    tile_expert = jnp.minimum(
        jnp.sum(tile_row0[:, None] >= pad_end[None, :], axis=1), N_EXPERTS - 1).astype(jnp.int32)
    shift = (raw_start - pad_start)[tile_expert]
    limit = (pad_start + counts)[tile_expert]
    pos = jnp.arange(n_tiles * tm, dtype=jnp.int32).reshape(n_tiles, tm)
    valid = pos < limit[:, None]
    pair = jnp.where(valid, order[jnp.clip(pos + shift[:, None], 0, pairs - 1)], 0).reshape(-1)
    valid = valid.reshape(-1)
    dummy = pairs + jnp.arange(MOE_LEAD_TILES * tm, dtype=jnp.int32)
    pad_dst = (pairs + ((pos // tm) % MOE_LEAD_TILES) * tm + pos % tm).reshape(-1)
    src_tok = jnp.concatenate([pair % n, jnp.zeros((tm,), jnp.int32)]).astype(jnp.int32)
    dst_row = jnp.concatenate([dummy, jnp.where(valid, pair, pad_dst)]).astype(jnp.int32)
    n_active = (pad_end[-1:] // tm).astype(jnp.int32)
    return n_active, tile_expert, src_tok, dst_row


def _final_kernel(x_ref, y0_ref, y1_ref, rtt_ref, g2_ref, gn_ref, o_ref):
    wt = rtt_ref[...]
    moe = wt[:, 2:3] * y0_ref[...] + wt[:, 3:4] * y1_ref[...]
    o_ref[0] = _rms(x_ref[0] + g2_ref[0] * moe, gn_ref[...])


def _final(x_new, y2, route, g2, g_final, tm):
    b, s, d = x_new.shape
    row = lambda bb, i: (bb, i, 0)
    n_i = s // tm
    return pl.pallas_call(
        _final_kernel,
        grid=(b, n_i),
        in_specs=[pl.BlockSpec((1, tm, d), row),
                  pl.BlockSpec((tm, d), lambda bb, i: (bb * n_i + i, 0)),
                  pl.BlockSpec((tm, d), lambda bb, i: (b * n_i + bb * n_i + i, 0)),
                  pl.BlockSpec((tm, LANES), lambda bb, i: (bb * n_i + i, 0)),
                  pl.BlockSpec((1, 1, d), lambda bb, i: (bb, 0, 0)),
                  pl.BlockSpec((1, d), lambda bb, i: (0, 0))],
        out_specs=pl.BlockSpec((1, tm, d), row),
        out_shape=jax.ShapeDtypeStruct((b, s, d), F32),
        compiler_params=_cparams("parallel", "parallel"),
        name="final_norm",
    )(x_new, y2, y2, route, g2, g_final)


def _rope_tables(n_tokens, rot_dim):
    rows = n_tokens // GRID_W
    row = np.repeat(np.arange(rows), GRID_W).astype(np.float64)
    col = np.tile(np.arange(GRID_W), rows).astype(np.float64)
    n_freq = rot_dim // 4
    freqs = ROPE_THETA ** (-np.arange(n_freq, dtype=np.float64) / n_freq)
    ang = np.concatenate([row[:, None] * freqs, col[:, None] * freqs], axis=-1)
    cos, sin = np.cos(ang), np.sin(ang)
    reps = LANES // rot_dim
    cos_t = np.tile(np.concatenate([cos, cos], axis=-1), (1, reps))
    sin_t = np.tile(np.concatenate([-sin, sin], axis=-1), (1, reps))
    return jnp.asarray(cos_t, F32), jnp.asarray(sin_t, F32)


def kernel(x, c, ctx, c_ctx, w_mod, b_mod, norm_mix, norm_ffn, w_in, a_q_norm, a_k_norm, b_kv_norm, w_ukv, w_br_a, w_br_b, w_out, w_group, b_group, w_router, b_router, w_e_gate, w_e_up, w_e_down, norm_final):
    b, s, d = x.shape
    n_ctx = ctx.shape[1]
    assert w_mod.shape[0] == 1, "single-layer block"
    assert s % GRID_W == 0

    wt = jnp.transpose(w_in[0])
    qa_w = A_HEADS * A_HEAD_DIM
    kv_w = A_KV_HEADS * A_HEAD_DIM
    o_qb = qa_w + 2 * kv_w
    qb_w = B_HEADS * (B_NOPE + B_ROPE)
    o_ckv = o_qb + qb_w
    o_kr = o_ckv + B_KV_RANK
    o_gl = o_kr + B_ROPE
    qb_rows = o_qb + (B_NOPE + B_ROPE) * np.arange(B_HEADS)[:, None]
    p_rows = np.concatenate(
        [np.arange(o_gl, wt.shape[0]), np.arange(o_qb),
         (qb_rows + np.arange(B_NOPE)).reshape(-1),
         (qb_rows + B_NOPE + np.arange(B_ROPE)).reshape(-1)])
    p_blocks = p_rows.reshape(-1, W_REGROUP_ROWS)
    assert (np.diff(p_blocks, axis=1) == 1).all() and (p_blocks[:, 0] % W_REGROUP_ROWS == 0).all()
    w_blocks = wt.reshape(-1, W_REGROUP_ROWS, d)
    regroup = (w_blocks, jnp.asarray(p_blocks[:, 0] // W_REGROUP_ROWS, jnp.int32))
    ck_blocks = np.concatenate([np.arange(o_ckv, o_gl), np.arange(o_kr, o_gl)])[::W_REGROUP_ROWS]
    w_ck = _cast_rows(w_blocks, jnp.asarray(ck_blocks // W_REGROUP_ROWS, jnp.int32))
    w_ck = w_ck.reshape(-1, d)
    w_kv = w_ukv[0].reshape(B_KV_RANK, B_HEADS, B_NOPE + B_V)
    w_k = w_kv[:, :, :B_NOPE].reshape(B_KV_RANK, B_HEADS * B_NOPE).astype(BF16)
    w_vt = w_kv[:, :, B_NOPE:].reshape(B_KV_RANK, B_HEADS * B_V).T.astype(BF16)
    w_r = jnp.concatenate(
        [w_group[0], jnp.transpose(w_router[0], (1, 0, 2)).reshape(d, N_EXPERTS),
         jnp.zeros((d, LANES - N_GROUPS - N_EXPERTS), F32)], axis=1).astype(BF16)
    b_r = jnp.concatenate([b_group[0], b_router[0].reshape(N_EXPERTS),
                           jnp.zeros((LANES - N_GROUPS - N_EXPERTS,), F32)])[None, :]

    cond = jnp.concatenate([c, c_ctx[None, :], jnp.zeros((SUBLANES - b - 1, d), F32)], axis=0)
    mod = _modulation(cond, w_mod[0], b_mod[0])
    mx = mod[:b].reshape(b, N_MOD, 1, d)
    sh1, sc1, g1, sh2, sc2, g2 = [mx[:, k] for k in range(N_MOD)]
    mc = jnp.broadcast_to(mod[b].reshape(N_MOD, 1, 1, d), (N_MOD, b, 1, d))
    csh1, csc1 = mc[0], mc[1]

    cos_a, sin_a = _rope_tables(s, A_HEAD_DIM)
    cos_b, sin_b = _rope_tables(s, B_ROPE)
    g_mix = norm_mix[0][None, :]
    g_kv = b_kv_norm[0][None, :]
    g_q = a_q_norm[0][None, :]
    g_k = a_k_norm[0][None, :]

    tm = min(TOKEN_TM, s)
    h, kb_lat, vbt_lat, (w_p,) = _prologue(x, sh1, sc1, g_mix, w_ck, g_kv, w_k, w_vt, cos_b, sin_b,
                                           True, tm, regroup)
    w_p = w_p.reshape(P_COLS, d)
    p_lat, vat_lat = _projection(h, w_p, g_q, g_k, cos_a, sin_a, cos_b, sin_b, True, tm)

    tab_c = jnp.zeros((n_ctx, LANES), F32)
    hc, kb_ctx, vbt_ctx, _ = _prologue(ctx, csh1, csc1, g_mix, w_ck, g_kv, w_k, w_vt, tab_c, tab_c,
                                       False, n_ctx)
    p_ctx, vat_ctx = _projection(hc, w_p, g_q, g_k, tab_c, tab_c, tab_c, tab_c, False, n_ctx)

    oa, (w_down, w_a, w_b, w_o) = _gqa_attention(
        p_lat, vat_lat, p_ctx, vat_ctx, min(GQA_TQ, s), min(GQA_TK, s),
        [w_e_down[0], w_br_a[0], w_br_b[0], w_out[0]])
    ob, (w_gate, w_up) = _mla_attention(p_lat, kb_lat, vbt_lat, kb_ctx, vbt_ctx, min(MLA_TQ, s),
                                        min(MLA_TK, s), [w_e_gate[0], w_e_up[0]])

    x_new, h2, route, route_t, cnt = _merge_route(
        oa, ob, p_lat, x, g1, sh2, sc2, norm_ffn[0][None, :],
        w_a, w_b, w_o, w_r, b_r,
        min(MERGE_TM, s))

    n = b * s
    counts = jnp.sum(cnt[:, 0, :N_EXPERTS], axis=0).astype(jnp.int32)
    plan = _moe_plan(route[0:2].astype(jnp.int32), counts, MOE_TM)
    y2 = _moe(h2.reshape(n, d), *plan, w_gate, w_up, w_down, MOE_TM)

    return _final(x_new, y2, route_t, g2, norm_final[None, :], tm)
```
